```python
import math
import jax, jax.numpy as jnp
from jax import lax
import numpy as np

D_MODEL = 1024
BATCH = 8
SEQ = 2048
DEPTH = 2
DEC_BATCH = 128
DEC_SEQ = 1
PAST_LEN = 16384
PAGE_SIZE = 128

N_EVEN = (DEPTH + 1) // 2
N_ODD = DEPTH // 2
CHUNK = 64
CONV_W = 4
EPS = 1e-6

DN_HEADS = 4
DN_DK = 128
DN_DV = 128
DN_QK = DN_HEADS * DN_DK
DN_V = DN_HEADS * DN_DV
DN_CONV_DIM = 2 * DN_QK + DN_V

SSM_HEADS = 8
SSM_P = 64
SSM_INNER = SSM_HEADS * SSM_P
SSM_GROUPS = 2
SSM_N = 128
SSM_CONV_DIM = SSM_INNER + 2 * SSM_GROUPS * SSM_N

MIX_EVEN = DN_V + SSM_INNER
IN_EVEN = DN_CONV_DIM + 2 * DN_HEADS + DN_V + SSM_INNER + SSM_CONV_DIM + SSM_HEADS

GLA_HEADS = 4
GLA_DK = 128
GLA_DV = 256
GLA_K = GLA_HEADS * GLA_DK
GLA_V = GLA_HEADS * GLA_DV
GLA_LR = 16
GLA_NORMALIZER = 16.0
IN_ODD = 2 * GLA_K + 2 * GLA_V + GLA_LR

PEER_HEADS = 8
PEER_DQ = 256
PEER_NKEYS = 128
PEER_TOPK = 16
PEER_N = PEER_NKEYS * PEER_NKEYS
PEER_BLOCK = 256

kernel_name = "hybrid_gdn_ssd_gla_peer_step"


def rmsnorm(x, w):
    xf = x.astype(jnp.float32)
    y = xf * lax.rsqrt(jnp.mean(xf * xf, axis=-1, keepdims=True) + EPS)
    return (y * w.astype(jnp.float32)).astype(x.dtype)


def l2norm(x):
    xf = x.astype(jnp.float32)
    return xf * lax.rsqrt(jnp.sum(xf * xf, axis=-1, keepdims=True) + EPS)


def causal_conv(x, buf, w):
    xc = jnp.concatenate([buf.astype(x.dtype), x], axis=1)
    y = lax.conv_general_dilated(xc, w[:, None, :].astype(x.dtype), (1,), 'VALID',
                                 dimension_numbers=('NWC', 'WIO', 'NWC'),
                                 feature_group_count=x.shape[-1])
    return y, xc[:, -(CONV_W - 1):]


def to_chunks(x, c):
    b, L = x.shape[:2]
    nc = -(-L // c)
    x = jnp.pad(x, [(0, 0), (0, nc * c - L)] + [(0, 0)] * (x.ndim - 2))
    x = x.reshape((b, nc, c) + x.shape[2:])
    return jnp.swapaxes(jnp.swapaxes(x, 2, 3), 0, 1)


def from_chunks(x, L):
    x = jnp.swapaxes(jnp.swapaxes(x, 0, 1), 2, 3)
    b, nc, c = x.shape[:3]
    return x.reshape((b, nc * c) + x.shape[3:])[:, :L]


def gated_delta_rule(q, k, v, beta, g, s0):
    L = q.shape[1]
    dv = v.shape[-1]
    c = min(CHUNK, L)
    q, k, v, beta, g = (to_chunks(t, c) for t in (q, k, v, beta, g))
    gc = jnp.cumsum(g, axis=-1)
    causal = jnp.tril(jnp.ones((c, c), bool))
    strict = jnp.tril(jnp.ones((c, c), bool), -1)
    decay = jnp.exp(jnp.where(causal, gc[..., :, None] - gc[..., None, :], -jnp.inf))
    kb = k * beta[..., None]
    m = jnp.where(strict, jnp.einsum('nbhik,nbhjk->nbhij', kb, k) * decay, 0.0)
    rhs = jnp.concatenate([v * beta[..., None], kb * jnp.exp(gc)[..., None]], axis=-1)
    sol = lax.linalg.triangular_solve(m + jnp.eye(c, dtype=m.dtype), rhs, left_side=True,
                                      lower=True, unit_diagonal=True)
    u, w = sol[..., :dv], sol[..., dv:]
    qk = jnp.where(causal, jnp.einsum('nbhik,nbhjk->nbhij', q, k) * decay, 0.0)
    q_in = q * jnp.exp(gc)[..., None]
    k_out = k * jnp.exp(gc[..., -1:] - gc)[..., None]
    g_last = jnp.exp(gc[..., -1])[..., None, None]

    def step(s, xs):
        u_n, w_n, qk_n, qi_n, ko_n, gl_n = xs
        v_new = u_n - jnp.einsum('bhik,bhkv->bhiv', w_n, s)
        o = jnp.einsum('bhik,bhkv->bhiv', qi_n, s) + jnp.einsum('bhij,bhjv->bhiv', qk_n, v_new)
        s = s * gl_n + jnp.einsum('bhik,bhiv->bhkv', ko_n, v_new)
        return s, o

    s, o = lax.scan(step, s0, (u, w, qk, q_in, k_out, g_last))
    return from_chunks(o, L), s


def ssd_scan(x, dt, a, bm, cm, s0):
    L = x.shape[1]
    c = min(CHUNK, L)
    rep = x.shape[2] // bm.shape[2]
    bh = jnp.repeat(bm, rep, axis=2)
    ch = jnp.repeat(cm, rep, axis=2)
    x, dt, bh, ch = (to_chunks(t, c) for t in (x, dt, bh, ch))
    ac = jnp.cumsum(dt * a[:, None], axis=-1)
    causal = jnp.tril(jnp.ones((c, c), bool))
    decay = jnp.exp(jnp.where(causal, ac[..., :, None] - ac[..., None, :], -jnp.inf))
    xdt = x * dt[..., None]
    scores = jnp.einsum('nbhis,nbhjs->nbhij', ch, bh) * decay
    y_diag = jnp.einsum('nbhij,nbhjp->nbhip', scores, xdt)
    c_in = ch * jnp.exp(ac)[..., None]
    b_out = bh * jnp.exp(ac[..., -1:] - ac)[..., None]
    a_last = jnp.exp(ac[..., -1])[..., None, None]

    def step(s, xs):
        yd, ci, bo, xd, al = xs
        y = yd + jnp.einsum('bhis,bhps->bhip', ci, s)
        s = s * al + jnp.einsum('bhip,bhis->bhps', xd, bo)
        return s, y

    s, y = lax.scan(step, s0, (y_diag, c_in, b_out, xdt, a_last))
    return from_chunks(y, L), s


def gla_scan(q, k, v, gk, s0):
    L = q.shape[1]
    c = min(CHUNK, L)
    q, k, v, gk = (to_chunks(t, c) for t in (q, k, v, gk))
    bc = jnp.cumsum(gk, axis=-2)
    causal = jnp.tril(jnp.ones((c, c), bool))[:, :, None]

    def step(s, xs):
        q_n, k_n, v_n, b_n = xs
        o = jnp.einsum('bhik,bhkv->bhiv', q_n * jnp.exp(b_n), s)
        dec = jnp.exp(jnp.where(causal, b_n[..., :, None, :] - b_n[..., None, :, :], -jnp.inf))
        att = jnp.einsum('bhik,bhjk,bhijk->bhij', q_n, k_n, dec)
        o = o + jnp.einsum('bhij,bhjv->bhiv', att, v_n)
        b_last = b_n[..., -1, :]
        s = s * jnp.exp(b_last)[..., None] + jnp.einsum(
            'bhik,bhiv->bhkv', k_n * jnp.exp(b_last[..., None, :] - b_n), v_n)
        return s, o

    s, o = lax.scan(step, s0, (q, k, v, bc))
    return from_chunks(o, L), s


def even_mixer(h, conv_dn, s_dn, conv_ssm, s_ssm, w_in, w_out, dn_conv_w, dn_a_log,
               dn_dt_bias, dn_norm_w, ssm_conv_w, ssm_conv_b, ssm_a_log, ssm_dt_bias,
               ssm_d, ssm_norm_w):
    f32 = jnp.float32
    bsz, L, _ = h.shape
    proj = h @ w_in
    sizes = [DN_CONV_DIM, DN_HEADS, DN_HEADS, DN_V, SSM_INNER, SSM_CONV_DIM]
    dn_qkv, dn_a, dn_b, dn_gate, ssm_z, ssm_xbc, ssm_dt = jnp.split(
        proj, [int(s) for s in np.cumsum(sizes)], axis=-1)

    qkv_pre, new_conv_dn = causal_conv(dn_qkv, conv_dn, dn_conv_w)
    qkv = jax.nn.silu(qkv_pre.astype(f32))
    q, k, v = jnp.split(qkv, [DN_QK, 2 * DN_QK], axis=-1)
    q = l2norm(q.reshape(bsz, L, DN_HEADS, DN_DK)) * (DN_DK ** -0.5)
    k = l2norm(k.reshape(bsz, L, DN_HEADS, DN_DK))
    v = v.reshape(bsz, L, DN_HEADS, DN_DV)
    beta = jax.nn.sigmoid(dn_b.astype(f32))
    g = -jnp.exp(dn_a_log.astype(f32)) * jax.nn.softplus(dn_a.astype(f32) + dn_dt_bias.astype(f32))
    o_dn, s_dn_new = gated_delta_rule(q, k, v, beta, g, s_dn.astype(f32))
    o_dn = rmsnorm(o_dn, dn_norm_w) * jax.nn.silu(
        dn_gate.astype(f32).reshape(bsz, L, DN_HEADS, DN_DV))

    xbc_pre, new_conv_ssm = causal_conv(ssm_xbc, conv_ssm, ssm_conv_w)
    xbc = jax.nn.silu(xbc_pre.astype(f32) + ssm_conv_b.astype(f32))
    xs, bm, cm = jnp.split(xbc, [SSM_INNER, SSM_INNER + SSM_GROUPS * SSM_N], axis=-1)
    xs = xs.reshape(bsz, L, SSM_HEADS, SSM_P)
    bm = bm.reshape(bsz, L, SSM_GROUPS, SSM_N)
    cm = cm.reshape(bsz, L, SSM_GROUPS, SSM_N)
    dt = jax.nn.softplus(ssm_dt.astype(f32) + ssm_dt_bias.astype(f32))
    a = -jnp.exp(ssm_a_log.astype(f32))
    y, s_ssm_new = ssd_scan(xs, dt, a, bm, cm, s_ssm.astype(f32))
    y = (y + ssm_d.astype(f32)[:, None] * xs).reshape(bsz, L, SSM_INNER)
    y = y * jax.nn.silu(ssm_z.astype(f32))
    y = rmsnorm(y.reshape(bsz, L, SSM_GROUPS, SSM_INNER // SSM_GROUPS),
                ssm_norm_w.reshape(SSM_GROUPS, SSM_INNER // SSM_GROUPS)).reshape(bsz, L, SSM_INNER)

    mixed = jnp.concatenate([o_dn.reshape(bsz, L, DN_V), y], axis=-1).astype(h.dtype)
    out = mixed @ w_out
    return (out, new_conv_dn.astype(conv_dn.dtype), s_dn_new.astype(s_dn.dtype),
            new_conv_ssm.astype(conv_ssm.dtype), s_ssm_new.astype(s_ssm.dtype))


def odd_mixer(h, s_gla, w_in, w_gate2, b_gate, norm_w, w_out):
    f32 = jnp.float32
    bsz, L, _ = h.shape
    proj = h @ w_in
    q, k, v, glr, og = jnp.split(
        proj, [GLA_K, 2 * GLA_K, 2 * GLA_K + GLA_V, 2 * GLA_K + GLA_V + GLA_LR], axis=-1)
    q = q.astype(f32).reshape(bsz, L, GLA_HEADS, GLA_DK) * (GLA_DK ** -0.5)
    k = k.astype(f32).reshape(bsz, L, GLA_HEADS, GLA_DK)
    v = v.astype(f32).reshape(bsz, L, GLA_HEADS, GLA_DV)
    gk = jax.nn.log_sigmoid((glr @ w_gate2).astype(f32) + b_gate.astype(f32)) / GLA_NORMALIZER
    gk = gk.reshape(bsz, L, GLA_HEADS, GLA_DK)
    o, s_new = gla_scan(q, k, v, gk, s_gla.astype(f32))
    o = rmsnorm(o, norm_w) * jax.nn.silu(og.astype(f32).reshape(bsz, L, GLA_HEADS, GLA_DV))
    out = o.reshape(bsz, L, GLA_V).astype(h.dtype) @ w_out
    return out, s_new.astype(s_gla.dtype)


def peer_ffn(h, w_q, keys1, keys2, u_tab, v_tab):
    f32 = jnp.float32
    bsz, L, d = h.shape
    t = h.reshape(bsz * L, d)
    n_tok = bsz * L
    blk = min(PEER_BLOCK, n_tok)
    nb = -(-n_tok // blk)
    t = jnp.pad(t, ((0, nb * blk - n_tok), (0, 0))).reshape(nb, blk, d)
    half = PEER_DQ // 2

    def block(tb):
        q = (tb @ w_q).astype(f32).reshape(blk, PEER_HEADS, 2, half)
        s1 = jnp.einsum('thd,hnd->thn', q[:, :, 0], keys1.astype(f32))
        s2 = jnp.einsum('thd,hnd->thn', q[:, :, 1], keys2.astype(f32))
        v1, i1 = lax.top_k(s1, PEER_TOPK)
        v2, i2 = lax.top_k(s2, PEER_TOPK)
        cand = (v1[..., :, None] + v2[..., None, :]).reshape(blk, PEER_HEADS, PEER_TOPK * PEER_TOPK)
        cand_idx = (i1[..., :, None] * PEER_NKEYS + i2[..., None, :]).reshape(
            blk, PEER_HEADS, PEER_TOPK * PEER_TOPK)
        top_s, top_j = lax.top_k(cand, PEER_TOPK)
        expert = jnp.take_along_axis(cand_idx, top_j, axis=-1)
        gate = jax.nn.softmax(top_s, axis=-1)
        pre = jnp.einsum('thkd,td->thk', u_tab[expert], tb).astype(f32)
        act = (jax.nn.gelu(pre) * gate).astype(tb.dtype)
        return jnp.einsum('thk,thkd->td', act, v_tab[expert])

    out = lax.map(block, t).reshape(nb * blk, d)[:n_tok]
    return out.reshape(bsz, L, d)


def setup_inputs(seed: int = 0) -> dict:
    key = jax.random.key(seed)
    ks = iter(jax.random.split(key, 64))

    def nrm(shape, scale):
        return jax.random.normal(next(ks), shape, jnp.float32) * scale

    def a_log(shape):
        return jnp.log(jax.random.uniform(next(ks), shape, jnp.float32, 1.0, 16.0))

    def dt_bias(shape):
        dt = jnp.exp(jax.random.uniform(next(ks), shape, jnp.float32,
                                        math.log(1e-3), math.log(1e-1)))
        return dt + jnp.log(-jnp.expm1(-dt))

    d = D_MODEL
    return {
        "x_prompt": nrm((BATCH, SEQ, d), 1.0),
        "x_sample": nrm((DEC_BATCH, DEC_SEQ, d), 1.0),
        "state_dn_conv": nrm((N_EVEN, DEC_BATCH, CONV_W - 1, DN_CONV_DIM), 1.0),
        "state_dn": nrm((N_EVEN, DEC_BATCH, DN_HEADS, DN_DK, DN_DV), 0.5),
        "state_ssm_conv": nrm((N_EVEN, DEC_BATCH, CONV_W - 1, SSM_CONV_DIM), 1.0),
        "state_ssm": nrm((N_EVEN, DEC_BATCH, SSM_HEADS, SSM_P, SSM_N), 0.5),
        "state_gla": nrm((N_ODD, DEC_BATCH, GLA_HEADS, GLA_DK, GLA_DV), 0.5),
        "ln_mix": 1.0 + nrm((DEPTH, d), 0.01),
        "ln_ffn": 1.0 + nrm((DEPTH, d), 0.01),
        "ln_final": 1.0 + nrm((d,), 0.01),
        "w_in_even": nrm((N_EVEN, d, IN_EVEN), d ** -0.5),
        "w_out_even": nrm((N_EVEN, MIX_EVEN, d), MIX_EVEN ** -0.5),
        "dn_conv_w": nrm((N_EVEN, CONV_W, DN_CONV_DIM), CONV_W ** -0.5),
        "dn_a_log": a_log((N_EVEN, DN_HEADS)),
        "dn_dt_bias": dt_bias((N_EVEN, DN_HEADS)),
        "dn_norm_w": 1.0 + nrm((N_EVEN, DN_DV), 0.01),
        "ssm_conv_w": nrm((N_EVEN, CONV_W, SSM_CONV_DIM), CONV_W ** -0.5),
        "ssm_conv_b": nrm((N_EVEN, SSM_CONV_DIM), 0.02),
        "ssm_a_log": a_log((N_EVEN, SSM_HEADS)),
        "ssm_dt_bias": dt_bias((N_EVEN, SSM_HEADS)),
        "ssm_d": 1.0 + nrm((N_EVEN, SSM_HEADS), 0.1),
        "ssm_norm_w": 1.0 + nrm((N_EVEN, SSM_INNER), 0.01),
        "w_in_odd": nrm((N_ODD, d, IN_ODD), d ** -0.5),
        "gla_w_gate2": nrm((N_ODD, GLA_LR, GLA_K), GLA_LR ** -0.5),
        "gla_b_gate": nrm((N_ODD, GLA_K), 0.1),
        "gla_norm_w": 1.0 + nrm((N_ODD, GLA_DV), 0.01),
        "w_out_odd": nrm((N_ODD, GLA_V, d), GLA_V ** -0.5),
        "peer_w_q": nrm((DEPTH, d, PEER_HEADS * PEER_DQ), d ** -0.5),
        "peer_keys1": nrm((DEPTH, PEER_HEADS, PEER_NKEYS, PEER_DQ // 2), (PEER_DQ // 2) ** -0.5),
        "peer_keys2": nrm((DEPTH, PEER_HEADS, PEER_NKEYS, PEER_DQ // 2), (PEER_DQ // 2) ** -0.5),
        "peer_u": nrm((DEPTH, PEER_N, d), d ** -0.5),
        "peer_v": nrm((DEPTH, PEER_N, d), PEER_HEADS ** -0.5),
    }


def reference(x_prompt, x_sample, state_dn_conv, state_dn, state_ssm_conv, state_ssm, state_gla,
              ln_mix, ln_ffn, ln_final, w_in_even, w_out_even, dn_conv_w, dn_a_log, dn_dt_bias,
              dn_norm_w, ssm_conv_w, ssm_conv_b, ssm_a_log, ssm_dt_bias, ssm_d, ssm_norm_w,
              w_in_odd, gla_w_gate2, gla_b_gate, gla_norm_w, w_out_odd, peer_w_q, peer_keys1,
              peer_keys2, peer_u, peer_v):

    def trunk(x, conv_dn0, s_dn0, conv_ssm0, s_ssm0, s_gla0):
        conv_dn_l, s_dn_l, conv_ssm_l, s_ssm_l, s_gla_l = [], [], [], [], []
        for i in range(DEPTH):
            j = i // 2
            h = rmsnorm(x, ln_mix[i])
            if i % 2 == 0:
                o, cd, sd, cs, ss = even_mixer(
                    h, conv_dn0[j], s_dn0[j], conv_ssm0[j], s_ssm0[j], w_in_even[j],
                    w_out_even[j], dn_conv_w[j], dn_a_log[j], dn_dt_bias[j], dn_norm_w[j],
                    ssm_conv_w[j], ssm_conv_b[j], ssm_a_log[j], ssm_dt_bias[j], ssm_d[j],
                    ssm_norm_w[j])
                conv_dn_l.append(cd)
                s_dn_l.append(sd)
                conv_ssm_l.append(cs)
                s_ssm_l.append(ss)
            else:
                o, sg = odd_mixer(h, s_gla0[j], w_in_odd[j], gla_w_gate2[j], gla_b_gate[j],
                                  gla_norm_w[j], w_out_odd[j])
                s_gla_l.append(sg)
            x = x + o
            x = x + peer_ffn(rmsnorm(x, ln_ffn[i]), peer_w_q[i], peer_keys1[i], peer_keys2[i],
                             peer_u[i], peer_v[i])
        return (rmsnorm(x, ln_final), jnp.stack(conv_dn_l), jnp.stack(s_dn_l),
                jnp.stack(conv_ssm_l), jnp.stack(s_ssm_l), jnp.stack(s_gla_l))

    zc_dn = jnp.zeros((N_EVEN, BATCH) + state_dn_conv.shape[2:], state_dn_conv.dtype)
    zs_dn = jnp.zeros((N_EVEN, BATCH) + state_dn.shape[2:], state_dn.dtype)
    zc_ssm = jnp.zeros((N_EVEN, BATCH) + state_ssm_conv.shape[2:], state_ssm_conv.dtype)
    zs_ssm = jnp.zeros((N_EVEN, BATCH) + state_ssm.shape[2:], state_ssm.dtype)
    zs_gla = jnp.zeros((N_ODD, BATCH) + state_gla.shape[2:], state_gla.dtype)
    y_prompt, p_dn_conv, p_dn, p_ssm_conv, p_ssm, p_gla = trunk(
        x_prompt, zc_dn, zs_dn, zc_ssm, zs_ssm, zs_gla)

    y_sample, s_dn_conv, s_dn, s_ssm_conv, s_ssm, s_gla = trunk(
        x_sample, state_dn_conv, state_dn, state_ssm_conv, state_ssm, state_gla)

    return (y_prompt, y_sample, p_dn_conv, p_dn, p_ssm_conv, p_ssm, p_gla,
            s_dn_conv, s_dn, s_ssm_conv, s_ssm, s_gla)
```

```python
import functools

import jax
import jax.numpy as jnp
from jax import lax
from jax.experimental import pallas as pl
from jax.experimental.pallas import tpu as pltpu

F32 = jnp.float32
BF16 = jnp.bfloat16
EPS = 1e-6

D_MODEL = 1024
CONV_W = 4
CONV_PAD = 8
PROMPT_CHUNK = 64
STEP_CHUNK = 8

DN_HEADS, DN_DK, DN_DV = 4, 128, 128
DN_QK = DN_HEADS * DN_DK
DN_V = DN_HEADS * DN_DV
DN_CONV_DIM = 2 * DN_QK + DN_V
SSM_HEADS, SSM_P, SSM_GROUPS, SSM_N = 8, 64, 2, 128
SSM_INNER = SSM_HEADS * SSM_P
SSM_CONV_DIM = SSM_INNER + 2 * SSM_GROUPS * SSM_N
GLA_HEADS, GLA_DK, GLA_DV, GLA_LR = 4, 128, 256, 16
GLA_K = GLA_HEADS * GLA_DK
GLA_V = GLA_HEADS * GLA_DV
GLA_NORMALIZER = 16.0
PEER_HEADS, PEER_DQ, PEER_NKEYS, PEER_TOPK = 8, 256, 128, 16
PEER_HALF = PEER_DQ // 2
PEER_N = PEER_NKEYS * PEER_NKEYS
LANES = 128
SMALL_W = LANES

EV_QKV, EV_GATE, EV_XBC, EV_Z, EV_SMALL = 0, 1536, 2048, 3072, 3584
EV_WIDTH = EV_SMALL + SMALL_W
OD_QKV, OD_OG, OD_GLR = 0, 2048, 3072
OD_WIDTH = OD_GLR + SMALL_W

VMEM_LIMIT = 48 * 1024 * 1024

_NT = (((1,), (1,)), ((), ()))
_TN = (((0,), (0,)), ((), ()))


def _dot(a, b):
    return jnp.dot(a.astype(BF16), b.astype(BF16), preferred_element_type=F32)


def _dot_nt(a, b):
    return lax.dot_general(a.astype(BF16), b.astype(BF16), _NT, preferred_element_type=F32)


def _dot_tn(a, b):
    return lax.dot_general(a.astype(BF16), b.astype(BF16), _TN, preferred_element_type=F32)


def _params(sem):
    return pltpu.CompilerParams(dimension_semantics=sem, vmem_limit_bytes=VMEM_LIMIT)


def _silu(x):
    return x * jax.nn.sigmoid(x)


def _softplus(x):
    return jnp.maximum(x, 0.0) + jnp.log(1.0 + jnp.exp(-jnp.abs(x)))


def _cumsum_rows(x):
    n = x.shape[0]
    row = lax.broadcasted_iota(jnp.int32, x.shape, 0)
    s = 1
    while s < n:
        x = x + jnp.where(row >= s, pltpu.roll(x, s, axis=0), 0.0)
        s *= 2
    return x


def _col_to_row(col, eye):
    return jnp.sum(jnp.where(eye, col, 0.0), axis=0, keepdims=True)


def _row_to_col(row, eye):
    return jnp.sum(jnp.where(eye, row, 0.0), axis=1, keepdims=True)


def _tri_masks(n):
    ri = lax.broadcasted_iota(jnp.int32, (n, n), 0)
    ci = lax.broadcasted_iota(jnp.int32, (n, n), 1)
    return ri, ci, ri == ci, ri >= ci, ri > ci


def _valid_rows(chunk, seq_len, shape):
    if seq_len % chunk == 0:
        return None
    pos = pl.program_id(1) * chunk + lax.broadcasted_iota(jnp.int32, shape, 0)
    return pos < seq_len


def _causal_conv(x_ref, xext, cw_ref, chunk):
    x = x_ref[...]
    xext[CONV_PAD:CONV_PAD + chunk, :] = x
    cw = cw_ref[...]
    pre = cw[CONV_W - 1:CONV_W, :] * x
    for j in range(1, CONV_W):
        pre = pre + cw[CONV_W - 1 - j:CONV_W - j, :] * xext[CONV_PAD - j:CONV_PAD - j + chunk, :]
    xext[0:CONV_PAD, :] = xext[chunk:chunk + CONV_PAD, :]
    return pre


def _norm_matmul_kernel(x_ref, lnw_ref, w_ref, o_ref):
    x = x_ref[...]
    y = x * lax.rsqrt(jnp.mean(x * x, axis=-1, keepdims=True) + EPS)
    h = (y * lnw_ref[...]).astype(BF16)
    o_ref[...] = jnp.dot(h, w_ref[...], preferred_element_type=F32)


def _norm_matmul(x, lnw, w):
    t, d = x.shape
    n = w.shape[1]
    tm = min(512, t)
    return pl.pallas_call(
        _norm_matmul_kernel,
        out_shape=jax.ShapeDtypeStruct((t, n), F32),
        grid=(t // tm,),
        in_specs=[pl.BlockSpec((tm, d), lambda i: (i, 0)),
                  pl.BlockSpec((1, d), lambda i: (0, 0)),
                  pl.BlockSpec((d, n), lambda i: (0, 0))],
        out_specs=pl.BlockSpec((tm, n), lambda i: (i, 0)),
        compiler_params=_params(("parallel",)),
        name="norm_matmul",
    )(x, lnw.reshape(1, d), w)


def _proj_residual_kernel(*refs, n_in):
    x_ref, o_ref = refs[0], refs[-1]
    acc = x_ref[...]
    for a_ref, w_ref in zip(refs[1:1 + n_in], refs[1 + n_in:1 + 2 * n_in]):
        acc = acc + jnp.dot(a_ref[...].astype(BF16), w_ref[...], preferred_element_type=F32)
    o_ref[...] = acc


def _proj_residual(x, acts, weights):
    t, d = x.shape
    tm = min(512, t)
    n_in = len(acts)
    in_specs = [pl.BlockSpec((tm, d), lambda i: (i, 0))]
    in_specs += [pl.BlockSpec((tm, a.shape[1]), lambda i: (i, 0)) for a in acts]
    in_specs += [pl.BlockSpec(w.shape, lambda i: (0, 0)) for w in weights]
    return pl.pallas_call(
        functools.partial(_proj_residual_kernel, n_in=n_in),
        out_shape=jax.ShapeDtypeStruct((t, d), F32),
        grid=(t // tm,),
        in_specs=in_specs,
        out_specs=pl.BlockSpec((tm, d), lambda i: (i, 0)),
        compiler_params=_params(("parallel",)),
        name="proj_residual",
    )(x, *acts, *weights)


def _final_norm_kernel(x_ref, w_ref, o_ref):
    x = x_ref[...]
    o_ref[...] = x * lax.rsqrt(jnp.mean(x * x, axis=-1, keepdims=True) + EPS) * w_ref[...]


def _final_norm(x, w):
    t, d = x.shape
    tm = min(512, t)
    return pl.pallas_call(
        _final_norm_kernel,
        out_shape=jax.ShapeDtypeStruct((t, d), F32),
        grid=(t // tm,),
        in_specs=[pl.BlockSpec((tm, d), lambda i: (i, 0)), pl.BlockSpec((1, d), lambda i: (0, 0))],
        out_specs=pl.BlockSpec((tm, d), lambda i: (i, 0)),
        compiler_params=_params(("parallel",)),
        name="final_norm",
    )(x, w.reshape(1, d))


def _unit_lower_inverse(m, ri, ci, eye):
    n = m.shape[0]
    t = jnp.where(eye, 1.0, 0.0)
    level = 0
    while (1 << level) < n:
        pair = ((ri >> (level + 1)) == (ci >> (level + 1))) & ((ri >> level) != (ci >> level))
        c = jnp.where(pair, m, 0.0)
        t = t - c if level == 0 else t - _dot(_dot(t, c), t)
        level += 1
    return t


def _dn_kernel(qkv_ref, gate_ref, small_ref, conv0_ref, s0_ref, cw_ref, par_ref, nw_ref,
               o_ref, s_out_ref, xext, state, *, chunk, seq_len):
    c_idx = pl.program_id(1)

    @pl.when(c_idx == 0)
    def _():
        xext[0:CONV_PAD, :] = conv0_ref[0]
        state[...] = s0_ref[0]

    act = _silu(_causal_conv(qkv_ref, xext, cw_ref, chunk))
    small = small_ref[...]
    par = par_ref[...]
    g_all = -jnp.exp(par[0:1, :]) * _softplus(small + par[1:2, :])
    beta_all = jax.nn.sigmoid(small)
    valid = _valid_rows(chunk, seq_len, small.shape)
    if valid is not None:
        g_all = jnp.where(valid, g_all, 0.0)
        beta_all = jnp.where(valid, beta_all, 0.0)
    gc_all = _cumsum_rows(g_all)

    ri, ci, eye, causal, strict = _tri_masks(chunk)
    nw = nw_ref[...]
    for h in range(DN_HEADS):
        gc = gc_all[:, h:h + 1]
        beta = beta_all[:, DN_HEADS + h:DN_HEADS + h + 1]
        decay = jnp.where(causal, jnp.exp(gc - _col_to_row(gc, eye)), 0.0)
        xq = act[:, h * DN_DK:(h + 1) * DN_DK]
        xk = act[:, DN_QK + h * DN_DK:DN_QK + (h + 1) * DN_DK]
        v = act[:, 2 * DN_QK + h * DN_DV:2 * DN_QK + (h + 1) * DN_DV]
        q = xq * lax.rsqrt(jnp.sum(xq * xq, axis=-1, keepdims=True) + EPS) * (DN_DK ** -0.5)
        k = xk * lax.rsqrt(jnp.sum(xk * xk, axis=-1, keepdims=True) + EPS)
        kb = k * beta
        eg = jnp.exp(gc)
        m = jnp.where(strict, _dot_nt(kb, k) * decay, 0.0)
        t_inv = _unit_lower_inverse(m, ri, ci, eye)
        u = _dot(t_inv, v * beta)
        w = _dot(t_inv, kb * eg)
        qk = jnp.where(causal, _dot_nt(q, k) * decay, 0.0)
        gc_last = gc[chunk - 1:chunk, :]
        s = state[h]
        v_new = u - _dot(w, s)
        o = _dot(q * eg, s) + _dot(qk, v_new)
        state[h] = s * jnp.exp(gc_last) + _dot_tn(k * jnp.exp(gc_last - gc), v_new)
        gate = gate_ref[:, h * DN_DV:(h + 1) * DN_DV]
        o = o * lax.rsqrt(jnp.mean(o * o, axis=-1, keepdims=True) + EPS) * nw
        o_ref[:, h * DN_DV:(h + 1) * DN_DV] = o * _silu(gate)

    @pl.when(c_idx == pl.num_programs(1) - 1)
    def _():
        s_out_ref[0] = state[...]


def _dn_mixer(proj, conv0, s0, cw, par, nw, *, batch, seq_len, chunk):
    nc = -(-seq_len // chunk)
    rows = batch * nc * chunk
    row = lambda b, c: b * nc + c
    return pl.pallas_call(
        functools.partial(_dn_kernel, chunk=chunk, seq_len=seq_len),
        out_shape=(jax.ShapeDtypeStruct((rows, DN_V), F32),
                   jax.ShapeDtypeStruct((batch, DN_HEADS, DN_DK, DN_DV), F32)),
        grid=(batch, nc),
        in_specs=[pl.BlockSpec((chunk, DN_CONV_DIM), lambda b, c: (row(b, c), EV_QKV // DN_CONV_DIM)),
                  pl.BlockSpec((chunk, DN_V), lambda b, c: (row(b, c), EV_GATE // DN_V)),
                  pl.BlockSpec((chunk, SMALL_W), lambda b, c: (row(b, c), EV_SMALL // SMALL_W)),
                  pl.BlockSpec((1, CONV_PAD, DN_CONV_DIM), lambda b, c: (b, 0, 0)),
                  pl.BlockSpec((1, DN_HEADS, DN_DK, DN_DV), lambda b, c: (b, 0, 0, 0)),
                  pl.BlockSpec((CONV_W, DN_CONV_DIM), lambda b, c: (0, 0)),
                  pl.BlockSpec((8, SMALL_W), lambda b, c: (0, 0)),
                  pl.BlockSpec((1, DN_DV), lambda b, c: (0, 0))],
        out_specs=(pl.BlockSpec((chunk, DN_V), lambda b, c: (row(b, c), 0)),
                   pl.BlockSpec((1, DN_HEADS, DN_DK, DN_DV), lambda b, c: (b, 0, 0, 0))),
        scratch_shapes=[pltpu.VMEM((chunk + CONV_PAD, DN_CONV_DIM), F32),
                        pltpu.VMEM((DN_HEADS, DN_DK, DN_DV), F32)],
        compiler_params=_params(("parallel", "arbitrary")),
        name="dn_mixer",
    )(proj, proj, proj, conv0, s0, cw, par, nw)


def _ssd_kernel(xbc_ref, z_ref, small_ref, conv0_ref, s0_ref, cw_ref, cb_ref, par_ref, nw_ref,
                y_ref, s_out_ref, xext, state, ybuf, *, chunk, seq_len):
    c_idx = pl.program_id(1)

    @pl.when(c_idx == 0)
    def _():
        xext[0:CONV_PAD, :] = conv0_ref[0]
        state[...] = s0_ref[0]

    xbc = _silu(_causal_conv(xbc_ref, xext, cw_ref, chunk) + cb_ref[...])
    small = small_ref[...]
    par = par_ref[...]
    dt_all = _softplus(small + par[1:2, :])
    valid = _valid_rows(chunk, seq_len, small.shape)
    if valid is not None:
        dt_all = jnp.where(valid, dt_all, 0.0)
    ac_all = _cumsum_rows(dt_all * -jnp.exp(par[0:1, :]))

    ri, ci, eye, causal, strict = _tri_masks(chunk)
    gw = SSM_GROUPS * SSM_N
    heads_per_group = SSM_HEADS // SSM_GROUPS
    for g in range(SSM_GROUPS):
        bm = xbc[:, SSM_INNER + g * SSM_N:SSM_INNER + (g + 1) * SSM_N]
        cm = xbc[:, SSM_INNER + gw + g * SSM_N:SSM_INNER + gw + (g + 1) * SSM_N]
        cb = _dot_nt(cm, bm)
        for hh in range(heads_per_group):
            h = g * heads_per_group + hh
            lane = 2 * DN_HEADS + h
            ac = ac_all[:, lane:lane + 1]
            dt = dt_all[:, lane:lane + 1]
            decay = jnp.where(causal, jnp.exp(ac - _col_to_row(ac, eye)), 0.0)
            xh = xbc[:, h * SSM_P:(h + 1) * SSM_P]
            xdt = xh * dt
            s = state[h]
            y = _dot(cb * decay, xdt) + _dot_nt(cm * jnp.exp(ac), s)
            ac_last = ac[chunk - 1:chunk, :]
            state[h] = s * jnp.exp(ac_last) + _dot_tn(xdt, bm * jnp.exp(ac_last - ac))
            ybuf[:, h * SSM_P:(h + 1) * SSM_P] = y + par[2:3, lane:lane + 1] * xh

    y = ybuf[...] * _silu(z_ref[...])
    nw = nw_ref[...]
    gn = SSM_INNER // SSM_GROUPS
    for g in range(SSM_GROUPS):
        yg = y[:, g * gn:(g + 1) * gn]
        yg = yg * lax.rsqrt(jnp.mean(yg * yg, axis=-1, keepdims=True) + EPS)
        y_ref[:, g * gn:(g + 1) * gn] = yg * nw[:, g * gn:(g + 1) * gn]

    @pl.when(c_idx == pl.num_programs(1) - 1)
    def _():
        s_out_ref[0] = state[...]


def _ssd_mixer(proj, conv0, s0, cw, cb, par, nw, *, batch, seq_len, chunk):
    nc = -(-seq_len // chunk)
    rows = batch * nc * chunk
    row = lambda b, c: b * nc + c
    return pl.pallas_call(
        functools.partial(_ssd_kernel, chunk=chunk, seq_len=seq_len),
        out_shape=(jax.ShapeDtypeStruct((rows, SSM_INNER), F32),
                   jax.ShapeDtypeStruct((batch, SSM_HEADS, SSM_P, SSM_N), F32)),
        grid=(batch, nc),
        in_specs=[pl.BlockSpec((chunk, SSM_CONV_DIM), lambda b, c: (row(b, c), EV_XBC // SSM_CONV_DIM)),
                  pl.BlockSpec((chunk, SSM_INNER), lambda b, c: (row(b, c), EV_Z // SSM_INNER)),
                  pl.BlockSpec((chunk, SMALL_W), lambda b, c: (row(b, c), EV_SMALL // SMALL_W)),
                  pl.BlockSpec((1, CONV_PAD, SSM_CONV_DIM), lambda b, c: (b, 0, 0)),
                  pl.BlockSpec((1, SSM_HEADS, SSM_P, SSM_N), lambda b, c: (b, 0, 0, 0)),
                  pl.BlockSpec((CONV_W, SSM_CONV_DIM), lambda b, c: (0, 0)),
                  pl.BlockSpec((1, SSM_CONV_DIM), lambda b, c: (0, 0)),
                  pl.BlockSpec((8, SMALL_W), lambda b, c: (0, 0)),
                  pl.BlockSpec((1, SSM_INNER), lambda b, c: (0, 0))],
        out_specs=(pl.BlockSpec((chunk, SSM_INNER), lambda b, c: (row(b, c), 0)),
                   pl.BlockSpec((1, SSM_HEADS, SSM_P, SSM_N), lambda b, c: (b, 0, 0, 0))),
        scratch_shapes=[pltpu.VMEM((chunk + CONV_PAD, SSM_CONV_DIM), F32),
                        pltpu.VMEM((SSM_HEADS, SSM_P, SSM_N), F32),
                        pltpu.VMEM((chunk, SSM_INNER), F32)],
        compiler_params=_params(("parallel", "arbitrary")),
        name="ssd_mixer",
    )(proj, proj, proj, conv0, s0, cw, cb, par, nw)


def _gla_kernel(qkv_ref, og_ref, glr_ref, s0_ref, wg_ref, bg_ref, nw_ref,
                o_ref, s_out_ref, state, *, chunk, seq_len):
    c_idx = pl.program_id(1)

    @pl.when(c_idx == 0)
    def _():
        state[...] = s0_ref[0]

    x = bg_ref[...] + jnp.dot(glr_ref[...].astype(BF16), wg_ref[...], preferred_element_type=F32)
    gk = (jnp.minimum(x, 0.0) - jnp.log(1.0 + jnp.exp(-jnp.abs(x)))) / GLA_NORMALIZER
    valid = _valid_rows(chunk, seq_len, gk.shape)
    if valid is not None:
        gk = jnp.where(valid, gk, 0.0)
    b_all = _cumsum_rows(gk)

    ri, ci, eye_c, causal, strict = _tri_masks(chunk)
    eye_k = _tri_masks(GLA_DK)[2]
    nw = nw_ref[...]
    for h in range(GLA_HEADS):
        b = b_all[:, h * GLA_DK:(h + 1) * GLA_DK]
        q = qkv_ref[:, h * GLA_DK:(h + 1) * GLA_DK] * (GLA_DK ** -0.5)
        k = qkv_ref[:, GLA_K + h * GLA_DK:GLA_K + (h + 1) * GLA_DK]
        v = qkv_ref[:, 2 * GLA_K + h * GLA_DV:2 * GLA_K + (h + 1) * GLA_DV]
        qe = q * jnp.exp(b)
        att = jnp.where(causal, _dot_nt(qe, k * jnp.exp(-b)), 0.0)
        s = state[h]
        o = _dot(qe, s) + _dot(att, v)
        b_last = b[chunk - 1:chunk, :]
        state[h] = s * _row_to_col(jnp.exp(b_last), eye_k) + _dot_tn(k * jnp.exp(b_last - b), v)
        o = o * lax.rsqrt(jnp.mean(o * o, axis=-1, keepdims=True) + EPS) * nw
        o_ref[:, h * GLA_DV:(h + 1) * GLA_DV] = o * _silu(og_ref[:, h * GLA_DV:(h + 1) * GLA_DV])

    @pl.when(c_idx == pl.num_programs(1) - 1)
    def _():
        s_out_ref[0] = state[...]


def _gla_mixer(proj, s0, wg, bg, nw, *, batch, seq_len, chunk):
    nc = -(-seq_len // chunk)
    rows = batch * nc * chunk
    row = lambda b, c: b * nc + c
    qkv_w = 2 * GLA_K + GLA_V
    return pl.pallas_call(
        functools.partial(_gla_kernel, chunk=chunk, seq_len=seq_len),
        out_shape=(jax.ShapeDtypeStruct((rows, GLA_V), F32),
                   jax.ShapeDtypeStruct((batch, GLA_HEADS, GLA_DK, GLA_DV), F32)),
        grid=(batch, nc),
        in_specs=[pl.BlockSpec((chunk, qkv_w), lambda b, c: (row(b, c), OD_QKV // qkv_w)),
                  pl.BlockSpec((chunk, GLA_V), lambda b, c: (row(b, c), OD_OG // GLA_V)),
                  pl.BlockSpec((chunk, SMALL_W), lambda b, c: (row(b, c), OD_GLR // SMALL_W)),
                  pl.BlockSpec((1, GLA_HEADS, GLA_DK, GLA_DV), lambda b, c: (b, 0, 0, 0)),
                  pl.BlockSpec((SMALL_W, GLA_K), lambda b, c: (0, 0)),
                  pl.BlockSpec((1, GLA_K), lambda b, c: (0, 0)),
                  pl.BlockSpec((1, GLA_DV), lambda b, c: (0, 0))],
        out_specs=(pl.BlockSpec((chunk, GLA_V), lambda b, c: (row(b, c), 0)),
                   pl.BlockSpec((1, GLA_HEADS, GLA_DK, GLA_DV), lambda b, c: (b, 0, 0, 0))),
        scratch_shapes=[pltpu.VMEM((GLA_HEADS, GLA_DK, GLA_DV), F32)],
        compiler_params=_params(("parallel", "arbitrary")),
        name="gla_mixer",
    )(proj, proj, proj, s0, wg, bg, nw)


def _take_top(x, count):
    nk, nt = x.shape
    key = lax.broadcasted_iota(jnp.int32, (nk, nt), 0)
    slot = lax.broadcasted_iota(jnp.int32, (count, nt), 0)

    def body(r, carry):
        x, rank, vals = carry
        m = jnp.max(x, axis=0, keepdims=True)
        first = jnp.min(jnp.where(x == m, key, nk), axis=0, keepdims=True)
        hit = key == first
        return (jnp.where(hit, -jnp.inf, x), jnp.where(hit, r, rank), jnp.where(slot == r, m, vals))

    init = (x, jnp.full((nk, nt), count, jnp.int32), jnp.zeros((count, nt), F32))
    _, rank, vals = lax.fori_loop(0, count, body, init)
    return vals, rank


def _peer_route_kernel(q_ref, k1_ref, k2_ref, n_ref, e1_ref, r2_ref, e2_ref):
    k = PEER_TOPK
    q = q_ref[...]
    s1 = _dot_nt(k1_ref[0], q[:, :PEER_HALF])
    s2 = _dot_nt(k2_ref[0], q[:, PEER_HALF:])
    v1, rank1 = _take_top(s1, k)
    v2, rank2 = _take_top(s2, k)
    cand = jnp.concatenate([v1[i:i + 1, :] + v2 for i in range(k)], axis=0)
    _, crank = _take_top(cand, k)
    sel = crank < k
    top = v1[0:1, :] + v2[0:1, :]
    z = jnp.sum(jnp.where(sel, jnp.exp(cand - top), 0.0), axis=0, keepdims=True)
    n = jnp.zeros(s1.shape, F32)
    for i in range(k):
        n_i = jnp.sum(jnp.where(sel[i * k:(i + 1) * k, :], 1.0, 0.0), axis=0, keepdims=True)
        n = jnp.where(rank1 == i, n_i, n)
    n_ref[0] = n
    e1_ref[0] = jnp.where(rank1 < k, jnp.exp(s1 - v1[0:1, :]) / z, 0.0)
    r2_ref[0] = rank2.astype(F32)
    e2_ref[0] = jnp.where(rank2 < k, jnp.exp(s2 - v2[0:1, :]), 0.0)


def _peer_route(q, keys1, keys2, tt):
    t = q.shape[0]
    out = jax.ShapeDtypeStruct((PEER_HEADS, PEER_NKEYS, t), F32)
    out_spec = pl.BlockSpec((1, PEER_NKEYS, tt), lambda i, h: (h, 0, i))
    key_spec = pl.BlockSpec((1, PEER_NKEYS, PEER_HALF), lambda i, h: (h, 0, 0))
    return pl.pallas_call(
        _peer_route_kernel,
        out_shape=(out, out, out, out),
        grid=(t // tt, PEER_HEADS),
        in_specs=[pl.BlockSpec((tt, PEER_DQ), lambda i, h: (i, h)), key_spec, key_spec],
        out_specs=(out_spec, out_spec, out_spec, out_spec),
        compiler_params=_params(("parallel", "arbitrary")),
        name="peer_route",
    )(q, keys1, keys2)


def _peer_dense_kernel(x_ref, lnw_ref, n_ref, e1_ref, r2_ref, e2_ref, u_ref, v_ref, o_ref,
                       h_ref, act_ref, acc_ref, *, rows_per_step):
    j = pl.program_id(1)

    @pl.when(j == 0)
    def _():
        x = x_ref[...]
        y = x * lax.rsqrt(jnp.mean(x * x, axis=-1, keepdims=True) + EPS)
        h_ref[...] = (y * lnw_ref[...]).astype(BF16)
        acc_ref[...] = jnp.zeros_like(acc_ref)

    hb = h_ref[...]
    for aa in range(rows_per_step):
        a = j * rows_per_step + aa
        pre = lax.dot_general(u_ref[aa * PEER_NKEYS:(aa + 1) * PEER_NKEYS, :], hb, _NT,
                              preferred_element_type=F32)
        gate = jnp.zeros(pre.shape, F32)
        for h in range(PEER_HEADS):
            n_row = n_ref[h, pl.ds(a, 1), :]
            e1_row = e1_ref[h, pl.ds(a, 1), :]
            gate = gate + jnp.where(r2_ref[h] < n_row, e2_ref[h], 0.0) * e1_row
        act_ref[aa * PEER_NKEYS:(aa + 1) * PEER_NKEYS, :] = (jax.nn.gelu(pre) * gate).astype(BF16)
    acc_ref[...] += lax.dot_general(act_ref[...], v_ref[...], _TN, preferred_element_type=F32)

    @pl.when(j == pl.num_programs(1) - 1)
    def _():
        o_ref[...] = x_ref[...] + acc_ref[...]


def _peer_dense(x, lnw, route, u, v, tt, rows_per_step):
    t, d = x.shape
    eb = rows_per_step * PEER_NKEYS
    route_spec = pl.BlockSpec((PEER_HEADS, PEER_NKEYS, tt), lambda i, j: (0, 0, i))
    return pl.pallas_call(
        functools.partial(_peer_dense_kernel, rows_per_step=rows_per_step),
        out_shape=jax.ShapeDtypeStruct((t, d), F32),
        grid=(t // tt, PEER_N // eb),
        in_specs=[pl.BlockSpec((tt, d), lambda i, j: (i, 0)),
                  pl.BlockSpec((1, d), lambda i, j: (0, 0)),
                  route_spec, route_spec, route_spec, route_spec,
                  pl.BlockSpec((eb, d), lambda i, j: (j, 0)),
                  pl.BlockSpec((eb, d), lambda i, j: (j, 0))],
        out_specs=pl.BlockSpec((tt, d), lambda i, j: (i, 0)),
        scratch_shapes=[pltpu.VMEM((tt, d), BF16), pltpu.VMEM((eb, tt), BF16), pltpu.VMEM((tt, d), F32)],
        compiler_params=_params(("parallel", "arbitrary")),
        name="peer_dense",
    )(x, lnw.reshape(1, d), *route, u, v)


def _peer_ffn(x, lnw, wq, keys1, keys2, u, v):
    tt = min(512, x.shape[0])
    q = _norm_matmul(x, lnw, wq)
    route = _peer_route(q, keys1, keys2, tt)
    return _peer_dense(x, lnw, route, u, v, tt, rows_per_step=8)


def _pad_lanes(vec, offset):
    return jnp.zeros((SMALL_W,), F32).at[offset:offset + vec.shape[0]].set(vec.astype(F32))


def _conv_history(buf):
    return jnp.pad(buf, ((0, 0), (CONV_PAD - (CONV_W - 1), 0), (0, 0)))


def kernel(x_prompt, x_sample, state_dn_conv, state_dn, state_ssm_conv, state_ssm, state_gla, ln_mix, ln_ffn, ln_final, w_in_even, w_out_even, dn_conv_w, dn_a_log, dn_dt_bias, dn_norm_w, ssm_conv_w, ssm_conv_b, ssm_a_log, ssm_dt_bias, ssm_d, ssm_norm_w, w_in_odd, gla_w_gate2, gla_b_gate, gla_norm_w, w_out_odd, peer_w_q, peer_keys1, peer_keys2, peer_u, peer_v):
    we = w_in_even[0]
    o = 0
    cuts = {}
    for name, size in (("qkv", DN_CONV_DIM), ("a", DN_HEADS), ("b", DN_HEADS), ("gate", DN_V),
                       ("z", SSM_INNER), ("xbc", SSM_CONV_DIM), ("dt", SSM_HEADS)):
        cuts[name] = we[:, o:o + size]
        o += size
    small_w = jnp.concatenate([cuts["a"], cuts["b"], cuts["dt"]], axis=1)
    small_w = jnp.pad(small_w, ((0, 0), (0, SMALL_W - small_w.shape[1])))
    w_even = jnp.concatenate([cuts["qkv"], cuts["gate"], cuts["xbc"], cuts["z"], small_w], axis=1).astype(BF16)
    wo = w_in_odd[0]
    qkv_w = 2 * GLA_K + GLA_V
    glr_w = jnp.pad(wo[:, qkv_w:qkv_w + GLA_LR], ((0, 0), (0, SMALL_W - GLA_LR)))
    w_odd = jnp.concatenate([wo[:, :qkv_w], wo[:, qkv_w + GLA_LR:], glr_w], axis=1).astype(BF16)
    w_out_dn = w_out_even[0, :DN_V].astype(BF16)
    w_out_ssm = w_out_even[0, DN_V:].astype(BF16)
    w_out_gla = w_out_odd[0].astype(BF16)
    wg = jnp.pad(gla_w_gate2[0], ((0, SMALL_W - GLA_LR), (0, 0))).astype(BF16)
    wq = peer_w_q.astype(BF16)
    k1 = peer_keys1.astype(BF16)
    k2 = peer_keys2.astype(BF16)
    pu = peer_u.astype(BF16)
    pv = peer_v.astype(BF16)

    zero8 = jnp.zeros((8 - 2, SMALL_W), F32)
    dn_par = jnp.concatenate([_pad_lanes(dn_a_log[0], 0)[None], _pad_lanes(dn_dt_bias[0], 0)[None], zero8])
    lane0 = 2 * DN_HEADS
    ssm_par = jnp.concatenate([_pad_lanes(ssm_a_log[0], lane0)[None], _pad_lanes(ssm_dt_bias[0], lane0)[None],
                               _pad_lanes(ssm_d[0], lane0)[None], zero8[1:]])

    def trunk(x, batch, seq_len, conv_dn0, s_dn0, conv_ssm0, s_ssm0, s_gla0):
        chunk = min(PROMPT_CHUNK, seq_len) if seq_len >= PROMPT_CHUNK else STEP_CHUNK
        nc = -(-seq_len // chunk)
        padded = nc * chunk != seq_len

        def to_seq(a):
            if not padded:
                return a
            a = a.reshape(batch, seq_len, a.shape[-1])
            return jnp.pad(a, ((0, 0), (0, nc * chunk - seq_len), (0, 0))).reshape(batch * nc * chunk, -1)

        def from_seq(a):
            if not padded:
                return a
            return a.reshape(batch, nc * chunk, a.shape[-1])[:, :seq_len].reshape(batch * seq_len, -1)

        def last_rows(buf, new):
            new = new.reshape(batch, seq_len, new.shape[-1])
            return jnp.concatenate([buf, new], axis=1)[:, -(CONV_W - 1):]

        seq = dict(batch=batch, seq_len=seq_len, chunk=chunk)

        proj = _norm_matmul(x, ln_mix[0], w_even)
        proj_seq = to_seq(proj)
        o_dn, s_dn = _dn_mixer(proj_seq, _conv_history(conv_dn0[0]), s_dn0[0], dn_conv_w[0], dn_par,
                               dn_norm_w[0].reshape(1, DN_DV), **seq)
        y_ssm, s_ssm = _ssd_mixer(proj_seq, _conv_history(conv_ssm0[0]), s_ssm0[0], ssm_conv_w[0],
                                  ssm_conv_b[0].reshape(1, SSM_CONV_DIM), ssm_par,
                                  ssm_norm_w[0].reshape(1, SSM_INNER), **seq)
        conv_dn = last_rows(conv_dn0[0], proj[:, EV_QKV:EV_QKV + DN_CONV_DIM])
        conv_ssm = last_rows(conv_ssm0[0], proj[:, EV_XBC:EV_XBC + SSM_CONV_DIM])
        x = _proj_residual(x, [from_seq(o_dn), from_seq(y_ssm)], [w_out_dn, w_out_ssm])
        x = _peer_ffn(x, ln_ffn[0], wq[0], k1[0], k2[0], pu[0], pv[0])

        proj = _norm_matmul(x, ln_mix[1], w_odd)
        o_gla, s_gla = _gla_mixer(to_seq(proj), s_gla0[0], wg, gla_b_gate[0].reshape(1, GLA_K),
                                  gla_norm_w[0].reshape(1, GLA_DV), **seq)
        x = _proj_residual(x, [from_seq(o_gla)], [w_out_gla])
        x = _peer_ffn(x, ln_ffn[1], wq[1], k1[1], k2[1], pu[1], pv[1])

        y = _final_norm(x, ln_final).reshape(batch, seq_len, D_MODEL)
        return y, conv_dn[None], s_dn[None], conv_ssm[None], s_ssm[None], s_gla[None]

    bp, lp = x_prompt.shape[:2]
    bs, ls = x_sample.shape[:2]

    def zeros_like_state(s):
        return jnp.zeros((s.shape[0], bp) + s.shape[2:], s.dtype)

    prompt = trunk(x_prompt.reshape(bp * lp, D_MODEL), bp, lp, zeros_like_state(state_dn_conv),
                   zeros_like_state(state_dn), zeros_like_state(state_ssm_conv),
                   zeros_like_state(state_ssm), zeros_like_state(state_gla))
    sample = trunk(x_sample.reshape(bs * ls, D_MODEL), bs, ls, state_dn_conv, state_dn,
                   state_ssm_conv, state_ssm, state_gla)
    return (prompt[0], sample[0]) + prompt[1:] + sample[1:]
```

```python
import functools

import jax
import jax.numpy as jnp
from jax import lax
from jax.experimental import pallas as pl
from jax.experimental.pallas import tpu as pltpu

F32 = jnp.float32
BF16 = jnp.bfloat16
EPS = 1e-6

D_MODEL = 1024
CONV_W = 4
CONV_PAD = 8
PROMPT_CHUNK = 64
STEP_CHUNK = 8

DN_HEADS, DN_DK, DN_DV = 4, 128, 128
DN_QK = DN_HEADS * DN_DK
DN_V = DN_HEADS * DN_DV
DN_CONV_DIM = 2 * DN_QK + DN_V
SSM_HEADS, SSM_P, SSM_GROUPS, SSM_N = 8, 64, 2, 128
SSM_INNER = SSM_HEADS * SSM_P
SSM_CONV_DIM = SSM_INNER + 2 * SSM_GROUPS * SSM_N
GLA_HEADS, GLA_DK, GLA_DV, GLA_LR = 4, 128, 256, 16
GLA_K = GLA_HEADS * GLA_DK
GLA_V = GLA_HEADS * GLA_DV
GLA_NORMALIZER = 16.0
PEER_HEADS, PEER_DQ, PEER_NKEYS, PEER_TOPK = 8, 256, 128, 16
PEER_HALF = PEER_DQ // 2
PEER_N = PEER_NKEYS * PEER_NKEYS
LANES = 128
BF16_ROWS = 16
SMALL_W = LANES

EV_QKV, EV_GATE, EV_XBC, EV_Z, EV_SMALL = 0, 1536, 2048, 3072, 3584
EV_WIDTH = EV_SMALL + SMALL_W
OD_QKV, OD_OG, OD_GLR = 0, 2048, 3072
OD_WIDTH = OD_GLR + SMALL_W

VMEM_LIMIT = 48 * 1024 * 1024

_NT = (((1,), (1,)), ((), ()))
_TN = (((0,), (0,)), ((), ()))


def _dot(a, b):
    return jnp.dot(a.astype(BF16), b.astype(BF16), preferred_element_type=F32)


def _dot_nt(a, b):
    return lax.dot_general(a.astype(BF16), b.astype(BF16), _NT, preferred_element_type=F32)


def _dot_tn(a, b):
    return lax.dot_general(a.astype(BF16), b.astype(BF16), _TN, preferred_element_type=F32)


def _params(sem):
    return pltpu.CompilerParams(dimension_semantics=sem, vmem_limit_bytes=VMEM_LIMIT)


def _silu(x):
    return x * jax.nn.sigmoid(x)


def _gelu_tanh(x):
    c0 = 0.7978845608028654
    hx = 0.5 * x
    return hx + hx * jnp.tanh(x * (c0 + (c0 * 0.044715) * (x * x)))


def _softplus(x):
    return jnp.maximum(x, 0.0) + jnp.log(1.0 + jnp.exp(-jnp.abs(x)))


def _cumsum_rows(x):
    n = x.shape[0]
    row = lax.broadcasted_iota(jnp.int32, x.shape, 0)
    s = 1
    while s < n:
        x = x + jnp.where(row >= s, pltpu.roll(x, s, axis=0), 0.0)
        s *= 2
    return x


def _col_to_row(col, eye):
    return jnp.sum(jnp.where(eye, col, 0.0), axis=0, keepdims=True)


def _row_to_col(row, eye):
    return jnp.sum(jnp.where(eye, row, 0.0), axis=1, keepdims=True)


def _tri_masks(n):
    ri = lax.broadcasted_iota(jnp.int32, (n, n), 0)
    ci = lax.broadcasted_iota(jnp.int32, (n, n), 1)
    return ri, ci, ri == ci, ri >= ci, ri > ci


def _valid_rows(chunk, seq_len, shape):
    if seq_len % chunk == 0:
        return None
    pos = pl.program_id(1) * chunk + lax.broadcasted_iota(jnp.int32, shape, 0)
    return pos < seq_len


def _causal_conv(x_ref, xext, cw_ref, chunk):
    x = x_ref[...]
    xext[CONV_PAD:CONV_PAD + chunk, :] = x
    cw = cw_ref[...]
    pre = cw[CONV_W - 1:CONV_W, :] * x
    for j in range(1, CONV_W):
        pre = pre + cw[CONV_W - 1 - j:CONV_W - j, :] * xext[CONV_PAD - j:CONV_PAD - j + chunk, :]
    xext[0:CONV_PAD, :] = xext[chunk:chunk + CONV_PAD, :]
    return pre


def _norm_matmul_kernel(x_ref, lnw_ref, w_ref, o_ref):
    x = x_ref[...]
    y = x * lax.rsqrt(jnp.mean(x * x, axis=-1, keepdims=True) + EPS)
    h = (y * lnw_ref[...]).astype(BF16)
    o_ref[...] = jnp.dot(h, w_ref[...], preferred_element_type=F32)


def _norm_matmul(x, lnw, w):
    t, d = x.shape
    n = w.shape[1]
    tm = min(512, t)
    return pl.pallas_call(
        _norm_matmul_kernel,
        out_shape=jax.ShapeDtypeStruct((t, n), F32),
        grid=(t // tm,),
        in_specs=[pl.BlockSpec((tm, d), lambda i: (i, 0)),
                  pl.BlockSpec((1, d), lambda i: (0, 0)),
                  pl.BlockSpec((d, n), lambda i: (0, 0))],
        out_specs=pl.BlockSpec((tm, n), lambda i: (i, 0)),
        compiler_params=_params(("parallel",)),
        name="norm_matmul",
    )(x, lnw.reshape(1, d), w)


def _proj_residual_kernel(*refs, n_in):
    x_ref, o_ref = refs[0], refs[-1]
    acc = x_ref[...]
    for a_ref, w_ref in zip(refs[1:1 + n_in], refs[1 + n_in:1 + 2 * n_in]):
        acc = acc + jnp.dot(a_ref[...].astype(BF16), w_ref[...], preferred_element_type=F32)
    o_ref[...] = acc


def _proj_residual(x, acts, weights):
    t, d = x.shape
    tm = min(512, t)
    n_in = len(acts)
    in_specs = [pl.BlockSpec((tm, d), lambda i: (i, 0))]
    in_specs += [pl.BlockSpec((tm, a.shape[1]), lambda i: (i, 0)) for a in acts]
    in_specs += [pl.BlockSpec(w.shape, lambda i: (0, 0)) for w in weights]
    return pl.pallas_call(
        functools.partial(_proj_residual_kernel, n_in=n_in),
        out_shape=jax.ShapeDtypeStruct((t, d), F32),
        grid=(t // tm,),
        in_specs=in_specs,
        out_specs=pl.BlockSpec((tm, d), lambda i: (i, 0)),
        compiler_params=_params(("parallel",)),
        name="proj_residual",
    )(x, *acts, *weights)


def _final_norm_kernel(x_ref, w_ref, o_ref):
    x = x_ref[...]
    o_ref[...] = x * lax.rsqrt(jnp.mean(x * x, axis=-1, keepdims=True) + EPS) * w_ref[...]


def _final_norm(x, w):
    t, d = x.shape
    tm = min(512, t)
    return pl.pallas_call(
        _final_norm_kernel,
        out_shape=jax.ShapeDtypeStruct((t, d), F32),
        grid=(t // tm,),
        in_specs=[pl.BlockSpec((tm, d), lambda i: (i, 0)), pl.BlockSpec((1, d), lambda i: (0, 0))],
        out_specs=pl.BlockSpec((tm, d), lambda i: (i, 0)),
        compiler_params=_params(("parallel",)),
        name="final_norm",
    )(x, w.reshape(1, d))


def _unit_lower_inverse(m, ri, ci, eye):
    n = m.shape[0]
    t = jnp.where(eye, 1.0, 0.0)
    level = 0
    while (1 << level) < n:
        pair = ((ri >> (level + 1)) == (ci >> (level + 1))) & ((ri >> level) != (ci >> level))
        c = jnp.where(pair, m, 0.0)
        t = t - c if level == 0 else t - _dot(_dot(t, c), t)
        level += 1
    return t


def _dn_kernel(qkv_ref, gate_ref, small_ref, conv0_ref, s0_ref, cw_ref, par_ref, nw_ref,
               o_ref, s_out_ref, xext, state, *, chunk, seq_len):
    c_idx = pl.program_id(1)

    @pl.when(c_idx == 0)
    def _():
        xext[0:CONV_PAD, :] = conv0_ref[0]
        state[...] = s0_ref[0]

    act = _silu(_causal_conv(qkv_ref, xext, cw_ref, chunk))
    small = small_ref[...]
    par = par_ref[...]
    g_all = -jnp.exp(par[0:1, :]) * _softplus(small + par[1:2, :])
    beta_all = jax.nn.sigmoid(small)
    valid = _valid_rows(chunk, seq_len, small.shape)
    if valid is not None:
        g_all = jnp.where(valid, g_all, 0.0)
        beta_all = jnp.where(valid, beta_all, 0.0)
    gc_all = _cumsum_rows(g_all)

    ri, ci, eye, causal, strict = _tri_masks(chunk)
    nw = nw_ref[...]
    for h in range(DN_HEADS):
        gc = gc_all[:, h:h + 1]
        beta = beta_all[:, DN_HEADS + h:DN_HEADS + h + 1]
        decay = jnp.where(causal, jnp.exp(gc - _col_to_row(gc, eye)), 0.0)
        xq = act[:, h * DN_DK:(h + 1) * DN_DK]
        xk = act[:, DN_QK + h * DN_DK:DN_QK + (h + 1) * DN_DK]
        v = act[:, 2 * DN_QK + h * DN_DV:2 * DN_QK + (h + 1) * DN_DV]
        q = xq * lax.rsqrt(jnp.sum(xq * xq, axis=-1, keepdims=True) + EPS) * (DN_DK ** -0.5)
        k = xk * lax.rsqrt(jnp.sum(xk * xk, axis=-1, keepdims=True) + EPS)
        kb = k * beta
        eg = jnp.exp(gc)
        m = jnp.where(strict, _dot_nt(kb, k) * decay, 0.0)
        t_inv = _unit_lower_inverse(m, ri, ci, eye)
        u = _dot(t_inv, v * beta)
        w = _dot(t_inv, kb * eg)
        qk = jnp.where(causal, _dot_nt(q, k) * decay, 0.0)
        gc_last = gc[chunk - 1:chunk, :]
        s = state[h]
        v_new = u - _dot(w, s)
        o = _dot(q * eg, s) + _dot(qk, v_new)
        state[h] = s * jnp.exp(gc_last) + _dot_tn(k * jnp.exp(gc_last - gc), v_new)
        gate = gate_ref[:, h * DN_DV:(h + 1) * DN_DV]
        o = o * lax.rsqrt(jnp.mean(o * o, axis=-1, keepdims=True) + EPS) * nw
        o_ref[:, h * DN_DV:(h + 1) * DN_DV] = o * _silu(gate)

    @pl.when(c_idx == pl.num_programs(1) - 1)
    def _():
        s_out_ref[0] = state[...]


def _dn_mixer(proj, conv0, s0, cw, par, nw, *, batch, seq_len, chunk):
    nc = -(-seq_len // chunk)
    rows = batch * nc * chunk
    row = lambda b, c: b * nc + c
    return pl.pallas_call(
        functools.partial(_dn_kernel, chunk=chunk, seq_len=seq_len),
        out_shape=(jax.ShapeDtypeStruct((rows, DN_V), F32),
                   jax.ShapeDtypeStruct((batch, DN_HEADS, DN_DK, DN_DV), F32)),
        grid=(batch, nc),
        in_specs=[pl.BlockSpec((chunk, DN_CONV_DIM), lambda b, c: (row(b, c), EV_QKV // DN_CONV_DIM)),
                  pl.BlockSpec((chunk, DN_V), lambda b, c: (row(b, c), EV_GATE // DN_V)),
                  pl.BlockSpec((chunk, SMALL_W), lambda b, c: (row(b, c), EV_SMALL // SMALL_W)),
                  pl.BlockSpec((1, CONV_PAD, DN_CONV_DIM), lambda b, c: (b, 0, 0)),
                  pl.BlockSpec((1, DN_HEADS, DN_DK, DN_DV), lambda b, c: (b, 0, 0, 0)),
                  pl.BlockSpec((CONV_W, DN_CONV_DIM), lambda b, c: (0, 0)),
                  pl.BlockSpec((8, SMALL_W), lambda b, c: (0, 0)),
                  pl.BlockSpec((1, DN_DV), lambda b, c: (0, 0))],
        out_specs=(pl.BlockSpec((chunk, DN_V), lambda b, c: (row(b, c), 0)),
                   pl.BlockSpec((1, DN_HEADS, DN_DK, DN_DV), lambda b, c: (b, 0, 0, 0))),
        scratch_shapes=[pltpu.VMEM((chunk + CONV_PAD, DN_CONV_DIM), F32),
                        pltpu.VMEM((DN_HEADS, DN_DK, DN_DV), F32)],
        compiler_params=_params(("parallel", "arbitrary")),
        name="dn_mixer",
    )(proj, proj, proj, conv0, s0, cw, par, nw)


def _ssd_kernel(xbc_ref, z_ref, small_ref, conv0_ref, s0_ref, cw_ref, cb_ref, par_ref, nw_ref,
                y_ref, s_out_ref, xext, state, ybuf, *, chunk, seq_len):
    c_idx = pl.program_id(1)

    @pl.when(c_idx == 0)
    def _():
        xext[0:CONV_PAD, :] = conv0_ref[0]
        state[...] = s0_ref[0]

    xbc = _silu(_causal_conv(xbc_ref, xext, cw_ref, chunk) + cb_ref[...])
    small = small_ref[...]
    par = par_ref[...]
    dt_all = _softplus(small + par[1:2, :])
    valid = _valid_rows(chunk, seq_len, small.shape)
    if valid is not None:
        dt_all = jnp.where(valid, dt_all, 0.0)
    ac_all = _cumsum_rows(dt_all * -jnp.exp(par[0:1, :]))

    ri, ci, eye, causal, strict = _tri_masks(chunk)
    gw = SSM_GROUPS * SSM_N
    heads_per_group = SSM_HEADS // SSM_GROUPS
    for g in range(SSM_GROUPS):
        bm = xbc[:, SSM_INNER + g * SSM_N:SSM_INNER + (g + 1) * SSM_N]
        cm = xbc[:, SSM_INNER + gw + g * SSM_N:SSM_INNER + gw + (g + 1) * SSM_N]
        cb = _dot_nt(cm, bm)
        for hh in range(heads_per_group):
            h = g * heads_per_group + hh
            lane = 2 * DN_HEADS + h
            ac = ac_all[:, lane:lane + 1]
            dt = dt_all[:, lane:lane + 1]
            decay = jnp.where(causal, jnp.exp(ac - _col_to_row(ac, eye)), 0.0)
            xh = xbc[:, h * SSM_P:(h + 1) * SSM_P]
            xdt = xh * dt
            s = state[h]
            y = _dot(cb * decay, xdt) + _dot_nt(cm * jnp.exp(ac), s)
            ac_last = ac[chunk - 1:chunk, :]
            state[h] = s * jnp.exp(ac_last) + _dot_tn(xdt, bm * jnp.exp(ac_last - ac))
            ybuf[:, h * SSM_P:(h + 1) * SSM_P] = y + par[2:3, lane:lane + 1] * xh

    y = ybuf[...] * _silu(z_ref[...])
    nw = nw_ref[...]
    gn = SSM_INNER // SSM_GROUPS
    for g in range(SSM_GROUPS):
        yg = y[:, g * gn:(g + 1) * gn]
        yg = yg * lax.rsqrt(jnp.mean(yg * yg, axis=-1, keepdims=True) + EPS)
        y_ref[:, g * gn:(g + 1) * gn] = yg * nw[:, g * gn:(g + 1) * gn]

    @pl.when(c_idx == pl.num_programs(1) - 1)
    def _():
        s_out_ref[0] = state[...]


def _ssd_mixer(proj, conv0, s0, cw, cb, par, nw, *, batch, seq_len, chunk):
    nc = -(-seq_len // chunk)
    rows = batch * nc * chunk
    row = lambda b, c: b * nc + c
    return pl.pallas_call(
        functools.partial(_ssd_kernel, chunk=chunk, seq_len=seq_len),
        out_shape=(jax.ShapeDtypeStruct((rows, SSM_INNER), F32),
                   jax.ShapeDtypeStruct((batch, SSM_HEADS, SSM_P, SSM_N), F32)),
        grid=(batch, nc),
        in_specs=[pl.BlockSpec((chunk, SSM_CONV_DIM), lambda b, c: (row(b, c), EV_XBC // SSM_CONV_DIM)),
                  pl.BlockSpec((chunk, SSM_INNER), lambda b, c: (row(b, c), EV_Z // SSM_INNER)),
                  pl.BlockSpec((chunk, SMALL_W), lambda b, c: (row(b, c), EV_SMALL // SMALL_W)),
                  pl.BlockSpec((1, CONV_PAD, SSM_CONV_DIM), lambda b, c: (b, 0, 0)),
                  pl.BlockSpec((1, SSM_HEADS, SSM_P, SSM_N), lambda b, c: (b, 0, 0, 0)),
                  pl.BlockSpec((CONV_W, SSM_CONV_DIM), lambda b, c: (0, 0)),
                  pl.BlockSpec((1, SSM_CONV_DIM), lambda b, c: (0, 0)),
                  pl.BlockSpec((8, SMALL_W), lambda b, c: (0, 0)),
                  pl.BlockSpec((1, SSM_INNER), lambda b, c: (0, 0))],
        out_specs=(pl.BlockSpec((chunk, SSM_INNER), lambda b, c: (row(b, c), 0)),
                   pl.BlockSpec((1, SSM_HEADS, SSM_P, SSM_N), lambda b, c: (b, 0, 0, 0))),
        scratch_shapes=[pltpu.VMEM((chunk + CONV_PAD, SSM_CONV_DIM), F32),
                        pltpu.VMEM((SSM_HEADS, SSM_P, SSM_N), F32),
                        pltpu.VMEM((chunk, SSM_INNER), F32)],
        compiler_params=_params(("parallel", "arbitrary")),
        name="ssd_mixer",
    )(proj, proj, proj, conv0, s0, cw, cb, par, nw)


def _gla_kernel(qkv_ref, og_ref, glr_ref, s0_ref, wg_ref, bg_ref, nw_ref,
                o_ref, s_out_ref, state, *, chunk, seq_len):
    c_idx = pl.program_id(1)

    @pl.when(c_idx == 0)
    def _():
        state[...] = s0_ref[0]

    x = bg_ref[...] + jnp.dot(glr_ref[...].astype(BF16), wg_ref[...], preferred_element_type=F32)
    gk = (jnp.minimum(x, 0.0) - jnp.log(1.0 + jnp.exp(-jnp.abs(x)))) / GLA_NORMALIZER
    valid = _valid_rows(chunk, seq_len, gk.shape)
    if valid is not None:
        gk = jnp.where(valid, gk, 0.0)
    b_all = _cumsum_rows(gk)

    ri, ci, eye_c, causal, strict = _tri_masks(chunk)
    eye_k = _tri_masks(GLA_DK)[2]
    nw = nw_ref[...]
    for h in range(GLA_HEADS):
        b = b_all[:, h * GLA_DK:(h + 1) * GLA_DK]
        q = qkv_ref[:, h * GLA_DK:(h + 1) * GLA_DK] * (GLA_DK ** -0.5)
        k = qkv_ref[:, GLA_K + h * GLA_DK:GLA_K + (h + 1) * GLA_DK]
        v = qkv_ref[:, 2 * GLA_K + h * GLA_DV:2 * GLA_K + (h + 1) * GLA_DV]
        qe = q * jnp.exp(b)
        att = jnp.where(causal, _dot_nt(qe, k * jnp.exp(-b)), 0.0)
        s = state[h]
        o = _dot(qe, s) + _dot(att, v)
        b_last = b[chunk - 1:chunk, :]
        state[h] = s * _row_to_col(jnp.exp(b_last), eye_k) + _dot_tn(k * jnp.exp(b_last - b), v)
        o = o * lax.rsqrt(jnp.mean(o * o, axis=-1, keepdims=True) + EPS) * nw
        o_ref[:, h * GLA_DV:(h + 1) * GLA_DV] = o * _silu(og_ref[:, h * GLA_DV:(h + 1) * GLA_DV])

    @pl.when(c_idx == pl.num_programs(1) - 1)
    def _():
        s_out_ref[0] = state[...]


def _gla_mixer(proj, s0, wg, bg, nw, *, batch, seq_len, chunk):
    nc = -(-seq_len // chunk)
    rows = batch * nc * chunk
    row = lambda b, c: b * nc + c
    qkv_w = 2 * GLA_K + GLA_V
    return pl.pallas_call(
        functools.partial(_gla_kernel, chunk=chunk, seq_len=seq_len),
        out_shape=(jax.ShapeDtypeStruct((rows, GLA_V), F32),
                   jax.ShapeDtypeStruct((batch, GLA_HEADS, GLA_DK, GLA_DV), F32)),
        grid=(batch, nc),
        in_specs=[pl.BlockSpec((chunk, qkv_w), lambda b, c: (row(b, c), OD_QKV // qkv_w)),
                  pl.BlockSpec((chunk, GLA_V), lambda b, c: (row(b, c), OD_OG // GLA_V)),
                  pl.BlockSpec((chunk, SMALL_W), lambda b, c: (row(b, c), OD_GLR // SMALL_W)),
                  pl.BlockSpec((1, GLA_HEADS, GLA_DK, GLA_DV), lambda b, c: (b, 0, 0, 0)),
                  pl.BlockSpec((SMALL_W, GLA_K), lambda b, c: (0, 0)),
                  pl.BlockSpec((1, GLA_K), lambda b, c: (0, 0)),
                  pl.BlockSpec((1, GLA_DV), lambda b, c: (0, 0))],
        out_specs=(pl.BlockSpec((chunk, GLA_V), lambda b, c: (row(b, c), 0)),
                   pl.BlockSpec((1, GLA_HEADS, GLA_DK, GLA_DV), lambda b, c: (b, 0, 0, 0))),
        scratch_shapes=[pltpu.VMEM((GLA_HEADS, GLA_DK, GLA_DV), F32)],
        compiler_params=_params(("parallel", "arbitrary")),
        name="gla_mixer",
    )(proj, proj, proj, s0, wg, bg, nw)


def _take_top(x, key, count):
    nr, nt = x.shape
    big = jnp.int32(2 ** 30)
    slot = lax.broadcasted_iota(jnp.int32, (count, nt), 0)

    def body(r, carry):
        x, rank, vals = carry
        m = jnp.max(x, axis=0, keepdims=True)
        first = jnp.min(jnp.where(x == m, key, big), axis=0, keepdims=True)
        hit = key == first
        return (jnp.where(hit, -jnp.inf, x), jnp.where(hit, r, rank), jnp.where(slot == r, m, vals))

    init = (x, jnp.full((nr, nt), count, jnp.int32), jnp.zeros((count, nt), F32))
    _, rank, vals = lax.fori_loop(0, count, body, init)
    return vals, rank


def _pair_candidates(v1, v2):
    k = PEER_TOPK
    nt = v1.shape[1]
    j8 = lax.broadcasted_iota(jnp.int32, (8, nt), 0)
    j16 = lax.broadcasted_iota(jnp.int32, (k, nt), 0)
    sums, pos = [v1[0:1, :] + v2], [j16]
    for i in range(1, 8):
        sums.append(jnp.where(j8 < k // (i + 1), v1[i:i + 1, :] + v2[0:8, :], -jnp.inf))
        pos.append(i * k + j8)
    sums.append(v1[8:k, :] + v2[0:1, :])
    pos.append((8 + j8) * k)
    return jnp.concatenate(sums, axis=0), jnp.concatenate(pos, axis=0)


def _peer_route_kernel(q_ref, k1_ref, k2_ref, n_ref, e1_ref, r2_ref, e2_ref, s1_ref, s2_ref):
    k = PEER_TOPK
    q = q_ref[...]
    s1_ref[...] = _dot_nt(k1_ref[0], q[:, :PEER_HALF])
    s2_ref[...] = _dot_nt(k2_ref[0], q[:, PEER_HALF:])
    tt = q.shape[0]
    lw = min(LANES, tt)
    for lc in range(tt // lw):
        sl = slice(lc * lw, (lc + 1) * lw)
        s1 = s1_ref[:, sl]
        s2 = s2_ref[:, sl]
        key = lax.broadcasted_iota(jnp.int32, s1.shape, 0)
        v1, rank1 = _take_top(s1, key, k)
        v2, rank2 = _take_top(s2, key, k)
        cand, pos = _pair_candidates(v1, v2)
        _, crank = _take_top(cand, pos, k)
        sel = crank < k
        top = v1[0:1, :] + v2[0:1, :]
        z = jnp.sum(jnp.where(sel, jnp.exp(cand - top), 0.0), axis=0, keepdims=True)
        cnt = jnp.where(sel, 1.0, 0.0)
        n = jnp.zeros(s1.shape, F32)
        for i in range(k):
            if i == 0:
                n_i = jnp.sum(cnt[0:k, :], axis=0, keepdims=True)
            elif i < 8:
                n_i = jnp.sum(cnt[k + 8 * (i - 1):k + 8 * i, :], axis=0, keepdims=True)
            else:
                n_i = cnt[k + 56 + i - 8:k + 56 + i - 7, :]
            n = jnp.where(rank1 == i, n_i, n)
        n_ref[0, :, sl] = n
        e1_ref[0, :, sl] = jnp.where(rank1 < k, jnp.exp(s1 - v1[0:1, :]) / z, 0.0)
        r2_ref[:, sl] = rank2.astype(F32).astype(BF16)
        e2_ref[:, sl] = jnp.where(rank2 < k, jnp.exp(s2 - v2[0:1, :]), 0.0).astype(BF16)


def _peer_route(q, keys1, keys2, tt):
    t = q.shape[0]
    out_a = jax.ShapeDtypeStruct((PEER_HEADS, PEER_NKEYS, t), F32)
    out_b = jax.ShapeDtypeStruct((PEER_HEADS * PEER_NKEYS, t), BF16)
    spec_a = pl.BlockSpec((1, PEER_NKEYS, tt), lambda i, h: (h, 0, i))
    spec_b = pl.BlockSpec((PEER_NKEYS, tt), lambda i, h: (h, i))
    key_spec = pl.BlockSpec((1, PEER_NKEYS, PEER_HALF), lambda i, h: (h, 0, 0))
    return pl.pallas_call(
        _peer_route_kernel,
        out_shape=(out_a, out_a, out_b, out_b),
        grid=(t // tt, PEER_HEADS),
        in_specs=[pl.BlockSpec((tt, PEER_DQ), lambda i, h: (i, h)), key_spec, key_spec],
        out_specs=(spec_a, spec_a, spec_b, spec_b),
        scratch_shapes=[pltpu.VMEM((PEER_NKEYS, tt), F32), pltpu.VMEM((PEER_NKEYS, tt), F32)],
        compiler_params=_params(("parallel", "arbitrary")),
        name="peer_route",
    )(q, keys1, keys2)


def _peer_dense_kernel(x_ref, lnw_ref, n_ref, e1_ref, r2_in_ref, e2_in_ref, u_ref, v_ref, o_ref,
                       h_ref, pre_ref, act_ref, acc_ref, r2_ref, e2_ref, *, rows_per_step):
    j = pl.program_id(1)

    @pl.when(j == 0)
    def _():
        x = x_ref[...]
        y = x * lax.rsqrt(jnp.mean(x * x, axis=-1, keepdims=True) + EPS)
        h_ref[...] = (y * lnw_ref[...]).astype(BF16)
        acc_ref[...] = jnp.zeros_like(acc_ref)
        r2_ref[...] = r2_in_ref[...]
        e2_ref[...] = e2_in_ref[...]

    pre_ref[...] = lax.dot_general(u_ref[...], h_ref[...], _NT, preferred_element_type=F32)
    tt = h_ref.shape[0]
    lw = min(LANES, tt)
    reps = PEER_NKEYS // BF16_ROWS
    zero = jnp.zeros((PEER_NKEYS, lw), BF16)
    for lc in range(tt // lw):
        sl = slice(lc * lw, (lc + 1) * lw)
        for aa in range(rows_per_step):
            rows = slice(aa * PEER_NKEYS, (aa + 1) * PEER_NKEYS)
            gate = zero
            for h in range(PEER_HEADS):
                n_row = jnp.broadcast_to(n_ref[h, aa:aa + 1, sl], (BF16_ROWS, lw)).astype(BF16)
                e1_row = jnp.broadcast_to(e1_ref[h, aa:aa + 1, sl], (BF16_ROWS, lw)).astype(BF16)
                n_tile = jnp.concatenate([n_row] * reps, axis=0)
                e1_tile = jnp.concatenate([e1_row] * reps, axis=0)
                keys = slice(h * PEER_NKEYS, (h + 1) * PEER_NKEYS)
                gate = gate + jnp.where(r2_ref[keys, sl] < n_tile, e2_ref[keys, sl], zero) * e1_tile
            act_ref[rows, sl] = _gelu_tanh(pre_ref[rows, sl]).astype(BF16) * gate
    acc_ref[...] += lax.dot_general(act_ref[...], v_ref[...], _TN, preferred_element_type=F32)

    @pl.when(j == pl.num_programs(1) - 1)
    def _():
        o_ref[...] = x_ref[...] + acc_ref[...]


def _peer_dense(x, lnw, route, u, v, tt, rows_per_step):
    t, d = x.shape
    eb = rows_per_step * PEER_NKEYS
    spec_a = pl.BlockSpec((PEER_HEADS, rows_per_step, tt), lambda i, j: (0, j, i))
    spec_b = pl.BlockSpec((PEER_HEADS * PEER_NKEYS, tt), lambda i, j: (0, i))
    return pl.pallas_call(
        functools.partial(_peer_dense_kernel, rows_per_step=rows_per_step),
        out_shape=jax.ShapeDtypeStruct((t, d), F32),
        grid=(t // tt, PEER_N // eb),
        in_specs=[pl.BlockSpec((tt, d), lambda i, j: (i, 0)),
                  pl.BlockSpec((1, d), lambda i, j: (0, 0)),
                  spec_a, spec_a, spec_b, spec_b,
                  pl.BlockSpec((eb, d), lambda i, j: (j, 0)),
                  pl.BlockSpec((eb, d), lambda i, j: (j, 0))],
        out_specs=pl.BlockSpec((tt, d), lambda i, j: (i, 0)),
        scratch_shapes=[pltpu.VMEM((tt, d), BF16), pltpu.VMEM((eb, tt), F32),
                        pltpu.VMEM((eb, tt), BF16), pltpu.VMEM((tt, d), F32),
                        pltpu.VMEM((PEER_HEADS * PEER_NKEYS, tt), BF16),
                        pltpu.VMEM((PEER_HEADS * PEER_NKEYS, tt), BF16)],
        compiler_params=_params(("parallel", "arbitrary")),
        name="peer_dense",
    )(x, lnw.reshape(1, d), *route, u, v)


def _peer_ffn(x, lnw, wq, keys1, keys2, u, v):
    tt = min(512, x.shape[0])
    q = _norm_matmul(x, lnw, wq)
    route = _peer_route(q, keys1, keys2, tt)
    return _peer_dense(x, lnw, route, u, v, tt, rows_per_step=8)


def _pad_lanes(vec, offset):
    return jnp.zeros((SMALL_W,), F32).at[offset:offset + vec.shape[0]].set(vec.astype(F32))


def _conv_history(buf):
    return jnp.pad(buf, ((0, 0), (CONV_PAD - (CONV_W - 1), 0), (0, 0)))


def kernel(x_prompt, x_sample, state_dn_conv, state_dn, state_ssm_conv, state_ssm, state_gla, ln_mix, ln_ffn, ln_final, w_in_even, w_out_even, dn_conv_w, dn_a_log, dn_dt_bias, dn_norm_w, ssm_conv_w, ssm_conv_b, ssm_a_log, ssm_dt_bias, ssm_d, ssm_norm_w, w_in_odd, gla_w_gate2, gla_b_gate, gla_norm_w, w_out_odd, peer_w_q, peer_keys1, peer_keys2, peer_u, peer_v):
    we = w_in_even[0]
    o = 0
    cuts = {}
    for name, size in (("qkv", DN_CONV_DIM), ("a", DN_HEADS), ("b", DN_HEADS), ("gate", DN_V),
                       ("z", SSM_INNER), ("xbc", SSM_CONV_DIM), ("dt", SSM_HEADS)):
        cuts[name] = we[:, o:o + size]
        o += size
    small_w = jnp.concatenate([cuts["a"], cuts["b"], cuts["dt"]], axis=1)
    small_w = jnp.pad(small_w, ((0, 0), (0, SMALL_W - small_w.shape[1])))
    w_even = jnp.concatenate([cuts["qkv"], cuts["gate"], cuts["xbc"], cuts["z"], small_w], axis=1).astype(BF16)
    wo = w_in_odd[0]
    qkv_w = 2 * GLA_K + GLA_V
    glr_w = jnp.pad(wo[:, qkv_w:qkv_w + GLA_LR], ((0, 0), (0, SMALL_W - GLA_LR)))
    w_odd = jnp.concatenate([wo[:, :qkv_w], wo[:, qkv_w + GLA_LR:], glr_w], axis=1).astype(BF16)
    w_out_dn = w_out_even[0, :DN_V].astype(BF16)
    w_out_ssm = w_out_even[0, DN_V:].astype(BF16)
    w_out_gla = w_out_odd[0].astype(BF16)
    wg = jnp.pad(gla_w_gate2[0], ((0, SMALL_W - GLA_LR), (0, 0))).astype(BF16)
    wq = peer_w_q.astype(BF16)
    k1 = peer_keys1.astype(BF16)
    k2 = peer_keys2.astype(BF16)
    pu = peer_u.astype(BF16)
    pv = peer_v.astype(BF16)

    zero8 = jnp.zeros((8 - 2, SMALL_W), F32)
    dn_par = jnp.concatenate([_pad_lanes(dn_a_log[0], 0)[None], _pad_lanes(dn_dt_bias[0], 0)[None], zero8])
    lane0 = 2 * DN_HEADS
    ssm_par = jnp.concatenate([_pad_lanes(ssm_a_log[0], lane0)[None], _pad_lanes(ssm_dt_bias[0], lane0)[None],
                               _pad_lanes(ssm_d[0], lane0)[None], zero8[1:]])

    def trunk(x, batch, seq_len, conv_dn0, s_dn0, conv_ssm0, s_ssm0, s_gla0):
        chunk = min(PROMPT_CHUNK, seq_len) if seq_len >= PROMPT_CHUNK else STEP_CHUNK
        nc = -(-seq_len // chunk)
        padded = nc * chunk != seq_len

        def to_seq(a):
            if not padded:
                return a
            a = a.reshape(batch, seq_len, a.shape[-1])
            return jnp.pad(a, ((0, 0), (0, nc * chunk - seq_len), (0, 0))).reshape(batch * nc * chunk, -1)

        def from_seq(a):
            if not padded:
                return a
            return a.reshape(batch, nc * chunk, a.shape[-1])[:, :seq_len].reshape(batch * seq_len, -1)

        def last_rows(buf, new):
            new = new.reshape(batch, seq_len, new.shape[-1])
            return jnp.concatenate([buf, new], axis=1)[:, -(CONV_W - 1):]

        seq = dict(batch=batch, seq_len=seq_len, chunk=chunk)

        proj = _norm_matmul(x, ln_mix[0], w_even)
        proj_seq = to_seq(proj)
        o_dn, s_dn = _dn_mixer(proj_seq, _conv_history(conv_dn0[0]), s_dn0[0], dn_conv_w[0], dn_par,
                               dn_norm_w[0].reshape(1, DN_DV), **seq)
        y_ssm, s_ssm = _ssd_mixer(proj_seq, _conv_history(conv_ssm0[0]), s_ssm0[0], ssm_conv_w[0],
                                  ssm_conv_b[0].reshape(1, SSM_CONV_DIM), ssm_par,
                                  ssm_norm_w[0].reshape(1, SSM_INNER), **seq)
        conv_dn = last_rows(conv_dn0[0], proj[:, EV_QKV:EV_QKV + DN_CONV_DIM])
        conv_ssm = last_rows(conv_ssm0[0], proj[:, EV_XBC:EV_XBC + SSM_CONV_DIM])
        x = _proj_residual(x, [from_seq(o_dn), from_seq(y_ssm)], [w_out_dn, w_out_ssm])
        x = _peer_ffn(x, ln_ffn[0], wq[0], k1[0], k2[0], pu[0], pv[0])

        proj = _norm_matmul(x, ln_mix[1], w_odd)
        o_gla, s_gla = _gla_mixer(to_seq(proj), s_gla0[0], wg, gla_b_gate[0].reshape(1, GLA_K),
                                  gla_norm_w[0].reshape(1, GLA_DV), **seq)
        x = _proj_residual(x, [from_seq(o_gla)], [w_out_gla])
        x = _peer_ffn(x, ln_ffn[1], wq[1], k1[1], k2[1], pu[1], pv[1])

        y = _final_norm(x, ln_final).reshape(batch, seq_len, D_MODEL)
        return y, conv_dn[None], s_dn[None], conv_ssm[None], s_ssm[None], s_gla[None]

    bp, lp = x_prompt.shape[:2]
    bs, ls = x_sample.shape[:2]

    def zeros_like_state(s):
        return jnp.zeros((s.shape[0], bp) + s.shape[2:], s.dtype)

    prompt = trunk(x_prompt.reshape(bp * lp, D_MODEL), bp, lp, zeros_like_state(state_dn_conv),
                   zeros_like_state(state_dn), zeros_like_state(state_ssm_conv),
                   zeros_like_state(state_ssm), zeros_like_state(state_gla))
    sample = trunk(x_sample.reshape(bs * ls, D_MODEL), bs, ls, state_dn_conv, state_dn,
                   state_ssm_conv, state_ssm, state_gla)
    return (prompt[0], sample[0]) + prompt[1:] + sample[1:]
```

```python
import functools

import jax
import jax.numpy as jnp
from jax import lax
from jax.experimental import pallas as pl
from jax.experimental.pallas import tpu as pltpu

F32 = jnp.float32
BF16 = jnp.bfloat16
EPS = 1e-6

D_MODEL = 1024
CONV_W = 4
CONV_PAD = 8
PROMPT_CHUNK = 64
STEP_CHUNK = 8

DN_HEADS, DN_DK, DN_DV = 4, 128, 128
DN_QK = DN_HEADS * DN_DK
DN_V = DN_HEADS * DN_DV
DN_CONV_DIM = 2 * DN_QK + DN_V
SSM_HEADS, SSM_P, SSM_GROUPS, SSM_N = 8, 64, 2, 128
SSM_INNER = SSM_HEADS * SSM_P
SSM_CONV_DIM = SSM_INNER + 2 * SSM_GROUPS * SSM_N
GLA_HEADS, GLA_DK, GLA_DV, GLA_LR = 4, 128, 256, 16
GLA_K = GLA_HEADS * GLA_DK
GLA_V = GLA_HEADS * GLA_DV
GLA_NORMALIZER = 16.0
PEER_HEADS, PEER_DQ, PEER_NKEYS, PEER_TOPK = 8, 256, 128, 16
PEER_HALF = PEER_DQ // 2
PEER_N = PEER_NKEYS * PEER_NKEYS
LANES = 128
BF16_ROWS = 16
SMALL_W = LANES

EV_QKV, EV_GATE, EV_XBC, EV_Z, EV_SMALL = 0, 1536, 2048, 3072, 3584
EV_WIDTH = EV_SMALL + SMALL_W
OD_QKV, OD_OG, OD_GLR = 0, 2048, 3072
OD_WIDTH = OD_GLR + SMALL_W

VMEM_LIMIT = 48 * 1024 * 1024

_NT = (((1,), (1,)), ((), ()))
_TN = (((0,), (0,)), ((), ()))


def _dot(a, b):
    return jnp.dot(a.astype(BF16), b.astype(BF16), preferred_element_type=F32)


def _dot_nt(a, b):
    return lax.dot_general(a.astype(BF16), b.astype(BF16), _NT, preferred_element_type=F32)


def _dot_tn(a, b):
    return lax.dot_general(a.astype(BF16), b.astype(BF16), _TN, preferred_element_type=F32)


def _params(sem):
    return pltpu.CompilerParams(dimension_semantics=sem, vmem_limit_bytes=VMEM_LIMIT)


def _silu(x):
    return x * jax.nn.sigmoid(x)


def _gelu_tanh(x):
    c0 = 0.7978845608028654
    hx = 0.5 * x
    return hx + hx * jnp.tanh(x * (c0 + (c0 * 0.044715) * (x * x)))


def _softplus(x):
    return jnp.maximum(x, 0.0) + jnp.log(1.0 + jnp.exp(-jnp.abs(x)))


def _cumsum_rows(x):
    n = x.shape[0]
    row = lax.broadcasted_iota(jnp.int32, x.shape, 0)
    s = 1
    while s < n:
        x = x + jnp.where(row >= s, pltpu.roll(x, s, axis=0), 0.0)
        s *= 2
    return x


def _col_to_row(col, eye):
    return jnp.sum(jnp.where(eye, col, 0.0), axis=0, keepdims=True)


def _row_to_col(row, eye):
    return jnp.sum(jnp.where(eye, row, 0.0), axis=1, keepdims=True)


def _tri_masks(n):
    ri = lax.broadcasted_iota(jnp.int32, (n, n), 0)
    ci = lax.broadcasted_iota(jnp.int32, (n, n), 1)
    return ri, ci, ri == ci, ri >= ci, ri > ci


def _valid_rows(chunk, seq_len, shape):
    if seq_len % chunk == 0:
        return None
    pos = pl.program_id(1) * chunk + lax.broadcasted_iota(jnp.int32, shape, 0)
    return pos < seq_len


def _causal_conv(x_ref, xext, cw_ref, chunk):
    x = x_ref[...]
    xext[CONV_PAD:CONV_PAD + chunk, :] = x
    cw = cw_ref[...]
    pre = cw[CONV_W - 1:CONV_W, :] * x
    for j in range(1, CONV_W):
        pre = pre + cw[CONV_W - 1 - j:CONV_W - j, :] * xext[CONV_PAD - j:CONV_PAD - j + chunk, :]
    xext[0:CONV_PAD, :] = xext[chunk:chunk + CONV_PAD, :]
    return pre


def _norm_matmul_kernel(x_ref, lnw_ref, w_ref, o_ref):
    x = x_ref[...]
    y = x * lax.rsqrt(jnp.mean(x * x, axis=-1, keepdims=True) + EPS)
    h = (y * lnw_ref[...]).astype(BF16)
    o_ref[...] = jnp.dot(h, w_ref[...], preferred_element_type=F32)


def _norm_matmul(x, lnw, w):
    t, d = x.shape
    n = w.shape[1]
    tm = min(512, t)
    return pl.pallas_call(
        _norm_matmul_kernel,
        out_shape=jax.ShapeDtypeStruct((t, n), F32),
        grid=(t // tm,),
        in_specs=[pl.BlockSpec((tm, d), lambda i: (i, 0)),
                  pl.BlockSpec((1, d), lambda i: (0, 0)),
                  pl.BlockSpec((d, n), lambda i: (0, 0))],
        out_specs=pl.BlockSpec((tm, n), lambda i: (i, 0)),
        compiler_params=_params(("parallel",)),
        name="norm_matmul",
    )(x, lnw.reshape(1, d), w)


def _proj_residual_kernel(*refs, n_in):
    x_ref, o_ref = refs[0], refs[-1]
    acc = x_ref[...]
    for a_ref, w_ref in zip(refs[1:1 + n_in], refs[1 + n_in:1 + 2 * n_in]):
        acc = acc + jnp.dot(a_ref[...].astype(BF16), w_ref[...], preferred_element_type=F32)
    o_ref[...] = acc


def _proj_residual(x, acts, weights):
    t, d = x.shape
    tm = min(512, t)
    n_in = len(acts)
    in_specs = [pl.BlockSpec((tm, d), lambda i: (i, 0))]
    in_specs += [pl.BlockSpec((tm, a.shape[1]), lambda i: (i, 0)) for a in acts]
    in_specs += [pl.BlockSpec(w.shape, lambda i: (0, 0)) for w in weights]
    return pl.pallas_call(
        functools.partial(_proj_residual_kernel, n_in=n_in),
        out_shape=jax.ShapeDtypeStruct((t, d), F32),
        grid=(t // tm,),
        in_specs=in_specs,
        out_specs=pl.BlockSpec((tm, d), lambda i: (i, 0)),
        compiler_params=_params(("parallel",)),
        name="proj_residual",
    )(x, *acts, *weights)


def _final_norm_kernel(x_ref, w_ref, o_ref):
    x = x_ref[...]
    o_ref[...] = x * lax.rsqrt(jnp.mean(x * x, axis=-1, keepdims=True) + EPS) * w_ref[...]


def _final_norm(x, w):
    t, d = x.shape
    tm = min(512, t)
    return pl.pallas_call(
        _final_norm_kernel,
        out_shape=jax.ShapeDtypeStruct((t, d), F32),
        grid=(t // tm,),
        in_specs=[pl.BlockSpec((tm, d), lambda i: (i, 0)), pl.BlockSpec((1, d), lambda i: (0, 0))],
        out_specs=pl.BlockSpec((tm, d), lambda i: (i, 0)),
        compiler_params=_params(("parallel",)),
        name="final_norm",
    )(x, w.reshape(1, d))


def _unit_lower_inverse(m, ri, ci, eye, block):
    t = jnp.where(eye, 1.0, 0.0)
    level = 0
    while (1 << level) < block:
        pair = ((ri >> (level + 1)) == (ci >> (level + 1))) & ((ri >> level) != (ci >> level))
        c = jnp.where(pair, m, 0.0)
        t = t - c if level == 0 else t - _dot(_dot(t, c), t)
        level += 1
    return t


def _dn_kernel(qkv_ref, gate_ref, small_ref, conv0_ref, s0_ref, cw_ref, par_ref, nw_ref,
               o_ref, s_out_ref, xext, state, *, chunk, seq_len, nb):
    c_idx = pl.program_id(1)
    groups = nb * DN_HEADS
    log2c = chunk.bit_length() - 1
    group_lanes = lambda g: slice(g * DN_DV, (g + 1) * DN_DV)

    @pl.when(c_idx == 0)
    def _():
        for s in range(nb):
            xext[s, 0:CONV_PAD, :] = conv0_ref[s]
            for h in range(DN_HEADS):
                state[:, group_lanes(s * DN_HEADS + h)] = s0_ref[s, h]

    par = par_ref[...]
    qs, ks, vs, gcs, betas, gates = [], [], [], [], [], []
    for s in range(nb):
        act = _silu(_causal_conv(qkv_ref.at[s], xext.at[s], cw_ref, chunk))
        small = small_ref[s]
        g_all = -jnp.exp(par[0:1, :]) * _softplus(small + par[1:2, :])
        beta_all = jax.nn.sigmoid(small)
        valid = _valid_rows(chunk, seq_len, small.shape)
        if valid is not None:
            g_all = jnp.where(valid, g_all, 0.0)
            beta_all = jnp.where(valid, beta_all, 0.0)
        gc_all = _cumsum_rows(g_all)
        for h in range(DN_HEADS):
            xq = act[:, h * DN_DK:(h + 1) * DN_DK]
            xk = act[:, DN_QK + h * DN_DK:DN_QK + (h + 1) * DN_DK]
            qs.append(xq * lax.rsqrt(jnp.sum(xq * xq, axis=-1, keepdims=True) + EPS) * (DN_DK ** -0.5))
            ks.append(xk * lax.rsqrt(jnp.sum(xk * xk, axis=-1, keepdims=True) + EPS))
            vs.append(act[:, 2 * DN_QK + h * DN_DV:2 * DN_QK + (h + 1) * DN_DV])
            gcs.append(gc_all[:, h:h + 1])
            betas.append(beta_all[:, DN_HEADS + h:DN_HEADS + h + 1])
            gates.append(gate_ref[s, :, h * DN_DV:(h + 1) * DN_DV])
    q, k, v = (jnp.concatenate(t, axis=0) for t in (qs, ks, vs))
    gc, beta = (jnp.concatenate(t, axis=0) for t in (gcs, betas))
    gc_last = jnp.concatenate([jnp.broadcast_to(t[chunk - 1:chunk, :], (chunk, 1)) for t in gcs], axis=0)
    last_decay = jnp.concatenate(
        [jnp.broadcast_to(jnp.exp(t[chunk - 1:chunk, :]), (1, DN_DV)) for t in gcs], axis=1)

    rows = groups * chunk
    ri, ci, eye, lower, strict_lower = _tri_masks(rows)
    same = (ri >> log2c) == (ci >> log2c)
    causal = same & lower
    strict = same & strict_lower
    decay = jnp.where(causal, jnp.exp(gc - _col_to_row(gc, eye)), 0.0)
    kb = k * beta
    eg = jnp.exp(gc)
    m = jnp.where(strict, _dot_nt(kb, k) * decay, 0.0)
    t_inv = _unit_lower_inverse(m, ri, ci, eye, chunk)
    sol = _dot(t_inv, jnp.concatenate([v * beta, kb * eg], axis=1))
    u, w = sol[:, :DN_DV], sol[:, DN_DV:]
    qk = jnp.where(causal, _dot_nt(q, k) * decay, 0.0)

    def own_block(x):
        return jnp.concatenate(
            [x[g * chunk:(g + 1) * chunk, group_lanes(g)] for g in range(groups)], axis=0)

    s_all = state[...]
    v_new = u - own_block(_dot(w, s_all))
    o = own_block(_dot(q * eg, s_all)) + _dot(qk, v_new)
    wide = (rows, groups * DN_DV)
    row_group = lax.broadcasted_iota(jnp.int32, wide, 0) >> log2c
    lane_group = lax.broadcasted_iota(jnp.int32, wide, 1) // DN_DV
    v_wide = jnp.where(row_group == lane_group, jnp.concatenate([v_new] * groups, axis=1), 0.0)
    state[...] = s_all * last_decay + _dot_tn(k * jnp.exp(gc_last - gc), v_wide)

    o = o * lax.rsqrt(jnp.mean(o * o, axis=-1, keepdims=True) + EPS) * nw_ref[...]
    y = o * _silu(jnp.concatenate(gates, axis=0))
    for s in range(nb):
        for h in range(DN_HEADS):
            g = s * DN_HEADS + h
            o_ref[s, :, h * DN_DV:(h + 1) * DN_DV] = y[g * chunk:(g + 1) * chunk, :]

    @pl.when(c_idx == pl.num_programs(1) - 1)
    def _():
        for s in range(nb):
            for h in range(DN_HEADS):
                s_out_ref[s, h] = state[:, group_lanes(s * DN_HEADS + h)]


def _dn_mixer(proj, conv0, s0, cw, par, nw, *, batch, seq_len, chunk, nb):
    nc = proj.shape[1] // chunk
    return pl.pallas_call(
        functools.partial(_dn_kernel, chunk=chunk, seq_len=seq_len, nb=nb),
        out_shape=(jax.ShapeDtypeStruct((batch, nc * chunk, DN_V), F32),
                   jax.ShapeDtypeStruct((batch, DN_HEADS, DN_DK, DN_DV), F32)),
        grid=(batch // nb, nc),
        in_specs=[pl.BlockSpec((nb, chunk, DN_CONV_DIM), lambda b, c: (b, c, EV_QKV // DN_CONV_DIM)),
                  pl.BlockSpec((nb, chunk, DN_V), lambda b, c: (b, c, EV_GATE // DN_V)),
                  pl.BlockSpec((nb, chunk, SMALL_W), lambda b, c: (b, c, EV_SMALL // SMALL_W)),
                  pl.BlockSpec((nb, CONV_PAD, DN_CONV_DIM), lambda b, c: (b, 0, 0)),
                  pl.BlockSpec((nb, DN_HEADS, DN_DK, DN_DV), lambda b, c: (b, 0, 0, 0)),
                  pl.BlockSpec((CONV_W, DN_CONV_DIM), lambda b, c: (0, 0)),
                  pl.BlockSpec((8, SMALL_W), lambda b, c: (0, 0)),
                  pl.BlockSpec((1, DN_DV), lambda b, c: (0, 0))],
        out_specs=(pl.BlockSpec((nb, chunk, DN_V), lambda b, c: (b, c, 0)),
                   pl.BlockSpec((nb, DN_HEADS, DN_DK, DN_DV), lambda b, c: (b, 0, 0, 0))),
        scratch_shapes=[pltpu.VMEM((nb, chunk + CONV_PAD, DN_CONV_DIM), F32),
                        pltpu.VMEM((DN_DK, nb * DN_HEADS * DN_DV), F32)],
        compiler_params=_params(("parallel", "arbitrary")),
        name="dn_mixer",
    )(proj, proj, proj, conv0, s0, cw, par, nw)


def _ssd_kernel(xbc_ref, z_ref, small_ref, conv0_ref, s0_ref, cw_ref, cb_ref, par_ref, nw_ref,
                y_ref, s_out_ref, xext, state, ybuf, *, chunk, seq_len):
    c_idx = pl.program_id(1)

    @pl.when(c_idx == 0)
    def _():
        xext[0:CONV_PAD, :] = conv0_ref[0]
        state[...] = s0_ref[0]

    xbc = _silu(_causal_conv(xbc_ref, xext, cw_ref, chunk) + cb_ref[...])
    small = small_ref[...]
    par = par_ref[...]
    dt_all = _softplus(small + par[1:2, :])
    valid = _valid_rows(chunk, seq_len, small.shape)
    if valid is not None:
        dt_all = jnp.where(valid, dt_all, 0.0)
    ac_all = _cumsum_rows(dt_all * -jnp.exp(par[0:1, :]))

    ri, ci, eye, causal, strict = _tri_masks(chunk)
    gw = SSM_GROUPS * SSM_N
    heads_per_group = SSM_HEADS // SSM_GROUPS
    for g in range(SSM_GROUPS):
        bm = xbc[:, SSM_INNER + g * SSM_N:SSM_INNER + (g + 1) * SSM_N]
        cm = xbc[:, SSM_INNER + gw + g * SSM_N:SSM_INNER + gw + (g + 1) * SSM_N]
        cb = _dot_nt(cm, bm)
        for hh in range(heads_per_group):
            h = g * heads_per_group + hh
            lane = 2 * DN_HEADS + h
            ac = ac_all[:, lane:lane + 1]
            dt = dt_all[:, lane:lane + 1]
            decay = jnp.where(causal, jnp.exp(ac - _col_to_row(ac, eye)), 0.0)
            xh = xbc[:, h * SSM_P:(h + 1) * SSM_P]
            xdt = xh * dt
            s = state[h]
            y = _dot(cb * decay, xdt) + _dot_nt(cm * jnp.exp(ac), s)
            ac_last = ac[chunk - 1:chunk, :]
            state[h] = s * jnp.exp(ac_last) + _dot_tn(xdt, bm * jnp.exp(ac_last - ac))
            ybuf[:, h * SSM_P:(h + 1) * SSM_P] = y + par[2:3, lane:lane + 1] * xh

    y = ybuf[...] * _silu(z_ref[...])
    nw = nw_ref[...]
    gn = SSM_INNER // SSM_GROUPS
    for g in range(SSM_GROUPS):
        yg = y[:, g * gn:(g + 1) * gn]
        yg = yg * lax.rsqrt(jnp.mean(yg * yg, axis=-1, keepdims=True) + EPS)
        y_ref[:, g * gn:(g + 1) * gn] = yg * nw[:, g * gn:(g + 1) * gn]

    @pl.when(c_idx == pl.num_programs(1) - 1)
    def _():
        s_out_ref[0] = state[...]


def _ssd_mixer(proj, conv0, s0, cw, cb, par, nw, *, batch, seq_len, chunk):
    nc = -(-seq_len // chunk)
    rows = batch * nc * chunk
    row = lambda b, c: b * nc + c
    return pl.pallas_call(
        functools.partial(_ssd_kernel, chunk=chunk, seq_len=seq_len),
        out_shape=(jax.ShapeDtypeStruct((rows, SSM_INNER), F32),
                   jax.ShapeDtypeStruct((batch, SSM_HEADS, SSM_P, SSM_N), F32)),
        grid=(batch, nc),
        in_specs=[pl.BlockSpec((chunk, SSM_CONV_DIM), lambda b, c: (row(b, c), EV_XBC // SSM_CONV_DIM)),
                  pl.BlockSpec((chunk, SSM_INNER), lambda b, c: (row(b, c), EV_Z // SSM_INNER)),
                  pl.BlockSpec((chunk, SMALL_W), lambda b, c: (row(b, c), EV_SMALL // SMALL_W)),
                  pl.BlockSpec((1, CONV_PAD, SSM_CONV_DIM), lambda b, c: (b, 0, 0)),
                  pl.BlockSpec((1, SSM_HEADS, SSM_P, SSM_N), lambda b, c: (b, 0, 0, 0)),
                  pl.BlockSpec((CONV_W, SSM_CONV_DIM), lambda b, c: (0, 0)),
                  pl.BlockSpec((1, SSM_CONV_DIM), lambda b, c: (0, 0)),
                  pl.BlockSpec((8, SMALL_W), lambda b, c: (0, 0)),
                  pl.BlockSpec((1, SSM_INNER), lambda b, c: (0, 0))],
        out_specs=(pl.BlockSpec((chunk, SSM_INNER), lambda b, c: (row(b, c), 0)),
                   pl.BlockSpec((1, SSM_HEADS, SSM_P, SSM_N), lambda b, c: (b, 0, 0, 0))),
        scratch_shapes=[pltpu.VMEM((chunk + CONV_PAD, SSM_CONV_DIM), F32),
                        pltpu.VMEM((SSM_HEADS, SSM_P, SSM_N), F32),
                        pltpu.VMEM((chunk, SSM_INNER), F32)],
        compiler_params=_params(("parallel", "arbitrary")),
        name="ssd_mixer",
    )(proj, proj, proj, conv0, s0, cw, cb, par, nw)


def _gla_kernel(qkv_ref, og_ref, glr_ref, s0_ref, wg_ref, bg_ref, nw_ref,
                o_ref, s_out_ref, state, *, chunk, seq_len):
    c_idx = pl.program_id(1)

    @pl.when(c_idx == 0)
    def _():
        state[...] = s0_ref[0]

    x = bg_ref[...] + jnp.dot(glr_ref[...].astype(BF16), wg_ref[...], preferred_element_type=F32)
    gk = (jnp.minimum(x, 0.0) - jnp.log(1.0 + jnp.exp(-jnp.abs(x)))) / GLA_NORMALIZER
    valid = _valid_rows(chunk, seq_len, gk.shape)
    if valid is not None:
        gk = jnp.where(valid, gk, 0.0)
    b_all = _cumsum_rows(gk)

    ri, ci, eye_c, causal, strict = _tri_masks(chunk)
    eye_k = _tri_masks(GLA_DK)[2]
    nw = nw_ref[...]
    for h in range(GLA_HEADS):
        b = b_all[:, h * GLA_DK:(h + 1) * GLA_DK]
        q = qkv_ref[:, h * GLA_DK:(h + 1) * GLA_DK] * (GLA_DK ** -0.5)
        k = qkv_ref[:, GLA_K + h * GLA_DK:GLA_K + (h + 1) * GLA_DK]
        v = qkv_ref[:, 2 * GLA_K + h * GLA_DV:2 * GLA_K + (h + 1) * GLA_DV]
        qe = q * jnp.exp(b)
        att = jnp.where(causal, _dot_nt(qe, k * jnp.exp(-b)), 0.0)
        s = state[h]
        o = _dot(qe, s) + _dot(att, v)
        b_last = b[chunk - 1:chunk, :]
        state[h] = s * _row_to_col(jnp.exp(b_last), eye_k) + _dot_tn(k * jnp.exp(b_last - b), v)
        o = o * lax.rsqrt(jnp.mean(o * o, axis=-1, keepdims=True) + EPS) * nw
        o_ref[:, h * GLA_DV:(h + 1) * GLA_DV] = o * _silu(og_ref[:, h * GLA_DV:(h + 1) * GLA_DV])

    @pl.when(c_idx == pl.num_programs(1) - 1)
    def _():
        s_out_ref[0] = state[...]


def _gla_mixer(proj, s0, wg, bg, nw, *, batch, seq_len, chunk):
    nc = -(-seq_len // chunk)
    rows = batch * nc * chunk
    row = lambda b, c: b * nc + c
    qkv_w = 2 * GLA_K + GLA_V
    return pl.pallas_call(
        functools.partial(_gla_kernel, chunk=chunk, seq_len=seq_len),
        out_shape=(jax.ShapeDtypeStruct((rows, GLA_V), F32),
                   jax.ShapeDtypeStruct((batch, GLA_HEADS, GLA_DK, GLA_DV), F32)),
        grid=(batch, nc),
        in_specs=[pl.BlockSpec((chunk, qkv_w), lambda b, c: (row(b, c), OD_QKV // qkv_w)),
                  pl.BlockSpec((chunk, GLA_V), lambda b, c: (row(b, c), OD_OG // GLA_V)),
                  pl.BlockSpec((chunk, SMALL_W), lambda b, c: (row(b, c), OD_GLR // SMALL_W)),
                  pl.BlockSpec((1, GLA_HEADS, GLA_DK, GLA_DV), lambda b, c: (b, 0, 0, 0)),
                  pl.BlockSpec((SMALL_W, GLA_K), lambda b, c: (0, 0)),
                  pl.BlockSpec((1, GLA_K), lambda b, c: (0, 0)),
                  pl.BlockSpec((1, GLA_DV), lambda b, c: (0, 0))],
        out_specs=(pl.BlockSpec((chunk, GLA_V), lambda b, c: (row(b, c), 0)),
                   pl.BlockSpec((1, GLA_HEADS, GLA_DK, GLA_DV), lambda b, c: (b, 0, 0, 0))),
        scratch_shapes=[pltpu.VMEM((GLA_HEADS, GLA_DK, GLA_DV), F32)],
        compiler_params=_params(("parallel", "arbitrary")),
        name="gla_mixer",
    )(proj, proj, proj, s0, wg, bg, nw)


def _take_top(x, key, count):
    nr, nt = x.shape
    big = jnp.int32(2 ** 30)
    slot = lax.broadcasted_iota(jnp.int32, (count, nt), 0)

    def body(r, carry):
        x, rank, vals = carry
        m = jnp.max(x, axis=0, keepdims=True)
        first = jnp.min(jnp.where(x == m, key, big), axis=0, keepdims=True)
        hit = key == first
        return (jnp.where(hit, -jnp.inf, x), jnp.where(hit, r, rank), jnp.where(slot == r, m, vals))

    init = (x, jnp.full((nr, nt), count, jnp.int32), jnp.zeros((count, nt), F32))
    _, rank, vals = lax.fori_loop(0, count, body, init)
    return vals, rank


def _pair_candidates(v1, v2):
    k = PEER_TOPK
    nt = v1.shape[1]
    j8 = lax.broadcasted_iota(jnp.int32, (8, nt), 0)
    j16 = lax.broadcasted_iota(jnp.int32, (k, nt), 0)
    sums, pos = [v1[0:1, :] + v2], [j16]
    for i in range(1, 8):
        sums.append(jnp.where(j8 < k // (i + 1), v1[i:i + 1, :] + v2[0:8, :], -jnp.inf))
        pos.append(i * k + j8)
    sums.append(v1[8:k, :] + v2[0:1, :])
    pos.append((8 + j8) * k)
    return jnp.concatenate(sums, axis=0), jnp.concatenate(pos, axis=0)


def _take_top_values(xs, count):
    nt = xs[0].shape[1]
    slot = lax.broadcasted_iota(jnp.int32, (count, nt), 0)

    def body(r, carry):
        out = []
        for x, vals in carry:
            m = jnp.max(x, axis=0, keepdims=True)
            out.append((jnp.where(x == m, -jnp.inf, x), jnp.where(slot == r, m, vals)))
        return tuple(out)

    init = tuple((x, jnp.zeros((count, nt), F32)) for x in xs)
    return [vals for _, vals in lax.fori_loop(0, count, body, init)]


def _count_rows(mask):
    return jnp.sum(jnp.where(mask, 1.0, 0.0), axis=0, keepdims=True)


def _per_first_key(sel):
    k = PEER_TOPK
    cnt = jnp.where(sel, 1.0, 0.0)
    out = [jnp.sum(cnt[0:k, :], axis=0, keepdims=True)]
    out += [jnp.sum(cnt[k + 8 * (i - 1):k + 8 * i, :], axis=0, keepdims=True) for i in range(1, 8)]
    out += [cnt[k + 56 + i:k + 57 + i, :] for i in range(k - 8)]
    return out


def _route_tables_exact(s1, s2):
    k = PEER_TOPK
    key = lax.broadcasted_iota(jnp.int32, s1.shape, 0)
    v1, rank1 = _take_top(s1, key, k)
    v2, rank2 = _take_top(s2, key, k)
    cand, pos = _pair_candidates(v1, v2)
    _, crank = _take_top(cand, pos, k)
    sel = crank < k
    z = jnp.sum(jnp.where(sel, jnp.exp(cand - (v1[0:1, :] + v2[0:1, :])), 0.0), axis=0, keepdims=True)
    n = jnp.zeros(s1.shape, F32)
    for i, n_i in enumerate(_per_first_key(sel)):
        n = jnp.where(rank1 == i, n_i, n)
    e1 = jnp.where(rank1 < k, jnp.exp(s1 - v1[0:1, :]) / z, 0.0)
    e2 = jnp.where(rank2 < k, jnp.exp(s2 - v2[0:1, :]), 0.0)
    return n, e1, rank2.astype(F32), e2


def _route_tables_distinct(s1, s2):
    k = PEER_TOPK
    v1, v2 = _take_top_values([s1, s2], k)
    in1 = s1 >= v1[k - 1:k, :]
    in2 = s2 >= v2[k - 1:k, :]
    cand, _ = _pair_candidates(v1, v2)
    cv, = _take_top_values([cand], k)
    sel = cand >= cv[k - 1:k, :]
    ok = (_count_rows(in1) == k) & (_count_rows(in2) == k) & (_count_rows(sel) == k)
    z = jnp.sum(jnp.where(sel, jnp.exp(cand - (v1[0:1, :] + v2[0:1, :])), 0.0), axis=0, keepdims=True)
    n = jnp.zeros(s1.shape, F32)
    rank2 = jnp.full(s2.shape, float(k), F32)
    for i, n_i in enumerate(_per_first_key(sel)):
        n = jnp.where(s1 == v1[i:i + 1, :], n_i, n)
        rank2 = jnp.where(s2 == v2[i:i + 1, :], float(i), rank2)
    e1 = jnp.where(in1, jnp.exp(s1 - v1[0:1, :]) / z, 0.0)
    e2 = jnp.where(in2, jnp.exp(s2 - v2[0:1, :]), 0.0)
    return (n, e1, rank2, e2), ok


def _peer_route_kernel(q_ref, k1_ref, k2_ref, n_ref, e1_ref, r2_ref, e2_ref, s1_ref, s2_ref):
    q = q_ref[...]
    s1_ref[...] = _dot_nt(k1_ref[0], q[:, :PEER_HALF])
    s2_ref[...] = _dot_nt(k2_ref[0], q[:, PEER_HALF:])
    tt = q.shape[0]
    lw = min(LANES, tt)
    for lc in range(tt // lw):
        sl = slice(lc * lw, (lc + 1) * lw)

        def store(tables):
            n, e1, rank2, e2 = tables
            n_ref[0, :, sl] = n
            e1_ref[0, :, sl] = e1
            r2_ref[:, sl] = rank2.astype(BF16)
            e2_ref[:, sl] = e2.astype(BF16)

        tables, ok = _route_tables_distinct(s1_ref[:, sl], s2_ref[:, sl])
        store(tables)

        @pl.when(jnp.min(jnp.where(ok, 1, 0)) == 0)
        def _():
            store(_route_tables_exact(s1_ref[:, sl], s2_ref[:, sl]))


def _peer_route(q, keys1, keys2, tt):
    t = q.shape[0]
    out_a = jax.ShapeDtypeStruct((PEER_HEADS, PEER_NKEYS, t), F32)
    out_b = jax.ShapeDtypeStruct((PEER_HEADS * PEER_NKEYS, t), BF16)
    spec_a = pl.BlockSpec((1, PEER_NKEYS, tt), lambda i, h: (h, 0, i))
    spec_b = pl.BlockSpec((PEER_NKEYS, tt), lambda i, h: (h, i))
    key_spec = pl.BlockSpec((1, PEER_NKEYS, PEER_HALF), lambda i, h: (h, 0, 0))
    return pl.pallas_call(
        _peer_route_kernel,
        out_shape=(out_a, out_a, out_b, out_b),
        grid=(t // tt, PEER_HEADS),
        in_specs=[pl.BlockSpec((tt, PEER_DQ), lambda i, h: (i, h)), key_spec, key_spec],
        out_specs=(spec_a, spec_a, spec_b, spec_b),
        scratch_shapes=[pltpu.VMEM((PEER_NKEYS, tt), F32), pltpu.VMEM((PEER_NKEYS, tt), F32)],
        compiler_params=_params(("parallel", "arbitrary")),
        name="peer_route",
    )(q, keys1, keys2)


def _peer_dense_kernel(x_ref, lnw_ref, n_ref, e1_ref, r2_in_ref, e2_in_ref, u_ref, v_ref, o_ref,
                       h_ref, pre_ref, act_ref, acc_ref, r2_ref, e2_ref, *, rows_per_step):
    j = pl.program_id(1)

    @pl.when(j == 0)
    def _():
        x = x_ref[...]
        y = x * lax.rsqrt(jnp.mean(x * x, axis=-1, keepdims=True) + EPS)
        h_ref[...] = (y * lnw_ref[...]).astype(BF16)
        acc_ref[...] = jnp.zeros_like(acc_ref)
        r2_ref[...] = r2_in_ref[...]
        e2_ref[...] = e2_in_ref[...]

    pre_ref[...] = lax.dot_general(u_ref[...], h_ref[...], _NT, preferred_element_type=F32)
    tt = h_ref.shape[0]
    lw = min(LANES, tt)
    reps = PEER_NKEYS // BF16_ROWS
    zero = jnp.zeros((PEER_NKEYS, lw), BF16)
    for lc in range(tt // lw):
        sl = slice(lc * lw, (lc + 1) * lw)
        for aa in range(rows_per_step):
            rows = slice(aa * PEER_NKEYS, (aa + 1) * PEER_NKEYS)
            gate = zero
            for h in range(PEER_HEADS):
                n_row = jnp.broadcast_to(n_ref[h, aa:aa + 1, sl], (BF16_ROWS, lw)).astype(BF16)
                e1_row = jnp.broadcast_to(e1_ref[h, aa:aa + 1, sl], (BF16_ROWS, lw)).astype(BF16)
                n_tile = jnp.concatenate([n_row] * reps, axis=0)
                e1_tile = jnp.concatenate([e1_row] * reps, axis=0)
                keys = slice(h * PEER_NKEYS, (h + 1) * PEER_NKEYS)
                gate = gate + jnp.where(r2_ref[keys, sl] < n_tile, e2_ref[keys, sl], zero) * e1_tile
            act_ref[rows, sl] = _gelu_tanh(pre_ref[rows, sl]).astype(BF16) * gate
    acc_ref[...] += lax.dot_general(act_ref[...], v_ref[...], _TN, preferred_element_type=F32)

    @pl.when(j == pl.num_programs(1) - 1)
    def _():
        o_ref[...] = x_ref[...] + acc_ref[...]


def _peer_dense(x, lnw, route, u, v, tt, rows_per_step):
    t, d = x.shape
    eb = rows_per_step * PEER_NKEYS
    spec_a = pl.BlockSpec((PEER_HEADS, rows_per_step, tt), lambda i, j: (0, j, i))
    spec_b = pl.BlockSpec((PEER_HEADS * PEER_NKEYS, tt), lambda i, j: (0, i))
    return pl.pallas_call(
        functools.partial(_peer_dense_kernel, rows_per_step=rows_per_step),
        out_shape=jax.ShapeDtypeStruct((t, d), F32),
        grid=(t // tt, PEER_N // eb),
        in_specs=[pl.BlockSpec((tt, d), lambda i, j: (i, 0)),
                  pl.BlockSpec((1, d), lambda i, j: (0, 0)),
                  spec_a, spec_a, spec_b, spec_b,
                  pl.BlockSpec((eb, d), lambda i, j: (j, 0)),
                  pl.BlockSpec((eb, d), lambda i, j: (j, 0))],
        out_specs=pl.BlockSpec((tt, d), lambda i, j: (i, 0)),
        scratch_shapes=[pltpu.VMEM((tt, d), BF16), pltpu.VMEM((eb, tt), F32),
                        pltpu.VMEM((eb, tt), BF16), pltpu.VMEM((tt, d), F32),
                        pltpu.VMEM((PEER_HEADS * PEER_NKEYS, tt), BF16),
                        pltpu.VMEM((PEER_HEADS * PEER_NKEYS, tt), BF16)],
        compiler_params=_params(("parallel", "arbitrary")),
        name="peer_dense",
    )(x, lnw.reshape(1, d), *route, u, v)


def _peer_ffn(x, lnw, wq, keys1, keys2, u, v):
    tt = min(512, x.shape[0])
    q = _norm_matmul(x, lnw, wq)
    route = _peer_route(q, keys1, keys2, tt)
    return _peer_dense(x, lnw, route, u, v, tt, rows_per_step=16)


def _pad_lanes(vec, offset):
    return jnp.zeros((SMALL_W,), F32).at[offset:offset + vec.shape[0]].set(vec.astype(F32))


def _conv_history(buf):
    return jnp.pad(buf, ((0, 0), (CONV_PAD - (CONV_W - 1), 0), (0, 0)))


def kernel(x_prompt, x_sample, state_dn_conv, state_dn, state_ssm_conv, state_ssm, state_gla, ln_mix, ln_ffn, ln_final, w_in_even, w_out_even, dn_conv_w, dn_a_log, dn_dt_bias, dn_norm_w, ssm_conv_w, ssm_conv_b, ssm_a_log, ssm_dt_bias, ssm_d, ssm_norm_w, w_in_odd, gla_w_gate2, gla_b_gate, gla_norm_w, w_out_odd, peer_w_q, peer_keys1, peer_keys2, peer_u, peer_v):
    we = w_in_even[0]
    o = 0
    cuts = {}
    for name, size in (("qkv", DN_CONV_DIM), ("a", DN_HEADS), ("b", DN_HEADS), ("gate", DN_V),
                       ("z", SSM_INNER), ("xbc", SSM_CONV_DIM), ("dt", SSM_HEADS)):
        cuts[name] = we[:, o:o + size]
        o += size
    small_w = jnp.concatenate([cuts["a"], cuts["b"], cuts["dt"]], axis=1)
    small_w = jnp.pad(small_w, ((0, 0), (0, SMALL_W - small_w.shape[1])))
    w_even = jnp.concatenate([cuts["qkv"], cuts["gate"], cuts["xbc"], cuts["z"], small_w], axis=1).astype(BF16)
    wo = w_in_odd[0]
    qkv_w = 2 * GLA_K + GLA_V
    glr_w = jnp.pad(wo[:, qkv_w:qkv_w + GLA_LR], ((0, 0), (0, SMALL_W - GLA_LR)))
    w_odd = jnp.concatenate([wo[:, :qkv_w], wo[:, qkv_w + GLA_LR:], glr_w], axis=1).astype(BF16)
    w_out_dn = w_out_even[0, :DN_V].astype(BF16)
    w_out_ssm = w_out_even[0, DN_V:].astype(BF16)
    w_out_gla = w_out_odd[0].astype(BF16)
    wg = jnp.pad(gla_w_gate2[0], ((0, SMALL_W - GLA_LR), (0, 0))).astype(BF16)
    wq = peer_w_q.astype(BF16)
    k1 = peer_keys1.astype(BF16)
    k2 = peer_keys2.astype(BF16)
    pu = peer_u.astype(BF16)
    pv = peer_v.astype(BF16)

    zero8 = jnp.zeros((8 - 2, SMALL_W), F32)
    dn_par = jnp.concatenate([_pad_lanes(dn_a_log[0], 0)[None], _pad_lanes(dn_dt_bias[0], 0)[None], zero8])
    lane0 = 2 * DN_HEADS
    ssm_par = jnp.concatenate([_pad_lanes(ssm_a_log[0], lane0)[None], _pad_lanes(ssm_dt_bias[0], lane0)[None],
                               _pad_lanes(ssm_d[0], lane0)[None], zero8[1:]])

    def trunk(x, batch, seq_len, conv_dn0, s_dn0, conv_ssm0, s_ssm0, s_gla0):
        chunk = PROMPT_CHUNK if seq_len >= PROMPT_CHUNK else STEP_CHUNK
        nc = -(-seq_len // chunk)
        padded = nc * chunk != seq_len
        dn_nb = max(1, min(batch, 256 // (DN_HEADS * chunk)))
        while batch % dn_nb:
            dn_nb -= 1

        def to_seq(a):
            if not padded:
                return a
            a = a.reshape(batch, seq_len, a.shape[-1])
            return jnp.pad(a, ((0, 0), (0, nc * chunk - seq_len), (0, 0))).reshape(batch * nc * chunk, -1)

        def from_seq(a):
            if not padded:
                return a
            return a.reshape(batch, nc * chunk, a.shape[-1])[:, :seq_len].reshape(batch * seq_len, -1)

        def last_rows(buf, proj, col, width):
            tail = proj.reshape(batch, seq_len, proj.shape[-1])[:, -min(seq_len, CONV_W - 1):, col:col + width]
            return jnp.concatenate([buf, tail], axis=1)[:, -(CONV_W - 1):]

        seq = dict(batch=batch, seq_len=seq_len, chunk=chunk)

        proj = _norm_matmul(x, ln_mix[0], w_even)
        proj_seq = to_seq(proj)
        o_dn, s_dn = _dn_mixer(proj_seq.reshape(batch, nc * chunk, EV_WIDTH), _conv_history(conv_dn0[0]),
                               s_dn0[0], dn_conv_w[0], dn_par, dn_norm_w[0].reshape(1, DN_DV),
                               nb=dn_nb, **seq)
        o_dn = o_dn.reshape(batch * nc * chunk, DN_V)
        y_ssm, s_ssm = _ssd_mixer(proj_seq, _conv_history(conv_ssm0[0]), s_ssm0[0], ssm_conv_w[0],
                                  ssm_conv_b[0].reshape(1, SSM_CONV_DIM), ssm_par,
                                  ssm_norm_w[0].reshape(1, SSM_INNER), **seq)
        conv_dn = last_rows(conv_dn0[0], proj, EV_QKV, DN_CONV_DIM)
        conv_ssm = last_rows(conv_ssm0[0], proj, EV_XBC, SSM_CONV_DIM)
        x = _proj_residual(x, [from_seq(o_dn), from_seq(y_ssm)], [w_out_dn, w_out_ssm])
        x = _peer_ffn(x, ln_ffn[0], wq[0], k1[0], k2[0], pu[0], pv[0])

        proj = _norm_matmul(x, ln_mix[1], w_odd)
        o_gla, s_gla = _gla_mixer(to_seq(proj), s_gla0[0], wg, gla_b_gate[0].reshape(1, GLA_K),
                                  gla_norm_w[0].reshape(1, GLA_DV), **seq)
        x = _proj_residual(x, [from_seq(o_gla)], [w_out_gla])
        x = _peer_ffn(x, ln_ffn[1], wq[1], k1[1], k2[1], pu[1], pv[1])

        y = _final_norm(x, ln_final).reshape(batch, seq_len, D_MODEL)
        return y, conv_dn[None], s_dn[None], conv_ssm[None], s_ssm[None], s_gla[None]

    bp, lp = x_prompt.shape[:2]
    bs, ls = x_sample.shape[:2]

    def zeros_like_state(s):
        return jnp.zeros((s.shape[0], bp) + s.shape[2:], s.dtype)

    prompt = trunk(x_prompt.reshape(bp * lp, D_MODEL), bp, lp, zeros_like_state(state_dn_conv),
                   zeros_like_state(state_dn), zeros_like_state(state_ssm_conv),
                   zeros_like_state(state_ssm), zeros_like_state(state_gla))
    sample = trunk(x_sample.reshape(bs * ls, D_MODEL), bs, ls, state_dn_conv, state_dn,
                   state_ssm_conv, state_ssm, state_gla)
    return (prompt[0], sample[0]) + prompt[1:] + sample[1:]
```

```python
import functools

import jax
import jax.numpy as jnp
from jax import lax
from jax.experimental import pallas as pl
from jax.experimental.pallas import tpu as pltpu

F32 = jnp.float32
BF16 = jnp.bfloat16
EPS = 1e-6

D_MODEL = 1024
CONV_W = 4
CONV_PAD = 8
PROMPT_CHUNK = 64
STEP_CHUNK = 8

DN_HEADS, DN_DK, DN_DV = 4, 128, 128
DN_QK = DN_HEADS * DN_DK
DN_V = DN_HEADS * DN_DV
DN_CONV_DIM = 2 * DN_QK + DN_V
SSM_HEADS, SSM_P, SSM_GROUPS, SSM_N = 8, 64, 2, 128
SSM_INNER = SSM_HEADS * SSM_P
SSM_CONV_DIM = SSM_INNER + 2 * SSM_GROUPS * SSM_N
GLA_HEADS, GLA_DK, GLA_DV, GLA_LR = 4, 128, 256, 16
GLA_K = GLA_HEADS * GLA_DK
GLA_V = GLA_HEADS * GLA_DV
GLA_NORMALIZER = 16.0
PEER_HEADS, PEER_DQ, PEER_NKEYS, PEER_TOPK = 8, 256, 128, 16
PEER_HALF = PEER_DQ // 2
PEER_N = PEER_NKEYS * PEER_NKEYS
LANES = 128
BF16_ROWS = 16
SMALL_W = LANES

EV_QKV, EV_GATE, EV_XBC, EV_Z, EV_SMALL = 0, 1536, 2048, 3072, 3584
EV_WIDTH = EV_SMALL + SMALL_W
OD_QKV, OD_OG, OD_GLR = 0, 2048, 3072
OD_WIDTH = OD_GLR + SMALL_W

VMEM_LIMIT = 48 * 1024 * 1024

_NT = (((1,), (1,)), ((), ()))
_TN = (((0,), (0,)), ((), ()))


def _dot(a, b):
    return jnp.dot(a.astype(BF16), b.astype(BF16), preferred_element_type=F32)


def _dot_nt(a, b):
    return lax.dot_general(a.astype(BF16), b.astype(BF16), _NT, preferred_element_type=F32)


def _dot_tn(a, b):
    return lax.dot_general(a.astype(BF16), b.astype(BF16), _TN, preferred_element_type=F32)


def _params(sem):
    return pltpu.CompilerParams(dimension_semantics=sem, vmem_limit_bytes=VMEM_LIMIT)


def _silu(x):
    return x * jax.nn.sigmoid(x)


def _gelu_tanh(x):
    c0 = 0.7978845608028654
    hx = 0.5 * x
    return hx + hx * jnp.tanh(x * (c0 + (c0 * 0.044715) * (x * x)))


def _softplus(x):
    return jnp.maximum(x, 0.0) + jnp.log(1.0 + jnp.exp(-jnp.abs(x)))


def _cumsum_rows(x):
    n = x.shape[0]
    row = lax.broadcasted_iota(jnp.int32, x.shape, 0)
    s = 1
    while s < n:
        x = x + jnp.where(row >= s, pltpu.roll(x, s, axis=0), 0.0)
        s *= 2
    return x


def _col_to_row(col, eye):
    return jnp.sum(jnp.where(eye, col, 0.0), axis=0, keepdims=True)


def _row_to_col(row, eye):
    return jnp.sum(jnp.where(eye, row, 0.0), axis=1, keepdims=True)


def _tri_masks(n):
    ri = lax.broadcasted_iota(jnp.int32, (n, n), 0)
    ci = lax.broadcasted_iota(jnp.int32, (n, n), 1)
    return ri, ci, ri == ci, ri >= ci, ri > ci


def _valid_rows(chunk, seq_len, shape):
    if seq_len % chunk == 0:
        return None
    pos = pl.program_id(1) * chunk + lax.broadcasted_iota(jnp.int32, shape, 0)
    return pos < seq_len


def _causal_conv(x_ref, xext, cw_ref, chunk):
    x = x_ref[...]
    xext[CONV_PAD:CONV_PAD + chunk, :] = x
    cw = cw_ref[...]
    pre = cw[CONV_W - 1:CONV_W, :] * x
    for j in range(1, CONV_W):
        pre = pre + cw[CONV_W - 1 - j:CONV_W - j, :] * xext[CONV_PAD - j:CONV_PAD - j + chunk, :]
    xext[0:CONV_PAD, :] = xext[chunk:chunk + CONV_PAD, :]
    return pre


def _norm_matmul_kernel(x_ref, lnw_ref, w_ref, o_ref):
    x = x_ref[...]
    y = x * lax.rsqrt(jnp.mean(x * x, axis=-1, keepdims=True) + EPS)
    h = (y * lnw_ref[...]).astype(BF16)
    o_ref[...] = jnp.dot(h, w_ref[...], preferred_element_type=F32)


def _norm_matmul(x, lnw, w):
    t, d = x.shape
    n = w.shape[1]
    tm = min(512, t)
    return pl.pallas_call(
        _norm_matmul_kernel,
        out_shape=jax.ShapeDtypeStruct((t, n), F32),
        grid=(t // tm,),
        in_specs=[pl.BlockSpec((tm, d), lambda i: (i, 0)),
                  pl.BlockSpec((1, d), lambda i: (0, 0)),
                  pl.BlockSpec((d, n), lambda i: (0, 0))],
        out_specs=pl.BlockSpec((tm, n), lambda i: (i, 0)),
        compiler_params=_params(("parallel",)),
        name="norm_matmul",
    )(x, lnw.reshape(1, d), w)


def _proj_residual_kernel(*refs, n_in):
    x_ref, o_ref = refs[0], refs[-1]
    acc = x_ref[...]
    for a_ref, w_ref in zip(refs[1:1 + n_in], refs[1 + n_in:1 + 2 * n_in]):
        acc = acc + jnp.dot(a_ref[...].astype(BF16), w_ref[...], preferred_element_type=F32)
    o_ref[...] = acc


def _proj_residual(x, acts, weights):
    t, d = x.shape
    tm = min(512, t)
    n_in = len(acts)
    in_specs = [pl.BlockSpec((tm, d), lambda i: (i, 0))]
    in_specs += [pl.BlockSpec((tm, a.shape[1]), lambda i: (i, 0)) for a in acts]
    in_specs += [pl.BlockSpec(w.shape, lambda i: (0, 0)) for w in weights]
    return pl.pallas_call(
        functools.partial(_proj_residual_kernel, n_in=n_in),
        out_shape=jax.ShapeDtypeStruct((t, d), F32),
        grid=(t // tm,),
        in_specs=in_specs,
        out_specs=pl.BlockSpec((tm, d), lambda i: (i, 0)),
        compiler_params=_params(("parallel",)),
        name="proj_residual",
    )(x, *acts, *weights)


def _final_norm_kernel(x_ref, w_ref, o_ref):
    x = x_ref[...]
    o_ref[...] = x * lax.rsqrt(jnp.mean(x * x, axis=-1, keepdims=True) + EPS) * w_ref[...]


def _final_norm(x, w):
    t, d = x.shape
    tm = min(512, t)
    return pl.pallas_call(
        _final_norm_kernel,
        out_shape=jax.ShapeDtypeStruct((t, d), F32),
        grid=(t // tm,),
        in_specs=[pl.BlockSpec((tm, d), lambda i: (i, 0)), pl.BlockSpec((1, d), lambda i: (0, 0))],
        out_specs=pl.BlockSpec((tm, d), lambda i: (i, 0)),
        compiler_params=_params(("parallel",)),
        name="final_norm",
    )(x, w.reshape(1, d))


def _unit_lower_inverse(m, ri, ci, eye, block):
    t = jnp.where(eye, 1.0, 0.0)
    level = 0
    while (1 << level) < block:
        pair = ((ri >> (level + 1)) == (ci >> (level + 1))) & ((ri >> level) != (ci >> level))
        c = jnp.where(pair, m, 0.0)
        t = t - c if level == 0 else t - _dot(_dot(t, c), t)
        level += 1
    return t


def _dn_kernel(qkv_ref, gate_ref, small_ref, conv0_ref, s0_ref, cw_ref, par_ref, nw_ref,
               o_ref, s_out_ref, xext, state, *, chunk, seq_len, nb):
    c_idx = pl.program_id(1)
    groups = nb * DN_HEADS
    log2c = chunk.bit_length() - 1
    group_lanes = lambda g: slice(g * DN_DV, (g + 1) * DN_DV)

    @pl.when(c_idx == 0)
    def _():
        for s in range(nb):
            xext[s, 0:CONV_PAD, :] = conv0_ref[s]
            for h in range(DN_HEADS):
                state[:, group_lanes(s * DN_HEADS + h)] = s0_ref[s, h]

    par = par_ref[...]
    qs, ks, vs, gcs, betas, gates = [], [], [], [], [], []
    for s in range(nb):
        act = _silu(_causal_conv(qkv_ref.at[s], xext.at[s], cw_ref, chunk))
        small = small_ref[s]
        g_all = -jnp.exp(par[0:1, :]) * _softplus(small + par[1:2, :])
        beta_all = jax.nn.sigmoid(small)
        valid = _valid_rows(chunk, seq_len, small.shape)
        if valid is not None:
            g_all = jnp.where(valid, g_all, 0.0)
            beta_all = jnp.where(valid, beta_all, 0.0)
        gc_all = _cumsum_rows(g_all)
        for h in range(DN_HEADS):
            xq = act[:, h * DN_DK:(h + 1) * DN_DK]
            xk = act[:, DN_QK + h * DN_DK:DN_QK + (h + 1) * DN_DK]
            qs.append(xq * lax.rsqrt(jnp.sum(xq * xq, axis=-1, keepdims=True) + EPS) * (DN_DK ** -0.5))
            ks.append(xk * lax.rsqrt(jnp.sum(xk * xk, axis=-1, keepdims=True) + EPS))
            vs.append(act[:, 2 * DN_QK + h * DN_DV:2 * DN_QK + (h + 1) * DN_DV])
            gcs.append(gc_all[:, h:h + 1])
            betas.append(beta_all[:, DN_HEADS + h:DN_HEADS + h + 1])
            gates.append(gate_ref[s, :, h * DN_DV:(h + 1) * DN_DV])
    q, k, v = (jnp.concatenate(t, axis=0) for t in (qs, ks, vs))
    gc, beta = (jnp.concatenate(t, axis=0) for t in (gcs, betas))
    gc_last = jnp.concatenate([jnp.broadcast_to(t[chunk - 1:chunk, :], (chunk, 1)) for t in gcs], axis=0)
    last_decay = jnp.concatenate(
        [jnp.broadcast_to(jnp.exp(t[chunk - 1:chunk, :]), (1, DN_DV)) for t in gcs], axis=1)

    rows = groups * chunk
    ri, ci, eye, lower, strict_lower = _tri_masks(rows)
    same = (ri >> log2c) == (ci >> log2c)
    causal = same & lower
    strict = same & strict_lower
    decay = jnp.where(causal, jnp.exp(gc - _col_to_row(gc, eye)), 0.0)
    kb = k * beta
    eg = jnp.exp(gc)
    m = jnp.where(strict, _dot_nt(kb, k) * decay, 0.0)
    t_inv = _unit_lower_inverse(m, ri, ci, eye, chunk)
    sol = _dot(t_inv, jnp.concatenate([v * beta, kb * eg], axis=1))
    u, w = sol[:, :DN_DV], sol[:, DN_DV:]
    qk = jnp.where(causal, _dot_nt(q, k) * decay, 0.0)

    def own_block(x):
        return jnp.concatenate(
            [x[g * chunk:(g + 1) * chunk, group_lanes(g)] for g in range(groups)], axis=0)

    s_all = state[...]
    v_new = u - own_block(_dot(w, s_all))
    o = own_block(_dot(q * eg, s_all)) + _dot(qk, v_new)
    wide = (rows, groups * DN_DV)
    row_group = lax.broadcasted_iota(jnp.int32, wide, 0) >> log2c
    lane_group = lax.broadcasted_iota(jnp.int32, wide, 1) // DN_DV
    v_wide = jnp.where(row_group == lane_group, jnp.concatenate([v_new] * groups, axis=1), 0.0)
    state[...] = s_all * last_decay + _dot_tn(k * jnp.exp(gc_last - gc), v_wide)

    o = o * lax.rsqrt(jnp.mean(o * o, axis=-1, keepdims=True) + EPS) * nw_ref[...]
    y = o * _silu(jnp.concatenate(gates, axis=0))
    for s in range(nb):
        for h in range(DN_HEADS):
            g = s * DN_HEADS + h
            o_ref[s, :, h * DN_DV:(h + 1) * DN_DV] = y[g * chunk:(g + 1) * chunk, :]

    @pl.when(c_idx == pl.num_programs(1) - 1)
    def _():
        for s in range(nb):
            for h in range(DN_HEADS):
                s_out_ref[s, h] = state[:, group_lanes(s * DN_HEADS + h)]


def _dn_mixer(proj, conv0, s0, cw, par, nw, *, batch, seq_len, chunk, nb):
    nc = proj.shape[1] // chunk
    return pl.pallas_call(
        functools.partial(_dn_kernel, chunk=chunk, seq_len=seq_len, nb=nb),
        out_shape=(jax.ShapeDtypeStruct((batch, nc * chunk, DN_V), F32),
                   jax.ShapeDtypeStruct((batch, DN_HEADS, DN_DK, DN_DV), F32)),
        grid=(batch // nb, nc),
        in_specs=[pl.BlockSpec((nb, chunk, DN_CONV_DIM), lambda b, c: (b, c, EV_QKV // DN_CONV_DIM)),
                  pl.BlockSpec((nb, chunk, DN_V), lambda b, c: (b, c, EV_GATE // DN_V)),
                  pl.BlockSpec((nb, chunk, SMALL_W), lambda b, c: (b, c, EV_SMALL // SMALL_W)),
                  pl.BlockSpec((nb, CONV_PAD, DN_CONV_DIM), lambda b, c: (b, 0, 0)),
                  pl.BlockSpec((nb, DN_HEADS, DN_DK, DN_DV), lambda b, c: (b, 0, 0, 0)),
                  pl.BlockSpec((CONV_W, DN_CONV_DIM), lambda b, c: (0, 0)),
                  pl.BlockSpec((8, SMALL_W), lambda b, c: (0, 0)),
                  pl.BlockSpec((1, DN_DV), lambda b, c: (0, 0))],
        out_specs=(pl.BlockSpec((nb, chunk, DN_V), lambda b, c: (b, c, 0)),
                   pl.BlockSpec((nb, DN_HEADS, DN_DK, DN_DV), lambda b, c: (b, 0, 0, 0))),
        scratch_shapes=[pltpu.VMEM((nb, chunk + CONV_PAD, DN_CONV_DIM), F32),
                        pltpu.VMEM((DN_DK, nb * DN_HEADS * DN_DV), F32)],
        compiler_params=_params(("parallel", "arbitrary")),
        name="dn_mixer",
    )(proj, proj, proj, conv0, s0, cw, par, nw)


def _ssd_kernel(xbc_ref, z_ref, small_ref, conv0_ref, s0_ref, cw_ref, cb_ref, par_ref, nw_ref,
                y_ref, s_out_ref, xext, state, ybuf, *, chunk, seq_len):
    c_idx = pl.program_id(1)

    @pl.when(c_idx == 0)
    def _():
        xext[0:CONV_PAD, :] = conv0_ref[0]
        state[...] = s0_ref[0]

    xbc = _silu(_causal_conv(xbc_ref, xext, cw_ref, chunk) + cb_ref[...])
    small = small_ref[...]
    par = par_ref[...]
    dt_all = _softplus(small + par[1:2, :])
    valid = _valid_rows(chunk, seq_len, small.shape)
    if valid is not None:
        dt_all = jnp.where(valid, dt_all, 0.0)
    ac_all = _cumsum_rows(dt_all * -jnp.exp(par[0:1, :]))

    ri, ci, eye, causal, strict = _tri_masks(chunk)
    gw = SSM_GROUPS * SSM_N
    heads_per_group = SSM_HEADS // SSM_GROUPS
    for g in range(SSM_GROUPS):
        bm = xbc[:, SSM_INNER + g * SSM_N:SSM_INNER + (g + 1) * SSM_N]
        cm = xbc[:, SSM_INNER + gw + g * SSM_N:SSM_INNER + gw + (g + 1) * SSM_N]
        cb = _dot_nt(cm, bm)
        for hh in range(heads_per_group):
            h = g * heads_per_group + hh
            lane = 2 * DN_HEADS + h
            ac = ac_all[:, lane:lane + 1]
            dt = dt_all[:, lane:lane + 1]
            decay = jnp.where(causal, jnp.exp(ac - _col_to_row(ac, eye)), 0.0)
            xh = xbc[:, h * SSM_P:(h + 1) * SSM_P]
            xdt = xh * dt
            s = state[h]
            y = _dot(cb * decay, xdt) + _dot_nt(cm * jnp.exp(ac), s)
            ac_last = ac[chunk - 1:chunk, :]
            state[h] = s * jnp.exp(ac_last) + _dot_tn(xdt, bm * jnp.exp(ac_last - ac))
            ybuf[:, h * SSM_P:(h + 1) * SSM_P] = y + par[2:3, lane:lane + 1] * xh

    y = ybuf[...] * _silu(z_ref[...])
    nw = nw_ref[...]
    gn = SSM_INNER // SSM_GROUPS
    for g in range(SSM_GROUPS):
        yg = y[:, g * gn:(g + 1) * gn]
        yg = yg * lax.rsqrt(jnp.mean(yg * yg, axis=-1, keepdims=True) + EPS)
        y_ref[:, g * gn:(g + 1) * gn] = yg * nw[:, g * gn:(g + 1) * gn]

    @pl.when(c_idx == pl.num_programs(1) - 1)
    def _():
        s_out_ref[0] = state[...]


def _ssd_mixer(proj, conv0, s0, cw, cb, par, nw, *, batch, seq_len, chunk):
    nc = -(-seq_len // chunk)
    rows = batch * nc * chunk
    row = lambda b, c: b * nc + c
    return pl.pallas_call(
        functools.partial(_ssd_kernel, chunk=chunk, seq_len=seq_len),
        out_shape=(jax.ShapeDtypeStruct((rows, SSM_INNER), F32),
                   jax.ShapeDtypeStruct((batch, SSM_HEADS, SSM_P, SSM_N), F32)),
        grid=(batch, nc),
        in_specs=[pl.BlockSpec((chunk, SSM_CONV_DIM), lambda b, c: (row(b, c), EV_XBC // SSM_CONV_DIM)),
                  pl.BlockSpec((chunk, SSM_INNER), lambda b, c: (row(b, c), EV_Z // SSM_INNER)),
                  pl.BlockSpec((chunk, SMALL_W), lambda b, c: (row(b, c), EV_SMALL // SMALL_W)),
                  pl.BlockSpec((1, CONV_PAD, SSM_CONV_DIM), lambda b, c: (b, 0, 0)),
                  pl.BlockSpec((1, SSM_HEADS, SSM_P, SSM_N), lambda b, c: (b, 0, 0, 0)),
                  pl.BlockSpec((CONV_W, SSM_CONV_DIM), lambda b, c: (0, 0)),
                  pl.BlockSpec((1, SSM_CONV_DIM), lambda b, c: (0, 0)),
                  pl.BlockSpec((8, SMALL_W), lambda b, c: (0, 0)),
                  pl.BlockSpec((1, SSM_INNER), lambda b, c: (0, 0))],
        out_specs=(pl.BlockSpec((chunk, SSM_INNER), lambda b, c: (row(b, c), 0)),
                   pl.BlockSpec((1, SSM_HEADS, SSM_P, SSM_N), lambda b, c: (b, 0, 0, 0))),
        scratch_shapes=[pltpu.VMEM((chunk + CONV_PAD, SSM_CONV_DIM), F32),
                        pltpu.VMEM((SSM_HEADS, SSM_P, SSM_N), F32),
                        pltpu.VMEM((chunk, SSM_INNER), F32)],
        compiler_params=_params(("parallel", "arbitrary")),
        name="ssd_mixer",
    )(proj, proj, proj, conv0, s0, cw, cb, par, nw)


def _gla_kernel(qkv_ref, og_ref, glr_ref, s0_ref, wg_ref, bg_ref, nw_ref,
                o_ref, s_out_ref, state, *, chunk, seq_len):
    c_idx = pl.program_id(1)

    @pl.when(c_idx == 0)
    def _():
        state[...] = s0_ref[0]

    x = bg_ref[...] + jnp.dot(glr_ref[...].astype(BF16), wg_ref[...], preferred_element_type=F32)
    gk = (jnp.minimum(x, 0.0) - jnp.log(1.0 + jnp.exp(-jnp.abs(x)))) / GLA_NORMALIZER
    valid = _valid_rows(chunk, seq_len, gk.shape)
    if valid is not None:
        gk = jnp.where(valid, gk, 0.0)
    b_all = _cumsum_rows(gk)

    ri, ci, eye_c, causal, strict = _tri_masks(chunk)
    eye_k = _tri_masks(GLA_DK)[2]
    nw = nw_ref[...]
    for h in range(GLA_HEADS):
        b = b_all[:, h * GLA_DK:(h + 1) * GLA_DK]
        q = qkv_ref[:, h * GLA_DK:(h + 1) * GLA_DK] * (GLA_DK ** -0.5)
        k = qkv_ref[:, GLA_K + h * GLA_DK:GLA_K + (h + 1) * GLA_DK]
        v = qkv_ref[:, 2 * GLA_K + h * GLA_DV:2 * GLA_K + (h + 1) * GLA_DV]
        qe = q * jnp.exp(b)
        att = jnp.where(causal, _dot_nt(qe, k * jnp.exp(-b)), 0.0)
        s = state[h]
        o = _dot(qe, s) + _dot(att, v)
        b_last = b[chunk - 1:chunk, :]
        state[h] = s * _row_to_col(jnp.exp(b_last), eye_k) + _dot_tn(k * jnp.exp(b_last - b), v)
        o = o * lax.rsqrt(jnp.mean(o * o, axis=-1, keepdims=True) + EPS) * nw
        o_ref[:, h * GLA_DV:(h + 1) * GLA_DV] = o * _silu(og_ref[:, h * GLA_DV:(h + 1) * GLA_DV])

    @pl.when(c_idx == pl.num_programs(1) - 1)
    def _():
        s_out_ref[0] = state[...]


def _gla_mixer(proj, s0, wg, bg, nw, *, batch, seq_len, chunk):
    nc = -(-seq_len // chunk)
    rows = batch * nc * chunk
    row = lambda b, c: b * nc + c
    qkv_w = 2 * GLA_K + GLA_V
    return pl.pallas_call(
        functools.partial(_gla_kernel, chunk=chunk, seq_len=seq_len),
        out_shape=(jax.ShapeDtypeStruct((rows, GLA_V), F32),
                   jax.ShapeDtypeStruct((batch, GLA_HEADS, GLA_DK, GLA_DV), F32)),
        grid=(batch, nc),
        in_specs=[pl.BlockSpec((chunk, qkv_w), lambda b, c: (row(b, c), OD_QKV // qkv_w)),
                  pl.BlockSpec((chunk, GLA_V), lambda b, c: (row(b, c), OD_OG // GLA_V)),
                  pl.BlockSpec((chunk, SMALL_W), lambda b, c: (row(b, c), OD_GLR // SMALL_W)),
                  pl.BlockSpec((1, GLA_HEADS, GLA_DK, GLA_DV), lambda b, c: (b, 0, 0, 0)),
                  pl.BlockSpec((SMALL_W, GLA_K), lambda b, c: (0, 0)),
                  pl.BlockSpec((1, GLA_K), lambda b, c: (0, 0)),
                  pl.BlockSpec((1, GLA_DV), lambda b, c: (0, 0))],
        out_specs=(pl.BlockSpec((chunk, GLA_V), lambda b, c: (row(b, c), 0)),
                   pl.BlockSpec((1, GLA_HEADS, GLA_DK, GLA_DV), lambda b, c: (b, 0, 0, 0))),
        scratch_shapes=[pltpu.VMEM((GLA_HEADS, GLA_DK, GLA_DV), F32)],
        compiler_params=_params(("parallel", "arbitrary")),
        name="gla_mixer",
    )(proj, proj, proj, s0, wg, bg, nw)


def _take_top(x, key, count):
    nr, nt = x.shape
    big = jnp.int32(2 ** 30)
    slot = lax.broadcasted_iota(jnp.int32, (count, nt), 0)

    def body(r, carry):
        x, rank, vals = carry
        m = jnp.max(x, axis=0, keepdims=True)
        first = jnp.min(jnp.where(x == m, key, big), axis=0, keepdims=True)
        hit = key == first
        return (jnp.where(hit, -jnp.inf, x), jnp.where(hit, r, rank), jnp.where(slot == r, m, vals))

    init = (x, jnp.full((nr, nt), count, jnp.int32), jnp.zeros((count, nt), F32))
    _, rank, vals = lax.fori_loop(0, count, body, init)
    return vals, rank


def _pair_candidates(v1, v2):
    k = PEER_TOPK
    nt = v1.shape[1]
    j8 = lax.broadcasted_iota(jnp.int32, (8, nt), 0)
    j16 = lax.broadcasted_iota(jnp.int32, (k, nt), 0)
    sums, pos = [v1[0:1, :] + v2], [j16]
    for i in range(1, 8):
        sums.append(jnp.where(j8 < k // (i + 1), v1[i:i + 1, :] + v2[0:8, :], -jnp.inf))
        pos.append(i * k + j8)
    sums.append(v1[8:k, :] + v2[0:1, :])
    pos.append((8 + j8) * k)
    return jnp.concatenate(sums, axis=0), jnp.concatenate(pos, axis=0)


def _take_top_values(xs, count):
    nt = xs[0].shape[1]
    slot = lax.broadcasted_iota(jnp.int32, (count, nt), 0)

    def body(r, carry):
        out = []
        for x, vals in carry:
            m = jnp.max(x, axis=0, keepdims=True)
            out.append((jnp.where(x == m, -jnp.inf, x), jnp.where(slot == r, m, vals)))
        return tuple(out)

    init = tuple((x, jnp.zeros((count, nt), F32)) for x in xs)
    return [vals for _, vals in lax.fori_loop(0, count, body, init)]


def _count_rows(mask):
    return jnp.sum(jnp.where(mask, 1.0, 0.0), axis=0, keepdims=True)


def _per_first_key(sel):
    k = PEER_TOPK
    cnt = jnp.where(sel, 1.0, 0.0)
    out = [jnp.sum(cnt[0:k, :], axis=0, keepdims=True)]
    out += [jnp.sum(cnt[k + 8 * (i - 1):k + 8 * i, :], axis=0, keepdims=True) for i in range(1, 8)]
    out += [cnt[k + 56 + i:k + 57 + i, :] for i in range(k - 8)]
    return out


def _route_tables_exact(s1, s2):
    k = PEER_TOPK
    key = lax.broadcasted_iota(jnp.int32, s1.shape, 0)
    v1, rank1 = _take_top(s1, key, k)
    v2, rank2 = _take_top(s2, key, k)
    cand, pos = _pair_candidates(v1, v2)
    _, crank = _take_top(cand, pos, k)
    sel = crank < k
    z = jnp.sum(jnp.where(sel, jnp.exp(cand - (v1[0:1, :] + v2[0:1, :])), 0.0), axis=0, keepdims=True)
    n = jnp.zeros(s1.shape, F32)
    for i, n_i in enumerate(_per_first_key(sel)):
        n = jnp.where(rank1 == i, n_i, n)
    e1 = jnp.where(rank1 < k, jnp.exp(s1 - v1[0:1, :]) / z, 0.0)
    e2 = jnp.where(rank2 < k, jnp.exp(s2 - v2[0:1, :]), 0.0)
    return n, e1, rank2.astype(F32), e2


def _route_tables_distinct(s1, s2):
    k = PEER_TOPK
    v1, v2 = _take_top_values([s1, s2], k)
    in1 = s1 >= v1[k - 1:k, :]
    in2 = s2 >= v2[k - 1:k, :]
    cand, _ = _pair_candidates(v1, v2)
    cv, = _take_top_values([cand], k)
    sel = cand >= cv[k - 1:k, :]
    ok = (_count_rows(in1) == k) & (_count_rows(in2) == k) & (_count_rows(sel) == k)
    z = jnp.sum(jnp.where(sel, jnp.exp(cand - (v1[0:1, :] + v2[0:1, :])), 0.0), axis=0, keepdims=True)
    n = jnp.zeros(s1.shape, F32)
    rank2 = jnp.full(s2.shape, float(k), F32)
    for i, n_i in enumerate(_per_first_key(sel)):
        n = jnp.where(s1 == v1[i:i + 1, :], n_i, n)
        rank2 = jnp.where(s2 == v2[i:i + 1, :], float(i), rank2)
    e1 = jnp.where(in1, jnp.exp(s1 - v1[0:1, :]) / z, 0.0)
    e2 = jnp.where(in2, jnp.exp(s2 - v2[0:1, :]), 0.0)
    return (n, e1, rank2, e2), ok


def _peer_route_kernel(q_ref, k1_ref, k2_ref, n_ref, e1_ref, r2_ref, e2_ref, s1_ref, s2_ref):
    q = q_ref[...]
    s1_ref[...] = _dot_nt(k1_ref[0], q[:, :PEER_HALF])
    s2_ref[...] = _dot_nt(k2_ref[0], q[:, PEER_HALF:])
    tt = q.shape[0]
    lw = min(LANES, tt)
    for lc in range(tt // lw):
        sl = slice(lc * lw, (lc + 1) * lw)

        def store(tables):
            n, e1, rank2, e2 = tables
            n_ref[0, :, sl] = n
            e1_ref[0, :, sl] = e1
            r2_ref[:, sl] = rank2.astype(BF16)
            e2_ref[:, sl] = e2.astype(BF16)

        tables, ok = _route_tables_distinct(s1_ref[:, sl], s2_ref[:, sl])
        store(tables)

        @pl.when(jnp.min(jnp.where(ok, 1, 0)) == 0)
        def _():
            store(_route_tables_exact(s1_ref[:, sl], s2_ref[:, sl]))


def _peer_route(q, keys1, keys2, tt):
    t = q.shape[0]
    out_a = jax.ShapeDtypeStruct((PEER_HEADS, PEER_NKEYS, t), F32)
    out_b = jax.ShapeDtypeStruct((PEER_HEADS * PEER_NKEYS, t), BF16)
    spec_a = pl.BlockSpec((1, PEER_NKEYS, tt), lambda i, h: (h, 0, i))
    spec_b = pl.BlockSpec((PEER_NKEYS, tt), lambda i, h: (h, i))
    key_spec = pl.BlockSpec((1, PEER_NKEYS, PEER_HALF), lambda i, h: (h, 0, 0))
    return pl.pallas_call(
        _peer_route_kernel,
        out_shape=(out_a, out_a, out_b, out_b),
        grid=(t // tt, PEER_HEADS),
        in_specs=[pl.BlockSpec((tt, PEER_DQ), lambda i, h: (i, h)), key_spec, key_spec],
        out_specs=(spec_a, spec_a, spec_b, spec_b),
        scratch_shapes=[pltpu.VMEM((PEER_NKEYS, tt), F32), pltpu.VMEM((PEER_NKEYS, tt), F32)],
        compiler_params=_params(("parallel", "arbitrary")),
        name="peer_route",
    )(q, keys1, keys2)


def _peer_dense_kernel(xa_ref, xc_ref, lnw_ref, n_ref, e1_ref, r2_in_ref, e2_in_ref, u_ref, v_ref, o_ref,
                       h_ref, pre_ref, pre2_ref, act_ref, act2_ref, acc_ref, r2_ref, e2_ref,
                       *, rows_per_step, n_blocks, n_work):
    g = pl.program_id(0)
    last = n_work - 1
    ja = jnp.minimum(g, last) % n_blocks
    jb = jnp.clip(g - 1, 0, last) % n_blocks
    jc = jnp.clip(g - 2, 0, last) % n_blocks

    @pl.when(g == 0)
    def _():
        for ref in (pre_ref, pre2_ref, act_ref, act2_ref):
            ref[...] = jnp.zeros_like(ref)

    @pl.when(ja == 0)
    def _():
        x = xa_ref[...]
        y = x * lax.rsqrt(jnp.mean(x * x, axis=-1, keepdims=True) + EPS)
        h_ref[...] = (y * lnw_ref[...]).astype(BF16)

    @pl.when(jb == 0)
    def _():
        r2_ref[...] = r2_in_ref[...]
        e2_ref[...] = e2_in_ref[...]

    @pl.when(jc == 0)
    def _():
        acc_ref[...] = jnp.zeros_like(acc_ref)

    tt = h_ref.shape[0]
    lw = min(LANES, tt)
    reps = PEER_NKEYS // BF16_ROWS

    def stages(pre_w, pre_r, act_w, act_r):
        zero = jnp.zeros((PEER_NKEYS, lw), BF16)
        mxu_width = min(256, tt)

        def stage_a(part):
            tok = slice(part * mxu_width, (part + 1) * mxu_width)
            pre_w[:, tok] = lax.dot_general(u_ref[...], h_ref[tok, :], _NT, preferred_element_type=F32)

        def stage_b(lc, aa):
            sl = slice(lc * lw, (lc + 1) * lw)
            rows = slice(aa * PEER_NKEYS, (aa + 1) * PEER_NKEYS)
            gate = zero
            for h in range(PEER_HEADS):
                n_row = jnp.broadcast_to(n_ref[h, aa:aa + 1, sl], (BF16_ROWS, lw)).astype(BF16)
                e1_row = jnp.broadcast_to(e1_ref[h, aa:aa + 1, sl], (BF16_ROWS, lw)).astype(BF16)
                n_tile = jnp.concatenate([n_row] * reps, axis=0)
                e1_tile = jnp.concatenate([e1_row] * reps, axis=0)
                keys = slice(h * PEER_NKEYS, (h + 1) * PEER_NKEYS)
                gate = gate + jnp.where(r2_ref[keys, sl] < n_tile, e2_ref[keys, sl], zero) * e1_tile
            act = _gelu_tanh(pre_r[rows, sl]).astype(BF16) * gate
            act_w[sl, rows] = act.T

        def stage_c(part):
            col = slice(part * 256, (part + 1) * 256)
            acc_ref[:, col] += jnp.dot(act_r[...], v_ref[:, col], preferred_element_type=F32)

        mxu = [functools.partial(stage_a, p) for p in range(tt // mxu_width)]
        mxu += [functools.partial(stage_c, p) for p in range(acc_ref.shape[1] // 256)]
        vpu = [functools.partial(stage_b, lc, aa) for lc in range(tt // lw) for aa in range(rows_per_step)]
        per = -(-len(vpu) // len(mxu))
        for i, piece in enumerate(mxu):
            piece()
            for tile in vpu[i * per:(i + 1) * per]:
                tile()

    @pl.when(g % 2 == 0)
    def _():
        stages(pre_ref, pre2_ref, act2_ref, act_ref)

    @pl.when(g % 2 == 1)
    def _():
        stages(pre2_ref, pre_ref, act_ref, act2_ref)

    @pl.when((jc == n_blocks - 1) & (g >= 2))
    def _():
        o_ref[...] = xc_ref[...] + acc_ref[...]


def _peer_dense(x, lnw, route, u, v, tt, rows_per_step):
    t, d = x.shape
    eb = rows_per_step * PEER_NKEYS
    n_blocks = PEER_N // eb
    n_work = (t // tt) * n_blocks

    def item(lag):
        def at(g):
            w = jnp.clip(g - lag, 0, n_work - 1)
            return w // n_blocks, w % n_blocks
        return at

    a, b, c = item(0), item(1), item(2)
    spec_rows = pl.BlockSpec((PEER_HEADS, rows_per_step, tt), lambda g: (0, b(g)[1], b(g)[0]))
    spec_keys = pl.BlockSpec((PEER_HEADS * PEER_NKEYS, tt), lambda g: (0, b(g)[0]))
    return pl.pallas_call(
        functools.partial(_peer_dense_kernel, rows_per_step=rows_per_step, n_blocks=n_blocks, n_work=n_work),
        out_shape=jax.ShapeDtypeStruct((t, d), F32),
        grid=(n_work + 2,),
        in_specs=[pl.BlockSpec((tt, d), lambda g: (a(g)[0], 0)),
                  pl.BlockSpec((tt, d), lambda g: (c(g)[0], 0)),
                  pl.BlockSpec((1, d), lambda g: (0, 0)),
                  spec_rows, spec_rows, spec_keys, spec_keys,
                  pl.BlockSpec((eb, d), lambda g: (a(g)[1], 0)),
                  pl.BlockSpec((eb, d), lambda g: (c(g)[1], 0))],
        out_specs=pl.BlockSpec((tt, d), lambda g: (c(g)[0], 0)),
        scratch_shapes=[pltpu.VMEM((tt, d), BF16), pltpu.VMEM((eb, tt), F32), pltpu.VMEM((eb, tt), F32),
                        pltpu.VMEM((tt, eb), BF16), pltpu.VMEM((tt, eb), BF16), pltpu.VMEM((tt, d), F32),
                        pltpu.VMEM((PEER_HEADS * PEER_NKEYS, tt), BF16),
                        pltpu.VMEM((PEER_HEADS * PEER_NKEYS, tt), BF16)],
        compiler_params=_params(("arbitrary",)),
        name="peer_dense",
    )(x, x, lnw.reshape(1, d), *route, u, v)


def _peer_ffn(x, lnw, wq, keys1, keys2, u, v):
    tt = min(512, x.shape[0])
    q = _norm_matmul(x, lnw, wq)
    route = _peer_route(q, keys1, keys2, tt)
    return _peer_dense(x, lnw, route, u, v, tt, rows_per_step=8)


def _pad_lanes(vec, offset):
    return jnp.zeros((SMALL_W,), F32).at[offset:offset + vec.shape[0]].set(vec.astype(F32))


def _conv_history(buf):
    return jnp.pad(buf, ((0, 0), (CONV_PAD - (CONV_W - 1), 0), (0, 0)))


def kernel(x_prompt, x_sample, state_dn_conv, state_dn, state_ssm_conv, state_ssm, state_gla, ln_mix, ln_ffn, ln_final, w_in_even, w_out_even, dn_conv_w, dn_a_log, dn_dt_bias, dn_norm_w, ssm_conv_w, ssm_conv_b, ssm_a_log, ssm_dt_bias, ssm_d, ssm_norm_w, w_in_odd, gla_w_gate2, gla_b_gate, gla_norm_w, w_out_odd, peer_w_q, peer_keys1, peer_keys2, peer_u, peer_v):
    we = w_in_even[0]
    o = 0
    cuts = {}
    for name, size in (("qkv", DN_CONV_DIM), ("a", DN_HEADS), ("b", DN_HEADS), ("gate", DN_V),
                       ("z", SSM_INNER), ("xbc", SSM_CONV_DIM), ("dt", SSM_HEADS)):
        cuts[name] = we[:, o:o + size]
        o += size
    small_w = jnp.concatenate([cuts["a"], cuts["b"], cuts["dt"]], axis=1)
    small_w = jnp.pad(small_w, ((0, 0), (0, SMALL_W - small_w.shape[1])))
    w_even = jnp.concatenate([cuts["qkv"], cuts["gate"], cuts["xbc"], cuts["z"], small_w], axis=1).astype(BF16)
    wo = w_in_odd[0]
    qkv_w = 2 * GLA_K + GLA_V
    glr_w = jnp.pad(wo[:, qkv_w:qkv_w + GLA_LR], ((0, 0), (0, SMALL_W - GLA_LR)))
    w_odd = jnp.concatenate([wo[:, :qkv_w], wo[:, qkv_w + GLA_LR:], glr_w], axis=1).astype(BF16)
    w_out_dn = w_out_even[0, :DN_V].astype(BF16)
    w_out_ssm = w_out_even[0, DN_V:].astype(BF16)
    w_out_gla = w_out_odd[0].astype(BF16)
    wg = jnp.pad(gla_w_gate2[0], ((0, SMALL_W - GLA_LR), (0, 0))).astype(BF16)
    wq = peer_w_q.astype(BF16)
    k1 = peer_keys1.astype(BF16)
    k2 = peer_keys2.astype(BF16)
    pu = peer_u.astype(BF16)
    pv = peer_v.astype(BF16)

    zero8 = jnp.zeros((8 - 2, SMALL_W), F32)
    dn_par = jnp.concatenate([_pad_lanes(dn_a_log[0], 0)[None], _pad_lanes(dn_dt_bias[0], 0)[None], zero8])
    lane0 = 2 * DN_HEADS
    ssm_par = jnp.concatenate([_pad_lanes(ssm_a_log[0], lane0)[None], _pad_lanes(ssm_dt_bias[0], lane0)[None],
                               _pad_lanes(ssm_d[0], lane0)[None], zero8[1:]])

    def trunk(x, batch, seq_len, conv_dn0, s_dn0, conv_ssm0, s_ssm0, s_gla0):
        chunk = PROMPT_CHUNK if seq_len >= PROMPT_CHUNK else STEP_CHUNK
        nc = -(-seq_len // chunk)
        padded = nc * chunk != seq_len
        dn_nb = max(1, min(batch, 256 // (DN_HEADS * chunk)))
        while batch % dn_nb:
            dn_nb -= 1

        def to_seq(a):
            if not padded:
                return a
            a = a.reshape(batch, seq_len, a.shape[-1])
            return jnp.pad(a, ((0, 0), (0, nc * chunk - seq_len), (0, 0))).reshape(batch * nc * chunk, -1)

        def from_seq(a):
            if not padded:
                return a
            return a.reshape(batch, nc * chunk, a.shape[-1])[:, :seq_len].reshape(batch * seq_len, -1)

        def last_rows(buf, proj, col, width):
            tail = proj.reshape(batch, seq_len, proj.shape[-1])[:, -min(seq_len, CONV_W - 1):, col:col + width]
            return jnp.concatenate([buf, tail], axis=1)[:, -(CONV_W - 1):]

        seq = dict(batch=batch, seq_len=seq_len, chunk=chunk)

        proj = _norm_matmul(x, ln_mix[0], w_even)
        proj_seq = to_seq(proj)
        o_dn, s_dn = _dn_mixer(proj_seq.reshape(batch, nc * chunk, EV_WIDTH), _conv_history(conv_dn0[0]),
                               s_dn0[0], dn_conv_w[0], dn_par, dn_norm_w[0].reshape(1, DN_DV),
                               nb=dn_nb, **seq)
        o_dn = o_dn.reshape(batch * nc * chunk, DN_V)
        y_ssm, s_ssm = _ssd_mixer(proj_seq, _conv_history(conv_ssm0[0]), s_ssm0[0], ssm_conv_w[0],
                                  ssm_conv_b[0].reshape(1, SSM_CONV_DIM), ssm_par,
                                  ssm_norm_w[0].reshape(1, SSM_INNER), **seq)
        conv_dn = last_rows(conv_dn0[0], proj, EV_QKV, DN_CONV_DIM)
        conv_ssm = last_rows(conv_ssm0[0], proj, EV_XBC, SSM_CONV_DIM)
        x = _proj_residual(x, [from_seq(o_dn), from_seq(y_ssm)], [w_out_dn, w_out_ssm])
        x = _peer_ffn(x, ln_ffn[0], wq[0], k1[0], k2[0], pu[0], pv[0])

        proj = _norm_matmul(x, ln_mix[1], w_odd)
        o_gla, s_gla = _gla_mixer(to_seq(proj), s_gla0[0], wg, gla_b_gate[0].reshape(1, GLA_K),
                                  gla_norm_w[0].reshape(1, GLA_DV), **seq)
        x = _proj_residual(x, [from_seq(o_gla)], [w_out_gla])
        x = _peer_ffn(x, ln_ffn[1], wq[1], k1[1], k2[1], pu[1], pv[1])

        y = _final_norm(x, ln_final).reshape(batch, seq_len, D_MODEL)
        return y, conv_dn[None], s_dn[None], conv_ssm[None], s_ssm[None], s_gla[None]

    bp, lp = x_prompt.shape[:2]
    bs, ls = x_sample.shape[:2]

    def zeros_like_state(s):
        return jnp.zeros((s.shape[0], bp) + s.shape[2:], s.dtype)

    prompt = trunk(x_prompt.reshape(bp * lp, D_MODEL), bp, lp, zeros_like_state(state_dn_conv),
                   zeros_like_state(state_dn), zeros_like_state(state_ssm_conv),
                   zeros_like_state(state_ssm), zeros_like_state(state_gla))
    sample = trunk(x_sample.reshape(bs * ls, D_MODEL), bs, ls, state_dn_conv, state_dn,
                   state_ssm_conv, state_ssm, state_gla)
    return (prompt[0], sample[0]) + prompt[1:] + sample[1:]
```

```python
import functools

import jax
import jax.numpy as jnp
from jax import lax
from jax.experimental import pallas as pl
from jax.experimental.pallas import tpu as pltpu

F32 = jnp.float32
BF16 = jnp.bfloat16
EPS = 1e-6

D_MODEL = 1024
CONV_W = 4
CONV_PAD = 8
PROMPT_CHUNK = 64
STEP_CHUNK = 8

DN_HEADS, DN_DK, DN_DV = 4, 128, 128
DN_QK = DN_HEADS * DN_DK
DN_V = DN_HEADS * DN_DV
DN_CONV_DIM = 2 * DN_QK + DN_V
SSM_HEADS, SSM_P, SSM_GROUPS, SSM_N = 8, 64, 2, 128
SSM_INNER = SSM_HEADS * SSM_P
SSM_CONV_DIM = SSM_INNER + 2 * SSM_GROUPS * SSM_N
GLA_HEADS, GLA_DK, GLA_DV, GLA_LR = 4, 128, 256, 16
GLA_K = GLA_HEADS * GLA_DK
GLA_V = GLA_HEADS * GLA_DV
GLA_NORMALIZER = 16.0
PEER_HEADS, PEER_DQ, PEER_NKEYS, PEER_TOPK = 8, 256, 128, 16
PEER_HALF = PEER_DQ // 2
PEER_N = PEER_NKEYS * PEER_NKEYS
LANES = 128
BF16_ROWS = 16
SMALL_W = LANES

EV_QKV, EV_GATE, EV_XBC, EV_Z, EV_SMALL = 0, 1536, 2048, 3072, 3584
EV_WIDTH = EV_SMALL + SMALL_W
OD_QKV, OD_OG, OD_GLR = 0, 2048, 3072
OD_WIDTH = OD_GLR + SMALL_W

VMEM_LIMIT = 48 * 1024 * 1024

_NT = (((1,), (1,)), ((), ()))
_TN = (((0,), (0,)), ((), ()))


def _dot(a, b):
    return jnp.dot(a.astype(BF16), b.astype(BF16), preferred_element_type=F32)


def _dot_nt(a, b):
    return lax.dot_general(a.astype(BF16), b.astype(BF16), _NT, preferred_element_type=F32)


def _dot_tn(a, b):
    return lax.dot_general(a.astype(BF16), b.astype(BF16), _TN, preferred_element_type=F32)


def _params(sem):
    return pltpu.CompilerParams(dimension_semantics=sem, vmem_limit_bytes=VMEM_LIMIT)


def _silu(x):
    return x * jax.nn.sigmoid(x)


def _gelu_tanh(x):
    c0 = 0.7978845608028654
    hx = 0.5 * x
    return hx + hx * jnp.tanh(x * (c0 + (c0 * 0.044715) * (x * x)))


def _softplus(x):
    return jnp.maximum(x, 0.0) + jnp.log(1.0 + jnp.exp(-jnp.abs(x)))


def _cumsum_rows(x):
    n = x.shape[0]
    row = lax.broadcasted_iota(jnp.int32, x.shape, 0)
    s = 1
    while s < n:
        x = x + jnp.where(row >= s, pltpu.roll(x, s, axis=0), 0.0)
        s *= 2
    return x


def _col_to_row(col, eye):
    return jnp.sum(jnp.where(eye, col, 0.0), axis=0, keepdims=True)


def _row_to_col(row, eye):
    return jnp.sum(jnp.where(eye, row, 0.0), axis=1, keepdims=True)


def _tri_masks(n):
    ri = lax.broadcasted_iota(jnp.int32, (n, n), 0)
    ci = lax.broadcasted_iota(jnp.int32, (n, n), 1)
    return ri, ci, ri == ci, ri >= ci, ri > ci


def _valid_rows(chunk, seq_len, shape):
    if seq_len % chunk == 0:
        return None
    pos = pl.program_id(1) * chunk + lax.broadcasted_iota(jnp.int32, shape, 0)
    return pos < seq_len


def _causal_conv(x_ref, xext, cw_ref, chunk):
    x = x_ref[...]
    xext[CONV_PAD:CONV_PAD + chunk, :] = x
    cw = cw_ref[...]
    pre = cw[CONV_W - 1:CONV_W, :] * x
    for j in range(1, CONV_W):
        pre = pre + cw[CONV_W - 1 - j:CONV_W - j, :] * xext[CONV_PAD - j:CONV_PAD - j + chunk, :]
    xext[0:CONV_PAD, :] = xext[chunk:chunk + CONV_PAD, :]
    return pre


def _norm_matmul_kernel(x_ref, lnw_ref, w_ref, o_ref):
    x = x_ref[...]
    y = x * lax.rsqrt(jnp.mean(x * x, axis=-1, keepdims=True) + EPS)
    h = (y * lnw_ref[...]).astype(BF16)
    o_ref[...] = jnp.dot(h, w_ref[...], preferred_element_type=F32)


def _norm_matmul(x, lnw, w):
    t, d = x.shape
    n = w.shape[1]
    tm = min(512, t)
    return pl.pallas_call(
        _norm_matmul_kernel,
        out_shape=jax.ShapeDtypeStruct((t, n), F32),
        grid=(t // tm,),
        in_specs=[pl.BlockSpec((tm, d), lambda i: (i, 0)),
                  pl.BlockSpec((1, d), lambda i: (0, 0)),
                  pl.BlockSpec((d, n), lambda i: (0, 0))],
        out_specs=pl.BlockSpec((tm, n), lambda i: (i, 0)),
        compiler_params=_params(("parallel",)),
        name="norm_matmul",
    )(x, lnw.reshape(1, d), w)


def _proj_residual_kernel(*refs, n_in):
    x_ref, o_ref = refs[0], refs[-1]
    acc = x_ref[...]
    for a_ref, w_ref in zip(refs[1:1 + n_in], refs[1 + n_in:1 + 2 * n_in]):
        acc = acc + jnp.dot(a_ref[...].astype(BF16), w_ref[...], preferred_element_type=F32)
    o_ref[...] = acc


def _proj_residual(x, acts, weights):
    t, d = x.shape
    tm = min(512, t)
    n_in = len(acts)
    in_specs = [pl.BlockSpec((tm, d), lambda i: (i, 0))]
    in_specs += [pl.BlockSpec((tm, a.shape[1]), lambda i: (i, 0)) for a in acts]
    in_specs += [pl.BlockSpec(w.shape, lambda i: (0, 0)) for w in weights]
    return pl.pallas_call(
        functools.partial(_proj_residual_kernel, n_in=n_in),
        out_shape=jax.ShapeDtypeStruct((t, d), F32),
        grid=(t // tm,),
        in_specs=in_specs,
        out_specs=pl.BlockSpec((tm, d), lambda i: (i, 0)),
        compiler_params=_params(("parallel",)),
        name="proj_residual",
    )(x, *acts, *weights)


def _final_norm_kernel(x_ref, w_ref, o_ref):
    x = x_ref[...]
    o_ref[...] = x * lax.rsqrt(jnp.mean(x * x, axis=-1, keepdims=True) + EPS) * w_ref[...]


def _final_norm(x, w):
    t, d = x.shape
    tm = min(512, t)
    return pl.pallas_call(
        _final_norm_kernel,
        out_shape=jax.ShapeDtypeStruct((t, d), F32),
        grid=(t // tm,),
        in_specs=[pl.BlockSpec((tm, d), lambda i: (i, 0)), pl.BlockSpec((1, d), lambda i: (0, 0))],
        out_specs=pl.BlockSpec((tm, d), lambda i: (i, 0)),
        compiler_params=_params(("parallel",)),
        name="final_norm",
    )(x, w.reshape(1, d))


def _unit_lower_inverse(m, ri, ci, eye, block):
    t = jnp.where(eye, 1.0, 0.0)
    level = 0
    while (1 << level) < block:
        pair = ((ri >> (level + 1)) == (ci >> (level + 1))) & ((ri >> level) != (ci >> level))
        c = jnp.where(pair, m, 0.0)
        t = t - c if level == 0 else t - _dot(_dot(t, c), t)
        level += 1
    return t


def _dn_kernel(qkv_ref, gate_ref, small_ref, conv0_ref, s0_ref, cw_ref, par_ref, nw_ref,
               o_ref, s_out_ref, xext, state, *, chunk, seq_len, nb):
    c_idx = pl.program_id(1)
    groups = nb * DN_HEADS
    log2c = chunk.bit_length() - 1
    group_lanes = lambda g: slice(g * DN_DV, (g + 1) * DN_DV)

    @pl.when(c_idx == 0)
    def _():
        for s in range(nb):
            xext[s, 0:CONV_PAD, :] = conv0_ref[s]
            for h in range(DN_HEADS):
                state[:, group_lanes(s * DN_HEADS + h)] = s0_ref[s, h]

    par = par_ref[...]
    qs, ks, vs, gcs, betas, gates = [], [], [], [], [], []
    for s in range(nb):
        act = _silu(_causal_conv(qkv_ref.at[s], xext.at[s], cw_ref, chunk))
        small = small_ref[s]
        g_all = -jnp.exp(par[0:1, :]) * _softplus(small + par[1:2, :])
        beta_all = jax.nn.sigmoid(small)
        valid = _valid_rows(chunk, seq_len, small.shape)
        if valid is not None:
            g_all = jnp.where(valid, g_all, 0.0)
            beta_all = jnp.where(valid, beta_all, 0.0)
        gc_all = _cumsum_rows(g_all)
        for h in range(DN_HEADS):
            xq = act[:, h * DN_DK:(h + 1) * DN_DK]
            xk = act[:, DN_QK + h * DN_DK:DN_QK + (h + 1) * DN_DK]
            qs.append(xq * lax.rsqrt(jnp.sum(xq * xq, axis=-1, keepdims=True) + EPS) * (DN_DK ** -0.5))
            ks.append(xk * lax.rsqrt(jnp.sum(xk * xk, axis=-1, keepdims=True) + EPS))
            vs.append(act[:, 2 * DN_QK + h * DN_DV:2 * DN_QK + (h + 1) * DN_DV])
            gcs.append(gc_all[:, h:h + 1])
            betas.append(beta_all[:, DN_HEADS + h:DN_HEADS + h + 1])
            gates.append(gate_ref[s, :, h * DN_DV:(h + 1) * DN_DV])
    q, k, v = (jnp.concatenate(t, axis=0) for t in (qs, ks, vs))
    gc, beta = (jnp.concatenate(t, axis=0) for t in (gcs, betas))
    gc_last = jnp.concatenate([jnp.broadcast_to(t[chunk - 1:chunk, :], (chunk, 1)) for t in gcs], axis=0)
    last_decay = jnp.concatenate(
        [jnp.broadcast_to(jnp.exp(t[chunk - 1:chunk, :]), (1, DN_DV)) for t in gcs], axis=1)

    rows = groups * chunk
    ri, ci, eye, lower, strict_lower = _tri_masks(rows)
    same = (ri >> log2c) == (ci >> log2c)
    causal = same & lower
    strict = same & strict_lower
    decay = jnp.where(causal, jnp.exp(gc - _col_to_row(gc, eye)), 0.0)
    kb = k * beta
    eg = jnp.exp(gc)
    m = jnp.where(strict, _dot_nt(kb, k) * decay, 0.0)
    t_inv = _unit_lower_inverse(m, ri, ci, eye, chunk)
    sol = _dot(t_inv, jnp.concatenate([v * beta, kb * eg], axis=1))
    u, w = sol[:, :DN_DV], sol[:, DN_DV:]
    qk = jnp.where(causal, _dot_nt(q, k) * decay, 0.0)

    def own_block(x):
        return jnp.concatenate(
            [x[g * chunk:(g + 1) * chunk, group_lanes(g)] for g in range(groups)], axis=0)

    s_all = state[...]
    v_new = u - own_block(_dot(w, s_all))
    o = own_block(_dot(q * eg, s_all)) + _dot(qk, v_new)
    wide = (rows, groups * DN_DV)
    row_group = lax.broadcasted_iota(jnp.int32, wide, 0) >> log2c
    lane_group = lax.broadcasted_iota(jnp.int32, wide, 1) // DN_DV
    v_wide = jnp.where(row_group == lane_group, jnp.concatenate([v_new] * groups, axis=1), 0.0)
    state[...] = s_all * last_decay + _dot_tn(k * jnp.exp(gc_last - gc), v_wide)

    o = o * lax.rsqrt(jnp.mean(o * o, axis=-1, keepdims=True) + EPS) * nw_ref[...]
    y = o * _silu(jnp.concatenate(gates, axis=0))
    for s in range(nb):
        for h in range(DN_HEADS):
            g = s * DN_HEADS + h
            o_ref[s, :, h * DN_DV:(h + 1) * DN_DV] = y[g * chunk:(g + 1) * chunk, :]

    @pl.when(c_idx == pl.num_programs(1) - 1)
    def _():
        for s in range(nb):
            for h in range(DN_HEADS):
                s_out_ref[s, h] = state[:, group_lanes(s * DN_HEADS + h)]


def _dn_mixer(proj, conv0, s0, cw, par, nw, *, batch, seq_len, chunk, nb):
    nc = proj.shape[1] // chunk
    return pl.pallas_call(
        functools.partial(_dn_kernel, chunk=chunk, seq_len=seq_len, nb=nb),
        out_shape=(jax.ShapeDtypeStruct((batch, nc * chunk, DN_V), F32),
                   jax.ShapeDtypeStruct((batch, DN_HEADS, DN_DK, DN_DV), F32)),
        grid=(batch // nb, nc),
        in_specs=[pl.BlockSpec((nb, chunk, DN_CONV_DIM), lambda b, c: (b, c, EV_QKV // DN_CONV_DIM)),
                  pl.BlockSpec((nb, chunk, DN_V), lambda b, c: (b, c, EV_GATE // DN_V)),
                  pl.BlockSpec((nb, chunk, SMALL_W), lambda b, c: (b, c, EV_SMALL // SMALL_W)),
                  pl.BlockSpec((nb, CONV_PAD, DN_CONV_DIM), lambda b, c: (b, 0, 0)),
                  pl.BlockSpec((nb, DN_HEADS, DN_DK, DN_DV), lambda b, c: (b, 0, 0, 0)),
                  pl.BlockSpec((CONV_W, DN_CONV_DIM), lambda b, c: (0, 0)),
                  pl.BlockSpec((8, SMALL_W), lambda b, c: (0, 0)),
                  pl.BlockSpec((1, DN_DV), lambda b, c: (0, 0))],
        out_specs=(pl.BlockSpec((nb, chunk, DN_V), lambda b, c: (b, c, 0)),
                   pl.BlockSpec((nb, DN_HEADS, DN_DK, DN_DV), lambda b, c: (b, 0, 0, 0))),
        scratch_shapes=[pltpu.VMEM((nb, chunk + CONV_PAD, DN_CONV_DIM), F32),
                        pltpu.VMEM((DN_DK, nb * DN_HEADS * DN_DV), F32)],
        compiler_params=_params(("parallel", "arbitrary")),
        name="dn_mixer",
    )(proj, proj, proj, conv0, s0, cw, par, nw)


def _ssd_kernel(xbc_ref, z_ref, small_ref, conv0_ref, s0_ref, cw_ref, cb_ref, par_ref, nw_ref,
                y_ref, s_out_ref, xext, state, ybuf, *, chunk, seq_len):
    c_idx = pl.program_id(1)

    @pl.when(c_idx == 0)
    def _():
        xext[0:CONV_PAD, :] = conv0_ref[0]
        state[...] = s0_ref[0]

    xbc = _silu(_causal_conv(xbc_ref, xext, cw_ref, chunk) + cb_ref[...])
    small = small_ref[...]
    par = par_ref[...]
    dt_all = _softplus(small + par[1:2, :])
    valid = _valid_rows(chunk, seq_len, small.shape)
    if valid is not None:
        dt_all = jnp.where(valid, dt_all, 0.0)
    ac_all = _cumsum_rows(dt_all * -jnp.exp(par[0:1, :]))

    ri, ci, eye, causal, strict = _tri_masks(chunk)
    gw = SSM_GROUPS * SSM_N
    heads_per_group = SSM_HEADS // SSM_GROUPS
    for g in range(SSM_GROUPS):
        bm = xbc[:, SSM_INNER + g * SSM_N:SSM_INNER + (g + 1) * SSM_N]
        cm = xbc[:, SSM_INNER + gw + g * SSM_N:SSM_INNER + gw + (g + 1) * SSM_N]
        cb = _dot_nt(cm, bm)
        for hh in range(heads_per_group):
            h = g * heads_per_group + hh
            lane = 2 * DN_HEADS + h
            ac = ac_all[:, lane:lane + 1]
            dt = dt_all[:, lane:lane + 1]
            decay = jnp.where(causal, jnp.exp(ac - _col_to_row(ac, eye)), 0.0)
            xh = xbc[:, h * SSM_P:(h + 1) * SSM_P]
            xdt = xh * dt
            s = state[h]
            y = _dot(cb * decay, xdt) + _dot_nt(cm * jnp.exp(ac), s)
            ac_last = ac[chunk - 1:chunk, :]
            state[h] = s * jnp.exp(ac_last) + _dot_tn(xdt, bm * jnp.exp(ac_last - ac))
            ybuf[:, h * SSM_P:(h + 1) * SSM_P] = y + par[2:3, lane:lane + 1] * xh

    y = ybuf[...] * _silu(z_ref[...])
    nw = nw_ref[...]
    gn = SSM_INNER // SSM_GROUPS
    for g in range(SSM_GROUPS):
        yg = y[:, g * gn:(g + 1) * gn]
        yg = yg * lax.rsqrt(jnp.mean(yg * yg, axis=-1, keepdims=True) + EPS)
        y_ref[:, g * gn:(g + 1) * gn] = yg * nw[:, g * gn:(g + 1) * gn]

    @pl.when(c_idx == pl.num_programs(1) - 1)
    def _():
        s_out_ref[0] = state[...]


def _ssd_mixer(proj, conv0, s0, cw, cb, par, nw, *, batch, seq_len, chunk):
    nc = -(-seq_len // chunk)
    rows = batch * nc * chunk
    row = lambda b, c: b * nc + c
    return pl.pallas_call(
        functools.partial(_ssd_kernel, chunk=chunk, seq_len=seq_len),
        out_shape=(jax.ShapeDtypeStruct((rows, SSM_INNER), F32),
                   jax.ShapeDtypeStruct((batch, SSM_HEADS, SSM_P, SSM_N), F32)),
        grid=(batch, nc),
        in_specs=[pl.BlockSpec((chunk, SSM_CONV_DIM), lambda b, c: (row(b, c), EV_XBC // SSM_CONV_DIM)),
                  pl.BlockSpec((chunk, SSM_INNER), lambda b, c: (row(b, c), EV_Z // SSM_INNER)),
                  pl.BlockSpec((chunk, SMALL_W), lambda b, c: (row(b, c), EV_SMALL // SMALL_W)),
                  pl.BlockSpec((1, CONV_PAD, SSM_CONV_DIM), lambda b, c: (b, 0, 0)),
                  pl.BlockSpec((1, SSM_HEADS, SSM_P, SSM_N), lambda b, c: (b, 0, 0, 0)),
                  pl.BlockSpec((CONV_W, SSM_CONV_DIM), lambda b, c: (0, 0)),
                  pl.BlockSpec((1, SSM_CONV_DIM), lambda b, c: (0, 0)),
                  pl.BlockSpec((8, SMALL_W), lambda b, c: (0, 0)),
                  pl.BlockSpec((1, SSM_INNER), lambda b, c: (0, 0))],
        out_specs=(pl.BlockSpec((chunk, SSM_INNER), lambda b, c: (row(b, c), 0)),
                   pl.BlockSpec((1, SSM_HEADS, SSM_P, SSM_N), lambda b, c: (b, 0, 0, 0))),
        scratch_shapes=[pltpu.VMEM((chunk + CONV_PAD, SSM_CONV_DIM), F32),
                        pltpu.VMEM((SSM_HEADS, SSM_P, SSM_N), F32),
                        pltpu.VMEM((chunk, SSM_INNER), F32)],
        compiler_params=_params(("parallel", "arbitrary")),
        name="ssd_mixer",
    )(proj, proj, proj, conv0, s0, cw, cb, par, nw)


def _gla_kernel(qkv_ref, og_ref, glr_ref, s0_ref, wg_ref, bg_ref, nw_ref,
                o_ref, s_out_ref, state, *, chunk, seq_len):
    c_idx = pl.program_id(1)

    @pl.when(c_idx == 0)
    def _():
        state[...] = s0_ref[0]

    x = bg_ref[...] + jnp.dot(glr_ref[...].astype(BF16), wg_ref[...], preferred_element_type=F32)
    gk = (jnp.minimum(x, 0.0) - jnp.log(1.0 + jnp.exp(-jnp.abs(x)))) / GLA_NORMALIZER
    valid = _valid_rows(chunk, seq_len, gk.shape)
    if valid is not None:
        gk = jnp.where(valid, gk, 0.0)
    b_all = _cumsum_rows(gk)

    ri, ci, eye_c, causal, strict = _tri_masks(chunk)
    eye_k = _tri_masks(GLA_DK)[2]
    nw = nw_ref[...]
    for h in range(GLA_HEADS):
        b = b_all[:, h * GLA_DK:(h + 1) * GLA_DK]
        q = qkv_ref[:, h * GLA_DK:(h + 1) * GLA_DK] * (GLA_DK ** -0.5)
        k = qkv_ref[:, GLA_K + h * GLA_DK:GLA_K + (h + 1) * GLA_DK]
        v = qkv_ref[:, 2 * GLA_K + h * GLA_DV:2 * GLA_K + (h + 1) * GLA_DV]
        qe = q * jnp.exp(b)
        att = jnp.where(causal, _dot_nt(qe, k * jnp.exp(-b)), 0.0)
        s = state[h]
        o = _dot(qe, s) + _dot(att, v)
        b_last = b[chunk - 1:chunk, :]
        state[h] = s * _row_to_col(jnp.exp(b_last), eye_k) + _dot_tn(k * jnp.exp(b_last - b), v)
        o = o * lax.rsqrt(jnp.mean(o * o, axis=-1, keepdims=True) + EPS) * nw
        o_ref[:, h * GLA_DV:(h + 1) * GLA_DV] = o * _silu(og_ref[:, h * GLA_DV:(h + 1) * GLA_DV])

    @pl.when(c_idx == pl.num_programs(1) - 1)
    def _():
        s_out_ref[0] = state[...]


def _gla_mixer(proj, s0, wg, bg, nw, *, batch, seq_len, chunk):
    nc = -(-seq_len // chunk)
    rows = batch * nc * chunk
    row = lambda b, c: b * nc + c
    qkv_w = 2 * GLA_K + GLA_V
    return pl.pallas_call(
        functools.partial(_gla_kernel, chunk=chunk, seq_len=seq_len),
        out_shape=(jax.ShapeDtypeStruct((rows, GLA_V), F32),
                   jax.ShapeDtypeStruct((batch, GLA_HEADS, GLA_DK, GLA_DV), F32)),
        grid=(batch, nc),
        in_specs=[pl.BlockSpec((chunk, qkv_w), lambda b, c: (row(b, c), OD_QKV // qkv_w)),
                  pl.BlockSpec((chunk, GLA_V), lambda b, c: (row(b, c), OD_OG // GLA_V)),
                  pl.BlockSpec((chunk, SMALL_W), lambda b, c: (row(b, c), OD_GLR // SMALL_W)),
                  pl.BlockSpec((1, GLA_HEADS, GLA_DK, GLA_DV), lambda b, c: (b, 0, 0, 0)),
                  pl.BlockSpec((SMALL_W, GLA_K), lambda b, c: (0, 0)),
                  pl.BlockSpec((1, GLA_K), lambda b, c: (0, 0)),
                  pl.BlockSpec((1, GLA_DV), lambda b, c: (0, 0))],
        out_specs=(pl.BlockSpec((chunk, GLA_V), lambda b, c: (row(b, c), 0)),
                   pl.BlockSpec((1, GLA_HEADS, GLA_DK, GLA_DV), lambda b, c: (b, 0, 0, 0))),
        scratch_shapes=[pltpu.VMEM((GLA_HEADS, GLA_DK, GLA_DV), F32)],
        compiler_params=_params(("parallel", "arbitrary")),
        name="gla_mixer",
    )(proj, proj, proj, s0, wg, bg, nw)


def _take_top(x, key, count):
    nr, nt = x.shape
    big = jnp.int32(2 ** 30)
    slot = lax.broadcasted_iota(jnp.int32, (count, nt), 0)

    def body(r, carry):
        x, rank, vals = carry
        m = jnp.max(x, axis=0, keepdims=True)
        first = jnp.min(jnp.where(x == m, key, big), axis=0, keepdims=True)
        hit = key == first
        return (jnp.where(hit, -jnp.inf, x), jnp.where(hit, r, rank), jnp.where(slot == r, m, vals))

    init = (x, jnp.full((nr, nt), count, jnp.int32), jnp.zeros((count, nt), F32))
    _, rank, vals = lax.fori_loop(0, count, body, init)
    return vals, rank


def _pair_candidates(v1, v2):
    k = PEER_TOPK
    nt = v1.shape[1]
    j8 = lax.broadcasted_iota(jnp.int32, (8, nt), 0)
    j16 = lax.broadcasted_iota(jnp.int32, (k, nt), 0)
    sums, pos = [v1[0:1, :] + v2], [j16]
    for i in range(1, 8):
        sums.append(jnp.where(j8 < k // (i + 1), v1[i:i + 1, :] + v2[0:8, :], -jnp.inf))
        pos.append(i * k + j8)
    sums.append(v1[8:k, :] + v2[0:1, :])
    pos.append((8 + j8) * k)
    return jnp.concatenate(sums, axis=0), jnp.concatenate(pos, axis=0)


def _take_top_values(xs, count):
    nt = xs[0].shape[1]
    slot = lax.broadcasted_iota(jnp.int32, (count, nt), 0)

    def body(r, carry):
        out = []
        for x, vals in carry:
            m = jnp.max(x, axis=0, keepdims=True)
            out.append((jnp.where(x == m, -jnp.inf, x), jnp.where(slot == r, m, vals)))
        return tuple(out)

    init = tuple((x, jnp.zeros((count, nt), F32)) for x in xs)
    return [vals for _, vals in lax.fori_loop(0, count, body, init)]


def _count_rows(mask):
    return jnp.sum(jnp.where(mask, 1.0, 0.0), axis=0, keepdims=True)


def _per_first_key(sel):
    k = PEER_TOPK
    cnt = jnp.where(sel, 1.0, 0.0)
    out = [jnp.sum(cnt[0:k, :], axis=0, keepdims=True)]
    out += [jnp.sum(cnt[k + 8 * (i - 1):k + 8 * i, :], axis=0, keepdims=True) for i in range(1, 8)]
    out += [cnt[k + 56 + i:k + 57 + i, :] for i in range(k - 8)]
    return out


def _route_tables_exact(s1, s2):
    k = PEER_TOPK
    key = lax.broadcasted_iota(jnp.int32, s1.shape, 0)
    v1, rank1 = _take_top(s1, key, k)
    v2, rank2 = _take_top(s2, key, k)
    cand, pos = _pair_candidates(v1, v2)
    _, crank = _take_top(cand, pos, k)
    sel = crank < k
    z = jnp.sum(jnp.where(sel, jnp.exp(cand - (v1[0:1, :] + v2[0:1, :])), 0.0), axis=0, keepdims=True)
    n = jnp.zeros(s1.shape, F32)
    for i, n_i in enumerate(_per_first_key(sel)):
        n = jnp.where(rank1 == i, n_i, n)
    e1 = jnp.where(rank1 < k, jnp.exp(s1 - v1[0:1, :]) / z, 0.0)
    e2 = jnp.where(rank2 < k, jnp.exp(s2 - v2[0:1, :]), 0.0)
    return n, e1, rank2.astype(F32), e2


def _route_tables_distinct(s1, s2):
    k = PEER_TOPK
    v1, v2 = _take_top_values([s1, s2], k)
    in1 = s1 >= v1[k - 1:k, :]
    in2 = s2 >= v2[k - 1:k, :]
    cand, _ = _pair_candidates(v1, v2)
    cv, = _take_top_values([cand], k)
    sel = cand >= cv[k - 1:k, :]
    ok = (_count_rows(in1) == k) & (_count_rows(in2) == k) & (_count_rows(sel) == k)
    z = jnp.sum(jnp.where(sel, jnp.exp(cand - (v1[0:1, :] + v2[0:1, :])), 0.0), axis=0, keepdims=True)
    n = jnp.zeros(s1.shape, F32)
    rank2 = jnp.full(s2.shape, float(k), F32)
    for i, n_i in enumerate(_per_first_key(sel)):
        n = jnp.where(s1 == v1[i:i + 1, :], n_i, n)
        rank2 = jnp.where(s2 == v2[i:i + 1, :], float(i), rank2)
    e1 = jnp.where(in1, jnp.exp(s1 - v1[0:1, :]) / z, 0.0)
    e2 = jnp.where(in2, jnp.exp(s2 - v2[0:1, :]), 0.0)
    return (n, e1, rank2, e2), ok


def _peer_route_kernel(q_ref, k1_ref, k2_ref, n_ref, e1_ref, r2_ref, e2_ref, s1_ref, s2_ref):
    q = q_ref[...]
    s1_ref[...] = _dot_nt(k1_ref[0], q[:, :PEER_HALF])
    s2_ref[...] = _dot_nt(k2_ref[0], q[:, PEER_HALF:])
    tt = q.shape[0]
    lw = min(LANES, tt)
    for lc in range(tt // lw):
        sl = slice(lc * lw, (lc + 1) * lw)

        def store(tables):
            n, e1, rank2, e2 = tables
            n_ref[0, :, sl] = n
            e1_ref[0, :, sl] = e1
            r2_ref[:, sl] = rank2.astype(BF16)
            e2_ref[:, sl] = e2.astype(BF16)

        tables, ok = _route_tables_distinct(s1_ref[:, sl], s2_ref[:, sl])
        store(tables)

        @pl.when(jnp.min(jnp.where(ok, 1, 0)) == 0)
        def _():
            store(_route_tables_exact(s1_ref[:, sl], s2_ref[:, sl]))


def _peer_route(q, keys1, keys2, tt):
    t = q.shape[0]
    out_a = jax.ShapeDtypeStruct((PEER_HEADS, PEER_NKEYS, t), F32)
    out_b = jax.ShapeDtypeStruct((PEER_HEADS * PEER_NKEYS, t), BF16)
    spec_a = pl.BlockSpec((1, PEER_NKEYS, tt), lambda i, h: (h, 0, i))
    spec_b = pl.BlockSpec((PEER_NKEYS, tt), lambda i, h: (h, i))
    key_spec = pl.BlockSpec((1, PEER_NKEYS, PEER_HALF), lambda i, h: (h, 0, 0))
    return pl.pallas_call(
        _peer_route_kernel,
        out_shape=(out_a, out_a, out_b, out_b),
        grid=(t // tt, PEER_HEADS),
        in_specs=[pl.BlockSpec((tt, PEER_DQ), lambda i, h: (i, h)), key_spec, key_spec],
        out_specs=(spec_a, spec_a, spec_b, spec_b),
        scratch_shapes=[pltpu.VMEM((PEER_NKEYS, tt), F32), pltpu.VMEM((PEER_NKEYS, tt), F32)],
        compiler_params=_params(("parallel", "arbitrary")),
        name="peer_route",
    )(q, keys1, keys2)


def _peer_dense_kernel(xa_ref, xc_ref, lnw_ref, n_ref, e1_ref, r2_in_ref, e2_in_ref, u_ref, v_ref, o_ref,
                       h_ref, pre_ref, pre2_ref, act_ref, act2_ref, acc_ref, r2_ref, e2_ref,
                       *, rows_per_step, n_blocks, n_work):
    g = pl.program_id(0)
    last = n_work - 1
    ja = jnp.minimum(g, last) % n_blocks
    jb = jnp.clip(g - 1, 0, last) % n_blocks
    jc = jnp.clip(g - 2, 0, last) % n_blocks

    @pl.when(g == 0)
    def _():
        for ref in (pre_ref, pre2_ref, act_ref, act2_ref):
            ref[...] = jnp.zeros_like(ref)

    @pl.when(ja == 0)
    def _():
        x = xa_ref[...]
        y = x * lax.rsqrt(jnp.mean(x * x, axis=-1, keepdims=True) + EPS)
        h_ref[...] = (y * lnw_ref[...]).astype(BF16)

    @pl.when(jb == 0)
    def _():
        r2_ref[...] = r2_in_ref[...]
        e2_ref[...] = e2_in_ref[...]

    @pl.when(jc == 0)
    def _():
        acc_ref[...] = jnp.zeros_like(acc_ref)

    tt = h_ref.shape[0]
    lw = min(LANES, tt)
    reps = PEER_NKEYS // BF16_ROWS

    def stages(pre_w, pre_r, act_w, act_r):
        zero = jnp.zeros((PEER_NKEYS, lw), BF16)
        mxu_width = min(256, tt)

        def stage_a(part):
            tok = slice(part * mxu_width, (part + 1) * mxu_width)
            pre_w[:, tok] = lax.dot_general(u_ref[...], h_ref[tok, :], _NT, preferred_element_type=F32)

        def stage_b(lc, aa):
            sl = slice(lc * lw, (lc + 1) * lw)
            rows = slice(aa * PEER_NKEYS, (aa + 1) * PEER_NKEYS)
            gate = zero
            for h in range(PEER_HEADS):
                n_row = jnp.broadcast_to(n_ref[h, aa:aa + 1, sl], (BF16_ROWS, lw)).astype(BF16)
                e1_row = jnp.broadcast_to(e1_ref[h, aa:aa + 1, sl], (BF16_ROWS, lw)).astype(BF16)
                n_tile = jnp.concatenate([n_row] * reps, axis=0)
                e1_tile = jnp.concatenate([e1_row] * reps, axis=0)
                keys = slice(h * PEER_NKEYS, (h + 1) * PEER_NKEYS)
                gate = gate + jnp.where(r2_ref[keys, sl] < n_tile, e2_ref[keys, sl], zero) * e1_tile
            act = _gelu_tanh(pre_r[rows, sl].astype(BF16)) * gate
            act_w[sl, rows] = act.T

        def stage_c(part):
            col = slice(part * 256, (part + 1) * 256)
            acc_ref[:, col] += jnp.dot(act_r[...], v_ref[:, col], preferred_element_type=F32)

        mxu = [functools.partial(stage_a, p) for p in range(tt // mxu_width)]
        mxu += [functools.partial(stage_c, p) for p in range(acc_ref.shape[1] // 256)]
        vpu = [functools.partial(stage_b, lc, aa) for lc in range(tt // lw) for aa in range(rows_per_step)]
        per = -(-len(vpu) // len(mxu))
        for i, piece in enumerate(mxu):
            piece()
            for tile in vpu[i * per:(i + 1) * per]:
                tile()

    @pl.when(g % 2 == 0)
    def _():
        stages(pre_ref, pre2_ref, act2_ref, act_ref)

    @pl.when(g % 2 == 1)
    def _():
        stages(pre2_ref, pre_ref, act_ref, act2_ref)

    @pl.when((jc == n_blocks - 1) & (g >= 2))
    def _():
        o_ref[...] = xc_ref[...] + acc_ref[...]


def _peer_dense(x, lnw, route, u, v, tt, rows_per_step):
    t, d = x.shape
    eb = rows_per_step * PEER_NKEYS
    n_blocks = PEER_N // eb
    n_work = (t // tt) * n_blocks

    def item(lag):
        def at(g):
            w = jnp.clip(g - lag, 0, n_work - 1)
            return w // n_blocks, w % n_blocks
        return at

    a, b, c = item(0), item(1), item(2)
    spec_rows = pl.BlockSpec((PEER_HEADS, rows_per_step, tt), lambda g: (0, b(g)[1], b(g)[0]))
    spec_keys = pl.BlockSpec((PEER_HEADS * PEER_NKEYS, tt), lambda g: (0, b(g)[0]))
    return pl.pallas_call(
        functools.partial(_peer_dense_kernel, rows_per_step=rows_per_step, n_blocks=n_blocks, n_work=n_work),
        out_shape=jax.ShapeDtypeStruct((t, d), F32),
        grid=(n_work + 2,),
        in_specs=[pl.BlockSpec((tt, d), lambda g: (a(g)[0], 0)),
                  pl.BlockSpec((tt, d), lambda g: (c(g)[0], 0)),
                  pl.BlockSpec((1, d), lambda g: (0, 0)),
                  spec_rows, spec_rows, spec_keys, spec_keys,
                  pl.BlockSpec((eb, d), lambda g: (a(g)[1], 0)),
                  pl.BlockSpec((eb, d), lambda g: (c(g)[1], 0))],
        out_specs=pl.BlockSpec((tt, d), lambda g: (c(g)[0], 0)),
        scratch_shapes=[pltpu.VMEM((tt, d), BF16), pltpu.VMEM((eb, tt), F32), pltpu.VMEM((eb, tt), F32),
                        pltpu.VMEM((tt, eb), BF16), pltpu.VMEM((tt, eb), BF16), pltpu.VMEM((tt, d), F32),
                        pltpu.VMEM((PEER_HEADS * PEER_NKEYS, tt), BF16),
                        pltpu.VMEM((PEER_HEADS * PEER_NKEYS, tt), BF16)],
        compiler_params=_params(("arbitrary",)),
        name="peer_dense",
    )(x, x, lnw.reshape(1, d), *route, u, v)


def _peer_ffn(x, lnw, wq, keys1, keys2, u, v):
    tt = min(512, x.shape[0])
    q = _norm_matmul(x, lnw, wq)
    route = _peer_route(q, keys1, keys2, tt)
    return _peer_dense(x, lnw, route, u, v, tt, rows_per_step=8)


def _pad_lanes(vec, offset):
    return jnp.zeros((SMALL_W,), F32).at[offset:offset + vec.shape[0]].set(vec.astype(F32))


def _conv_history(buf):
    return jnp.pad(buf, ((0, 0), (CONV_PAD - (CONV_W - 1), 0), (0, 0)))


def kernel(x_prompt, x_sample, state_dn_conv, state_dn, state_ssm_conv, state_ssm, state_gla, ln_mix, ln_ffn, ln_final, w_in_even, w_out_even, dn_conv_w, dn_a_log, dn_dt_bias, dn_norm_w, ssm_conv_w, ssm_conv_b, ssm_a_log, ssm_dt_bias, ssm_d, ssm_norm_w, w_in_odd, gla_w_gate2, gla_b_gate, gla_norm_w, w_out_odd, peer_w_q, peer_keys1, peer_keys2, peer_u, peer_v):
    we = w_in_even[0]
    o = 0
    cuts = {}
    for name, size in (("qkv", DN_CONV_DIM), ("a", DN_HEADS), ("b", DN_HEADS), ("gate", DN_V),
                       ("z", SSM_INNER), ("xbc", SSM_CONV_DIM), ("dt", SSM_HEADS)):
        cuts[name] = we[:, o:o + size]
        o += size
    small_w = jnp.concatenate([cuts["a"], cuts["b"], cuts["dt"]], axis=1)
    small_w = jnp.pad(small_w, ((0, 0), (0, SMALL_W - small_w.shape[1])))
    w_even = jnp.concatenate([cuts["qkv"], cuts["gate"], cuts["xbc"], cuts["z"], small_w], axis=1).astype(BF16)
    wo = w_in_odd[0]
    qkv_w = 2 * GLA_K + GLA_V
    glr_w = jnp.pad(wo[:, qkv_w:qkv_w + GLA_LR], ((0, 0), (0, SMALL_W - GLA_LR)))
    w_odd = jnp.concatenate([wo[:, :qkv_w], wo[:, qkv_w + GLA_LR:], glr_w], axis=1).astype(BF16)
    w_out_dn = w_out_even[0, :DN_V].astype(BF16)
    w_out_ssm = w_out_even[0, DN_V:].astype(BF16)
    w_out_gla = w_out_odd[0].astype(BF16)
    wg = jnp.pad(gla_w_gate2[0], ((0, SMALL_W - GLA_LR), (0, 0))).astype(BF16)
    wq = peer_w_q.astype(BF16)
    k1 = peer_keys1.astype(BF16)
    k2 = peer_keys2.astype(BF16)
    pu = [peer_u[i].astype(BF16) for i in range(peer_u.shape[0])]
    pv = [peer_v[i].astype(BF16) for i in range(peer_v.shape[0])]

    zero8 = jnp.zeros((8 - 2, SMALL_W), F32)
    dn_par = jnp.concatenate([_pad_lanes(dn_a_log[0], 0)[None], _pad_lanes(dn_dt_bias[0], 0)[None], zero8])
    lane0 = 2 * DN_HEADS
    ssm_par = jnp.concatenate([_pad_lanes(ssm_a_log[0], lane0)[None], _pad_lanes(ssm_dt_bias[0], lane0)[None],
                               _pad_lanes(ssm_d[0], lane0)[None], zero8[1:]])

    def trunk(x, batch, seq_len, conv_dn0, s_dn0, conv_ssm0, s_ssm0, s_gla0):
        chunk = PROMPT_CHUNK if seq_len >= PROMPT_CHUNK else STEP_CHUNK
        nc = -(-seq_len // chunk)
        padded = nc * chunk != seq_len
        dn_nb = max(1, min(batch, 256 // (DN_HEADS * chunk)))
        while batch % dn_nb:
            dn_nb -= 1

        def to_seq(a):
            if not padded:
                return a
            a = a.reshape(batch, seq_len, a.shape[-1])
            return jnp.pad(a, ((0, 0), (0, nc * chunk - seq_len), (0, 0))).reshape(batch * nc * chunk, -1)

        def from_seq(a):
            if not padded:
                return a
            return a.reshape(batch, nc * chunk, a.shape[-1])[:, :seq_len].reshape(batch * seq_len, -1)

        def last_rows(buf, proj, col, width):
            tail = proj.reshape(batch, seq_len, proj.shape[-1])[:, -min(seq_len, CONV_W - 1):, col:col + width]
            return jnp.concatenate([buf, tail], axis=1)[:, -(CONV_W - 1):]

        seq = dict(batch=batch, seq_len=seq_len, chunk=chunk)

        proj = _norm_matmul(x, ln_mix[0], w_even)
        proj_seq = to_seq(proj)
        o_dn, s_dn = _dn_mixer(proj_seq.reshape(batch, nc * chunk, EV_WIDTH), _conv_history(conv_dn0[0]),
                               s_dn0[0], dn_conv_w[0], dn_par, dn_norm_w[0].reshape(1, DN_DV),
                               nb=dn_nb, **seq)
        o_dn = o_dn.reshape(batch * nc * chunk, DN_V)
        y_ssm, s_ssm = _ssd_mixer(proj_seq, _conv_history(conv_ssm0[0]), s_ssm0[0], ssm_conv_w[0],
                                  ssm_conv_b[0].reshape(1, SSM_CONV_DIM), ssm_par,
                                  ssm_norm_w[0].reshape(1, SSM_INNER), **seq)
        conv_dn = last_rows(conv_dn0[0], proj, EV_QKV, DN_CONV_DIM)
        conv_ssm = last_rows(conv_ssm0[0], proj, EV_XBC, SSM_CONV_DIM)
        x = _proj_residual(x, [from_seq(o_dn), from_seq(y_ssm)], [w_out_dn, w_out_ssm])
        x = _peer_ffn(x, ln_ffn[0], wq[0], k1[0], k2[0], pu[0], pv[0])

        proj = _norm_matmul(x, ln_mix[1], w_odd)
        o_gla, s_gla = _gla_mixer(to_seq(proj), s_gla0[0], wg, gla_b_gate[0].reshape(1, GLA_K),
                                  gla_norm_w[0].reshape(1, GLA_DV), **seq)
        x = _proj_residual(x, [from_seq(o_gla)], [w_out_gla])
        x = _peer_ffn(x, ln_ffn[1], wq[1], k1[1], k2[1], pu[1], pv[1])

        y = _final_norm(x, ln_final).reshape(batch, seq_len, D_MODEL)
        return y, conv_dn[None], s_dn[None], conv_ssm[None], s_ssm[None], s_gla[None]

    bp, lp = x_prompt.shape[:2]
    bs, ls = x_sample.shape[:2]

    def zeros_like_state(s):
        return jnp.zeros((s.shape[0], bp) + s.shape[2:], s.dtype)

    prompt = trunk(x_prompt.reshape(bp * lp, D_MODEL), bp, lp, zeros_like_state(state_dn_conv),
                   zeros_like_state(state_dn), zeros_like_state(state_ssm_conv),
                   zeros_like_state(state_ssm), zeros_like_state(state_gla))
    sample = trunk(x_sample.reshape(bs * ls, D_MODEL), bs, ls, state_dn_conv, state_dn,
                   state_ssm_conv, state_ssm, state_gla)
    return (prompt[0], sample[0]) + prompt[1:] + sample[1:]
```

```python
import functools

import jax
import jax.numpy as jnp
from jax import lax
from jax.experimental import pallas as pl
from jax.experimental.pallas import tpu as pltpu

F32 = jnp.float32
BF16 = jnp.bfloat16
EPS = 1e-6

D_MODEL = 1024
CONV_W = 4
CONV_PAD = 8
PROMPT_CHUNK = 64
STEP_CHUNK = 8

DN_HEADS, DN_DK, DN_DV = 4, 128, 128
DN_QK = DN_HEADS * DN_DK
DN_V = DN_HEADS * DN_DV
DN_CONV_DIM = 2 * DN_QK + DN_V
SSM_HEADS, SSM_P, SSM_GROUPS, SSM_N = 8, 64, 2, 128
SSM_INNER = SSM_HEADS * SSM_P
SSM_CONV_DIM = SSM_INNER + 2 * SSM_GROUPS * SSM_N
GLA_HEADS, GLA_DK, GLA_DV, GLA_LR = 4, 128, 256, 16
GLA_K = GLA_HEADS * GLA_DK
GLA_V = GLA_HEADS * GLA_DV
GLA_NORMALIZER = 16.0
PEER_HEADS, PEER_DQ, PEER_NKEYS, PEER_TOPK = 8, 256, 128, 16
PEER_HALF = PEER_DQ // 2
PEER_N = PEER_NKEYS * PEER_NKEYS
LANES = 128
BF16_ROWS = 16
SMALL_W = LANES

EV_QKV, EV_GATE, EV_XBC, EV_Z, EV_SMALL = 0, 1536, 2048, 3072, 3584
EV_WIDTH = EV_SMALL + SMALL_W
OD_QKV, OD_OG, OD_GLR = 0, 2048, 3072
OD_WIDTH = OD_GLR + SMALL_W

VMEM_LIMIT = 56 * 1024 * 1024

_NT = (((1,), (1,)), ((), ()))
_TN = (((0,), (0,)), ((), ()))


def _dot(a, b):
    return jnp.dot(a.astype(BF16), b.astype(BF16), preferred_element_type=F32)


def _dot_nt(a, b):
    return lax.dot_general(a.astype(BF16), b.astype(BF16), _NT, preferred_element_type=F32)


def _dot_tn(a, b):
    return lax.dot_general(a.astype(BF16), b.astype(BF16), _TN, preferred_element_type=F32)


def _params(sem):
    return pltpu.CompilerParams(dimension_semantics=sem, vmem_limit_bytes=VMEM_LIMIT)


def _silu(x):
    return x * jax.nn.sigmoid(x)


def _gelu_tanh(x):
    c0 = 0.7978845608028654
    hx = 0.5 * x
    return hx + hx * jnp.tanh(x * (c0 + (c0 * 0.044715) * (x * x)))


def _softplus(x):
    return jnp.maximum(x, 0.0) + jnp.log(1.0 + jnp.exp(-jnp.abs(x)))


def _cumsum_rows(x):
    n = x.shape[0]
    row = lax.broadcasted_iota(jnp.int32, x.shape, 0)
    s = 1
    while s < n:
        x = x + jnp.where(row >= s, pltpu.roll(x, s, axis=0), 0.0)
        s *= 2
    return x


def _col_to_row(col, eye):
    return jnp.sum(jnp.where(eye, col, 0.0), axis=0, keepdims=True)


def _row_to_col(row, eye):
    return jnp.sum(jnp.where(eye, row, 0.0), axis=1, keepdims=True)


def _tri_masks(n):
    ri = lax.broadcasted_iota(jnp.int32, (n, n), 0)
    ci = lax.broadcasted_iota(jnp.int32, (n, n), 1)
    return ri, ci, ri == ci, ri >= ci, ri > ci


def _valid_rows(chunk, seq_len, shape):
    if seq_len % chunk == 0:
        return None
    pos = pl.program_id(1) * chunk + lax.broadcasted_iota(jnp.int32, shape, 0)
    return pos < seq_len


def _causal_conv(x_ref, xext, cw_ref, chunk):
    x = x_ref[...]
    xext[CONV_PAD:CONV_PAD + chunk, :] = x
    cw = cw_ref[...]
    pre = cw[CONV_W - 1:CONV_W, :] * x
    for j in range(1, CONV_W):
        pre = pre + cw[CONV_W - 1 - j:CONV_W - j, :] * xext[CONV_PAD - j:CONV_PAD - j + chunk, :]
    xext[0:CONV_PAD, :] = xext[chunk:chunk + CONV_PAD, :]
    return pre


def _norm_matmul_kernel(x_ref, lnw_ref, w_ref, o_ref):
    x = x_ref[...]
    y = x * lax.rsqrt(jnp.mean(x * x, axis=-1, keepdims=True) + EPS)
    h = (y * lnw_ref[...]).astype(BF16)
    o_ref[...] = jnp.dot(h, w_ref[...], preferred_element_type=F32)


def _norm_matmul(x, lnw, w):
    t, d = x.shape
    n = w.shape[1]
    tm = min(512, t)
    return pl.pallas_call(
        _norm_matmul_kernel,
        out_shape=jax.ShapeDtypeStruct((t, n), F32),
        grid=(t // tm,),
        in_specs=[pl.BlockSpec((tm, d), lambda i: (i, 0)),
                  pl.BlockSpec((1, d), lambda i: (0, 0)),
                  pl.BlockSpec((d, n), lambda i: (0, 0))],
        out_specs=pl.BlockSpec((tm, n), lambda i: (i, 0)),
        compiler_params=_params(("parallel",)),
        name="norm_matmul",
    )(x, lnw.reshape(1, d), w)


def _proj_residual_kernel(*refs, n_in):
    x_ref, o_ref = refs[0], refs[-1]
    acc = x_ref[...]
    for a_ref, w_ref in zip(refs[1:1 + n_in], refs[1 + n_in:1 + 2 * n_in]):
        acc = acc + jnp.dot(a_ref[...].astype(BF16), w_ref[...], preferred_element_type=F32)
    o_ref[...] = acc


def _proj_residual(x, acts, weights):
    t, d = x.shape
    tm = min(512, t)
    n_in = len(acts)
    in_specs = [pl.BlockSpec((tm, d), lambda i: (i, 0))]
    in_specs += [pl.BlockSpec((tm, a.shape[1]), lambda i: (i, 0)) for a in acts]
    in_specs += [pl.BlockSpec(w.shape, lambda i: (0, 0)) for w in weights]
    return pl.pallas_call(
        functools.partial(_proj_residual_kernel, n_in=n_in),
        out_shape=jax.ShapeDtypeStruct((t, d), F32),
        grid=(t // tm,),
        in_specs=in_specs,
        out_specs=pl.BlockSpec((tm, d), lambda i: (i, 0)),
        compiler_params=_params(("parallel",)),
        name="proj_residual",
    )(x, *acts, *weights)


def _final_norm_kernel(x_ref, w_ref, o_ref):
    x = x_ref[...]
    o_ref[...] = x * lax.rsqrt(jnp.mean(x * x, axis=-1, keepdims=True) + EPS) * w_ref[...]


def _final_norm(x, w):
    t, d = x.shape
    tm = min(512, t)
    return pl.pallas_call(
        _final_norm_kernel,
        out_shape=jax.ShapeDtypeStruct((t, d), F32),
        grid=(t // tm,),
        in_specs=[pl.BlockSpec((tm, d), lambda i: (i, 0)), pl.BlockSpec((1, d), lambda i: (0, 0))],
        out_specs=pl.BlockSpec((tm, d), lambda i: (i, 0)),
        compiler_params=_params(("parallel",)),
        name="final_norm",
    )(x, w.reshape(1, d))


def _unit_lower_inverse(m, ri, ci, eye, block):
    t = jnp.where(eye, 1.0, 0.0)
    level = 0
    while (1 << level) < block:
        pair = ((ri >> (level + 1)) == (ci >> (level + 1))) & ((ri >> level) != (ci >> level))
        c = jnp.where(pair, m, 0.0)
        t = t - c if level == 0 else t - _dot(_dot(t, c), t)
        level += 1
    return t


def _dn_kernel(qkv_ref, gate_ref, small_ref, conv0_ref, s0_ref, cw_ref, par_ref, nw_ref,
               o_ref, s_out_ref, xext, state, *, chunk, seq_len, nb):
    c_idx = pl.program_id(1)
    groups = nb * DN_HEADS
    log2c = chunk.bit_length() - 1
    group_lanes = lambda g: slice(g * DN_DV, (g + 1) * DN_DV)

    @pl.when(c_idx == 0)
    def _():
        for s in range(nb):
            xext[s, 0:CONV_PAD, :] = conv0_ref[s]
            for h in range(DN_HEADS):
                state[:, group_lanes(s * DN_HEADS + h)] = s0_ref[s, h]

    par = par_ref[...]
    qs, ks, vs, gcs, betas, gates = [], [], [], [], [], []
    for s in range(nb):
        act = _silu(_causal_conv(qkv_ref.at[s], xext.at[s], cw_ref, chunk))
        small = small_ref[s]
        g_all = -jnp.exp(par[0:1, :]) * _softplus(small + par[1:2, :])
        beta_all = jax.nn.sigmoid(small)
        valid = _valid_rows(chunk, seq_len, small.shape)
        if valid is not None:
            g_all = jnp.where(valid, g_all, 0.0)
            beta_all = jnp.where(valid, beta_all, 0.0)
        gc_all = _cumsum_rows(g_all)
        for h in range(DN_HEADS):
            xq = act[:, h * DN_DK:(h + 1) * DN_DK]
            xk = act[:, DN_QK + h * DN_DK:DN_QK + (h + 1) * DN_DK]
            qs.append(xq * lax.rsqrt(jnp.sum(xq * xq, axis=-1, keepdims=True) + EPS) * (DN_DK ** -0.5))
            ks.append(xk * lax.rsqrt(jnp.sum(xk * xk, axis=-1, keepdims=True) + EPS))
            vs.append(act[:, 2 * DN_QK + h * DN_DV:2 * DN_QK + (h + 1) * DN_DV])
            gcs.append(gc_all[:, h:h + 1])
            betas.append(beta_all[:, DN_HEADS + h:DN_HEADS + h + 1])
            gates.append(gate_ref[s, :, h * DN_DV:(h + 1) * DN_DV])
    q, k, v = (jnp.concatenate(t, axis=0) for t in (qs, ks, vs))
    gc, beta = (jnp.concatenate(t, axis=0) for t in (gcs, betas))
    gc_last = jnp.concatenate([jnp.broadcast_to(t[chunk - 1:chunk, :], (chunk, 1)) for t in gcs], axis=0)
    last_decay = jnp.concatenate(
        [jnp.broadcast_to(jnp.exp(t[chunk - 1:chunk, :]), (1, DN_DV)) for t in gcs], axis=1)

    rows = groups * chunk
    ri, ci, eye, lower, strict_lower = _tri_masks(rows)
    same = (ri >> log2c) == (ci >> log2c)
    causal = same & lower
    strict = same & strict_lower
    decay = jnp.where(causal, jnp.exp(gc - _col_to_row(gc, eye)), 0.0)
    kb = k * beta
    eg = jnp.exp(gc)
    m = jnp.where(strict, _dot_nt(kb, k) * decay, 0.0)
    t_inv = _unit_lower_inverse(m, ri, ci, eye, chunk)
    sol = _dot(t_inv, jnp.concatenate([v * beta, kb * eg], axis=1))
    u, w = sol[:, :DN_DV], sol[:, DN_DV:]
    qk = jnp.where(causal, _dot_nt(q, k) * decay, 0.0)

    def own_block(x):
        return jnp.concatenate(
            [x[g * chunk:(g + 1) * chunk, group_lanes(g)] for g in range(groups)], axis=0)

    s_all = state[...]
    v_new = u - own_block(_dot(w, s_all))
    o = own_block(_dot(q * eg, s_all)) + _dot(qk, v_new)
    wide = (rows, groups * DN_DV)
    row_group = lax.broadcasted_iota(jnp.int32, wide, 0) >> log2c
    lane_group = lax.broadcasted_iota(jnp.int32, wide, 1) // DN_DV
    v_wide = jnp.where(row_group == lane_group, jnp.concatenate([v_new] * groups, axis=1), 0.0)
    state[...] = s_all * last_decay + _dot_tn(k * jnp.exp(gc_last - gc), v_wide)

    o = o * lax.rsqrt(jnp.mean(o * o, axis=-1, keepdims=True) + EPS) * nw_ref[...]
    y = o * _silu(jnp.concatenate(gates, axis=0))
    for s in range(nb):
        for h in range(DN_HEADS):
            g = s * DN_HEADS + h
            o_ref[s, :, h * DN_DV:(h + 1) * DN_DV] = y[g * chunk:(g + 1) * chunk, :]

    @pl.when(c_idx == pl.num_programs(1) - 1)
    def _():
        for s in range(nb):
            for h in range(DN_HEADS):
                s_out_ref[s, h] = state[:, group_lanes(s * DN_HEADS + h)]


def _dn_mixer(proj, conv0, s0, cw, par, nw, *, batch, seq_len, chunk, nb):
    nc = proj.shape[1] // chunk
    return pl.pallas_call(
        functools.partial(_dn_kernel, chunk=chunk, seq_len=seq_len, nb=nb),
        out_shape=(jax.ShapeDtypeStruct((batch, nc * chunk, DN_V), F32),
                   jax.ShapeDtypeStruct((batch, DN_HEADS, DN_DK, DN_DV), F32)),
        grid=(batch // nb, nc),
        in_specs=[pl.BlockSpec((nb, chunk, DN_CONV_DIM), lambda b, c: (b, c, EV_QKV // DN_CONV_DIM)),
                  pl.BlockSpec((nb, chunk, DN_V), lambda b, c: (b, c, EV_GATE // DN_V)),
                  pl.BlockSpec((nb, chunk, SMALL_W), lambda b, c: (b, c, EV_SMALL // SMALL_W)),
                  pl.BlockSpec((nb, CONV_PAD, DN_CONV_DIM), lambda b, c: (b, 0, 0)),
                  pl.BlockSpec((nb, DN_HEADS, DN_DK, DN_DV), lambda b, c: (b, 0, 0, 0)),
                  pl.BlockSpec((CONV_W, DN_CONV_DIM), lambda b, c: (0, 0)),
                  pl.BlockSpec((8, SMALL_W), lambda b, c: (0, 0)),
                  pl.BlockSpec((1, DN_DV), lambda b, c: (0, 0))],
        out_specs=(pl.BlockSpec((nb, chunk, DN_V), lambda b, c: (b, c, 0)),
                   pl.BlockSpec((nb, DN_HEADS, DN_DK, DN_DV), lambda b, c: (b, 0, 0, 0))),
        scratch_shapes=[pltpu.VMEM((nb, chunk + CONV_PAD, DN_CONV_DIM), F32),
                        pltpu.VMEM((DN_DK, nb * DN_HEADS * DN_DV), F32)],
        compiler_params=_params(("parallel", "arbitrary")),
        name="dn_mixer",
    )(proj, proj, proj, conv0, s0, cw, par, nw)


def _ssd_kernel(xbc_ref, z_ref, small_ref, conv0_ref, s0_ref, cw_ref, cb_ref, par_ref, nw_ref,
                y_ref, s_out_ref, xext, state, ybuf, *, chunk, seq_len):
    c_idx = pl.program_id(1)

    @pl.when(c_idx == 0)
    def _():
        xext[0:CONV_PAD, :] = conv0_ref[0]
        state[...] = s0_ref[0]

    xbc = _silu(_causal_conv(xbc_ref, xext, cw_ref, chunk) + cb_ref[...])
    small = small_ref[...]
    par = par_ref[...]
    dt_all = _softplus(small + par[1:2, :])
    valid = _valid_rows(chunk, seq_len, small.shape)
    if valid is not None:
        dt_all = jnp.where(valid, dt_all, 0.0)
    ac_all = _cumsum_rows(dt_all * -jnp.exp(par[0:1, :]))

    ri, ci, eye, causal, strict = _tri_masks(chunk)
    gw = SSM_GROUPS * SSM_N
    heads_per_group = SSM_HEADS // SSM_GROUPS
    for g in range(SSM_GROUPS):
        bm = xbc[:, SSM_INNER + g * SSM_N:SSM_INNER + (g + 1) * SSM_N]
        cm = xbc[:, SSM_INNER + gw + g * SSM_N:SSM_INNER + gw + (g + 1) * SSM_N]
        cb = _dot_nt(cm, bm)
        for hh in range(heads_per_group):
            h = g * heads_per_group + hh
            lane = 2 * DN_HEADS + h
            ac = ac_all[:, lane:lane + 1]
            dt = dt_all[:, lane:lane + 1]
            decay = jnp.where(causal, jnp.exp(ac - _col_to_row(ac, eye)), 0.0)
            xh = xbc[:, h * SSM_P:(h + 1) * SSM_P]
            xdt = xh * dt
            s = state[h]
            y = _dot(cb * decay, xdt) + _dot_nt(cm * jnp.exp(ac), s)
            ac_last = ac[chunk - 1:chunk, :]
            state[h] = s * jnp.exp(ac_last) + _dot_tn(xdt, bm * jnp.exp(ac_last - ac))
            ybuf[:, h * SSM_P:(h + 1) * SSM_P] = y + par[2:3, lane:lane + 1] * xh

    y = ybuf[...] * _silu(z_ref[...])
    nw = nw_ref[...]
    gn = SSM_INNER // SSM_GROUPS
    for g in range(SSM_GROUPS):
        yg = y[:, g * gn:(g + 1) * gn]
        yg = yg * lax.rsqrt(jnp.mean(yg * yg, axis=-1, keepdims=True) + EPS)
        y_ref[:, g * gn:(g + 1) * gn] = yg * nw[:, g * gn:(g + 1) * gn]

    @pl.when(c_idx == pl.num_programs(1) - 1)
    def _():
        s_out_ref[0] = state[...]


def _ssd_mixer(proj, conv0, s0, cw, cb, par, nw, *, batch, seq_len, chunk):
    nc = -(-seq_len // chunk)
    rows = batch * nc * chunk
    row = lambda b, c: b * nc + c
    return pl.pallas_call(
        functools.partial(_ssd_kernel, chunk=chunk, seq_len=seq_len),
        out_shape=(jax.ShapeDtypeStruct((rows, SSM_INNER), F32),
                   jax.ShapeDtypeStruct((batch, SSM_HEADS, SSM_P, SSM_N), F32)),
        grid=(batch, nc),
        in_specs=[pl.BlockSpec((chunk, SSM_CONV_DIM), lambda b, c: (row(b, c), EV_XBC // SSM_CONV_DIM)),
                  pl.BlockSpec((chunk, SSM_INNER), lambda b, c: (row(b, c), EV_Z // SSM_INNER)),
                  pl.BlockSpec((chunk, SMALL_W), lambda b, c: (row(b, c), EV_SMALL // SMALL_W)),
                  pl.BlockSpec((1, CONV_PAD, SSM_CONV_DIM), lambda b, c: (b, 0, 0)),
                  pl.BlockSpec((1, SSM_HEADS, SSM_P, SSM_N), lambda b, c: (b, 0, 0, 0)),
                  pl.BlockSpec((CONV_W, SSM_CONV_DIM), lambda b, c: (0, 0)),
                  pl.BlockSpec((1, SSM_CONV_DIM), lambda b, c: (0, 0)),
                  pl.BlockSpec((8, SMALL_W), lambda b, c: (0, 0)),
                  pl.BlockSpec((1, SSM_INNER), lambda b, c: (0, 0))],
        out_specs=(pl.BlockSpec((chunk, SSM_INNER), lambda b, c: (row(b, c), 0)),
                   pl.BlockSpec((1, SSM_HEADS, SSM_P, SSM_N), lambda b, c: (b, 0, 0, 0))),
        scratch_shapes=[pltpu.VMEM((chunk + CONV_PAD, SSM_CONV_DIM), F32),
                        pltpu.VMEM((SSM_HEADS, SSM_P, SSM_N), F32),
                        pltpu.VMEM((chunk, SSM_INNER), F32)],
        compiler_params=_params(("parallel", "arbitrary")),
        name="ssd_mixer",
    )(proj, proj, proj, conv0, s0, cw, cb, par, nw)


def _gla_kernel(qkv_ref, og_ref, glr_ref, s0_ref, wg_ref, bg_ref, nw_ref,
                o_ref, s_out_ref, state, *, chunk, seq_len):
    c_idx = pl.program_id(1)

    @pl.when(c_idx == 0)
    def _():
        state[...] = s0_ref[0]

    x = bg_ref[...] + jnp.dot(glr_ref[...].astype(BF16), wg_ref[...], preferred_element_type=F32)
    gk = (jnp.minimum(x, 0.0) - jnp.log(1.0 + jnp.exp(-jnp.abs(x)))) / GLA_NORMALIZER
    valid = _valid_rows(chunk, seq_len, gk.shape)
    if valid is not None:
        gk = jnp.where(valid, gk, 0.0)
    b_all = _cumsum_rows(gk)

    ri, ci, eye_c, causal, strict = _tri_masks(chunk)
    eye_k = _tri_masks(GLA_DK)[2]
    nw = nw_ref[...]
    for h in range(GLA_HEADS):
        b = b_all[:, h * GLA_DK:(h + 1) * GLA_DK]
        q = qkv_ref[:, h * GLA_DK:(h + 1) * GLA_DK] * (GLA_DK ** -0.5)
        k = qkv_ref[:, GLA_K + h * GLA_DK:GLA_K + (h + 1) * GLA_DK]
        v = qkv_ref[:, 2 * GLA_K + h * GLA_DV:2 * GLA_K + (h + 1) * GLA_DV]
        qe = q * jnp.exp(b)
        att = jnp.where(causal, _dot_nt(qe, k * jnp.exp(-b)), 0.0)
        s = state[h]
        o = _dot(qe, s) + _dot(att, v)
        b_last = b[chunk - 1:chunk, :]
        state[h] = s * _row_to_col(jnp.exp(b_last), eye_k) + _dot_tn(k * jnp.exp(b_last - b), v)
        o = o * lax.rsqrt(jnp.mean(o * o, axis=-1, keepdims=True) + EPS) * nw
        o_ref[:, h * GLA_DV:(h + 1) * GLA_DV] = o * _silu(og_ref[:, h * GLA_DV:(h + 1) * GLA_DV])

    @pl.when(c_idx == pl.num_programs(1) - 1)
    def _():
        s_out_ref[0] = state[...]


def _gla_mixer(proj, s0, wg, bg, nw, *, batch, seq_len, chunk):
    nc = -(-seq_len // chunk)
    rows = batch * nc * chunk
    row = lambda b, c: b * nc + c
    qkv_w = 2 * GLA_K + GLA_V
    return pl.pallas_call(
        functools.partial(_gla_kernel, chunk=chunk, seq_len=seq_len),
        out_shape=(jax.ShapeDtypeStruct((rows, GLA_V), F32),
                   jax.ShapeDtypeStruct((batch, GLA_HEADS, GLA_DK, GLA_DV), F32)),
        grid=(batch, nc),
        in_specs=[pl.BlockSpec((chunk, qkv_w), lambda b, c: (row(b, c), OD_QKV // qkv_w)),
                  pl.BlockSpec((chunk, GLA_V), lambda b, c: (row(b, c), OD_OG // GLA_V)),
                  pl.BlockSpec((chunk, SMALL_W), lambda b, c: (row(b, c), OD_GLR // SMALL_W)),
                  pl.BlockSpec((1, GLA_HEADS, GLA_DK, GLA_DV), lambda b, c: (b, 0, 0, 0)),
                  pl.BlockSpec((SMALL_W, GLA_K), lambda b, c: (0, 0)),
                  pl.BlockSpec((1, GLA_K), lambda b, c: (0, 0)),
                  pl.BlockSpec((1, GLA_DV), lambda b, c: (0, 0))],
        out_specs=(pl.BlockSpec((chunk, GLA_V), lambda b, c: (row(b, c), 0)),
                   pl.BlockSpec((1, GLA_HEADS, GLA_DK, GLA_DV), lambda b, c: (b, 0, 0, 0))),
        scratch_shapes=[pltpu.VMEM((GLA_HEADS, GLA_DK, GLA_DV), F32)],
        compiler_params=_params(("parallel", "arbitrary")),
        name="gla_mixer",
    )(proj, proj, proj, s0, wg, bg, nw)


def _take_top(x, key, count):
    nr, nt = x.shape
    big = jnp.int32(2 ** 30)
    slot = lax.broadcasted_iota(jnp.int32, (count, nt), 0)

    def body(r, carry):
        x, rank, vals = carry
        m = jnp.max(x, axis=0, keepdims=True)
        first = jnp.min(jnp.where(x == m, key, big), axis=0, keepdims=True)
        hit = key == first
        return (jnp.where(hit, -jnp.inf, x), jnp.where(hit, r, rank), jnp.where(slot == r, m, vals))

    init = (x, jnp.full((nr, nt), count, jnp.int32), jnp.zeros((count, nt), F32))
    _, rank, vals = lax.fori_loop(0, count, body, init)
    return vals, rank


def _pair_candidates(v1, v2):
    k = PEER_TOPK
    nt = v1.shape[1]
    j8 = lax.broadcasted_iota(jnp.int32, (8, nt), 0)
    j16 = lax.broadcasted_iota(jnp.int32, (k, nt), 0)
    sums, pos = [v1[0:1, :] + v2], [j16]
    for i in range(1, 8):
        sums.append(jnp.where(j8 < k // (i + 1), v1[i:i + 1, :] + v2[0:8, :], -jnp.inf))
        pos.append(i * k + j8)
    sums.append(v1[8:k, :] + v2[0:1, :])
    pos.append((8 + j8) * k)
    return jnp.concatenate(sums, axis=0), jnp.concatenate(pos, axis=0)


def _sorting_network(n):
    size = 1 << (n - 1).bit_length()
    pairs = []

    def merge(lo, hi, r):
        step = r * 2
        if step < hi - lo:
            merge(lo, hi, step)
            merge(lo + r, hi, step)
            pairs.extend((i, i + r) for i in range(lo + r, hi - r, step))
        else:
            pairs.append((lo, lo + r))

    def sort(lo, hi):
        if hi - lo >= 1:
            mid = lo + (hi - lo) // 2
            sort(lo, mid)
            sort(mid + 1, hi)
            merge(lo, hi, 1)

    sort(0, size - 1)
    return [(i, j) for i, j in pairs if j < n]


def _take_top_values(x, count):
    sub = 8
    tiles = [x[i:i + sub, :] for i in range(0, x.shape[0], sub)]
    for i, j in _sorting_network(len(tiles)):
        tiles[i], tiles[j] = jnp.maximum(tiles[i], tiles[j]), jnp.minimum(tiles[i], tiles[j])
    slot = lax.broadcasted_iota(jnp.int32, (count, x.shape[1]), 0)
    vals = jnp.zeros((count, x.shape[1]), F32)
    for r in range(count):
        m = jnp.max(tiles[0], axis=0, keepdims=True)
        vals = jnp.where(slot == r, m, vals)
        pop = tiles[0] == m
        depth = min(len(tiles), count - r)
        for k in range(depth - 1):
            tiles[k] = jnp.where(pop, tiles[k + 1], tiles[k])
        tiles[depth - 1] = jnp.where(pop, -jnp.inf, tiles[depth - 1])
    return vals


def _count_rows(mask):
    return jnp.sum(jnp.where(mask, 1.0, 0.0), axis=0, keepdims=True)


def _per_first_key(sel):
    k = PEER_TOPK
    cnt = jnp.where(sel, 1.0, 0.0)
    out = [jnp.sum(cnt[0:k, :], axis=0, keepdims=True)]
    out += [jnp.sum(cnt[k + 8 * (i - 1):k + 8 * i, :], axis=0, keepdims=True) for i in range(1, 8)]
    out += [cnt[k + 56 + i:k + 57 + i, :] for i in range(k - 8)]
    return out


def _route_tables_exact(s1, s2):
    k = PEER_TOPK
    key = lax.broadcasted_iota(jnp.int32, s1.shape, 0)
    v1, rank1 = _take_top(s1, key, k)
    v2, rank2 = _take_top(s2, key, k)
    cand, pos = _pair_candidates(v1, v2)
    _, crank = _take_top(cand, pos, k)
    sel = crank < k
    z = jnp.sum(jnp.where(sel, jnp.exp(cand - (v1[0:1, :] + v2[0:1, :])), 0.0), axis=0, keepdims=True)
    n = jnp.zeros(s1.shape, F32)
    for i, n_i in enumerate(_per_first_key(sel)):
        n = jnp.where(rank1 == i, n_i, n)
    e1 = jnp.where(rank1 < k, jnp.exp(s1 - v1[0:1, :]) / z, 0.0)
    e2 = jnp.where(rank2 < k, jnp.exp(s2 - v2[0:1, :]), 0.0)
    return n, e1, rank2.astype(F32), e2


def _route_tables_distinct(s1, s2):
    k = PEER_TOPK
    v1 = _take_top_values(s1, k)
    v2 = _take_top_values(s2, k)
    in1 = s1 >= v1[k - 1:k, :]
    in2 = s2 >= v2[k - 1:k, :]
    cand, _ = _pair_candidates(v1, v2)
    cv = _take_top_values(cand, k)
    sel = cand >= cv[k - 1:k, :]
    ok = (_count_rows(in1) == k) & (_count_rows(in2) == k) & (_count_rows(sel) == k)
    z = jnp.sum(jnp.where(sel, jnp.exp(cand - (v1[0:1, :] + v2[0:1, :])), 0.0), axis=0, keepdims=True)
    n = jnp.zeros(s1.shape, F32)
    rank2 = jnp.full(s2.shape, float(k), F32)
    for i, n_i in enumerate(_per_first_key(sel)):
        n = jnp.where(s1 == v1[i:i + 1, :], n_i, n)
        rank2 = jnp.where(s2 == v2[i:i + 1, :], float(i), rank2)
    e1 = jnp.where(in1, jnp.exp(s1 - v1[0:1, :]) / z, 0.0)
    e2 = jnp.where(in2, jnp.exp(s2 - v2[0:1, :]), 0.0)
    return (n, e1, rank2, e2), ok


def _peer_route_kernel(q_ref, k1_ref, k2_ref, n_ref, e1_ref, r2_ref, e2_ref, s1_ref, s2_ref):
    q = q_ref[...]
    s1_ref[...] = _dot_nt(k1_ref[0], q[:, :PEER_HALF])
    s2_ref[...] = _dot_nt(k2_ref[0], q[:, PEER_HALF:])
    tt = q.shape[0]
    lw = min(LANES, tt)
    for lc in range(tt // lw):
        sl = slice(lc * lw, (lc + 1) * lw)

        def store(tables):
            n, e1, rank2, e2 = tables
            n_ref[0, :, sl] = n
            e1_ref[0, :, sl] = e1
            r2_ref[:, sl] = rank2.astype(BF16)
            e2_ref[:, sl] = e2.astype(BF16)

        tables, ok = _route_tables_distinct(s1_ref[:, sl], s2_ref[:, sl])
        store(tables)

        @pl.when(jnp.min(jnp.where(ok, 1, 0)) == 0)
        def _():
            store(_route_tables_exact(s1_ref[:, sl], s2_ref[:, sl]))


def _peer_route(q, keys1, keys2, tt):
    t = q.shape[0]
    out_a = jax.ShapeDtypeStruct((PEER_HEADS, PEER_NKEYS, t), F32)
    out_b = jax.ShapeDtypeStruct((PEER_HEADS * PEER_NKEYS, t), BF16)
    spec_a = pl.BlockSpec((1, PEER_NKEYS, tt), lambda i, h: (h, 0, i))
    spec_b = pl.BlockSpec((PEER_NKEYS, tt), lambda i, h: (h, i))
    key_spec = pl.BlockSpec((1, PEER_NKEYS, PEER_HALF), lambda i, h: (h, 0, 0))
    return pl.pallas_call(
        _peer_route_kernel,
        out_shape=(out_a, out_a, out_b, out_b),
        grid=(t // tt, PEER_HEADS),
        in_specs=[pl.BlockSpec((tt, PEER_DQ), lambda i, h: (i, h)), key_spec, key_spec],
        out_specs=(spec_a, spec_a, spec_b, spec_b),
        scratch_shapes=[pltpu.VMEM((PEER_NKEYS, tt), F32), pltpu.VMEM((PEER_NKEYS, tt), F32)],
        compiler_params=_params(("parallel", "arbitrary")),
        name="peer_route",
    )(q, keys1, keys2)


def _peer_dense_kernel(xa_ref, xc_ref, lnw_ref, n_ref, e1_ref, r2_in_ref, e2_in_ref, u_ref, v_ref, o_ref,
                       h_ref, pre_ref, pre2_ref, act_ref, act2_ref, acc_ref, r2_ref, e2_ref,
                       *, rows_per_step, n_blocks, n_work):
    g = pl.program_id(0)
    last = n_work - 1
    ja = jnp.minimum(g, last) % n_blocks
    jb = jnp.clip(g - 1, 0, last) % n_blocks
    jc = jnp.clip(g - 2, 0, last) % n_blocks

    @pl.when(g == 0)
    def _():
        for ref in (pre_ref, pre2_ref, act_ref, act2_ref):
            ref[...] = jnp.zeros_like(ref)

    @pl.when(ja == 0)
    def _():
        x = xa_ref[...]
        y = x * lax.rsqrt(jnp.mean(x * x, axis=-1, keepdims=True) + EPS)
        h_ref[...] = (y * lnw_ref[...]).astype(BF16)

    @pl.when(jb == 0)
    def _():
        r2_ref[...] = r2_in_ref[...]
        e2_ref[...] = e2_in_ref[...]

    @pl.when(jc == 0)
    def _():
        acc_ref[...] = jnp.zeros_like(acc_ref)

    tt = h_ref.shape[0]
    lw = min(LANES, tt)
    reps = PEER_NKEYS // BF16_ROWS

    def stages(pre_w, pre_r, act_w, act_r):
        zero = jnp.zeros((PEER_NKEYS, lw), BF16)
        mxu_width = min(256, tt)

        def stage_a(part):
            tok = slice(part * mxu_width, (part + 1) * mxu_width)
            pre_w[:, tok] = lax.dot_general(u_ref[...], h_ref[tok, :], _NT, preferred_element_type=F32)

        def stage_b(lc, aa):
            sl = slice(lc * lw, (lc + 1) * lw)
            rows = slice(aa * PEER_NKEYS, (aa + 1) * PEER_NKEYS)
            gate = zero
            for h in range(PEER_HEADS):
                n_row = jnp.broadcast_to(n_ref[h, aa:aa + 1, sl], (BF16_ROWS, lw)).astype(BF16)
                e1_row = jnp.broadcast_to(e1_ref[h, aa:aa + 1, sl], (BF16_ROWS, lw)).astype(BF16)
                n_tile = jnp.concatenate([n_row] * reps, axis=0)
                e1_tile = jnp.concatenate([e1_row] * reps, axis=0)
                keys = slice(h * PEER_NKEYS, (h + 1) * PEER_NKEYS)
                gate = gate + jnp.where(r2_ref[keys, sl] < n_tile, e2_ref[keys, sl], zero) * e1_tile
            act = _gelu_tanh(pre_r[rows, sl].astype(BF16)) * gate
            act_w[sl, rows] = act.T

        def stage_c(part):
            col = slice(part * 256, (part + 1) * 256)
            acc_ref[:, col] += jnp.dot(act_r[...], v_ref[:, col], preferred_element_type=F32)

        mxu = [functools.partial(stage_a, p) for p in range(tt // mxu_width)]
        mxu += [functools.partial(stage_c, p) for p in range(acc_ref.shape[1] // 256)]
        vpu = [functools.partial(stage_b, lc, aa) for lc in range(tt // lw) for aa in range(rows_per_step)]
        per = -(-len(vpu) // len(mxu))
        for i, piece in enumerate(mxu):
            piece()
            for tile in vpu[i * per:(i + 1) * per]:
                tile()

    @pl.when(g % 2 == 0)
    def _():
        stages(pre_ref, pre2_ref, act2_ref, act_ref)

    @pl.when(g % 2 == 1)
    def _():
        stages(pre2_ref, pre_ref, act_ref, act2_ref)

    @pl.when((jc == n_blocks - 1) & (g >= 2))
    def _():
        o_ref[...] = xc_ref[...] + acc_ref[...]


def _peer_dense(x, lnw, route, u, v, tt, rows_per_step):
    t, d = x.shape
    eb = rows_per_step * PEER_NKEYS
    n_blocks = PEER_N // eb
    n_work = (t // tt) * n_blocks

    def item(lag):
        def at(g):
            w = jnp.clip(g - lag, 0, n_work - 1)
            return w // n_blocks, w % n_blocks
        return at

    a, b, c = item(0), item(1), item(2)
    spec_rows = pl.BlockSpec((PEER_HEADS, rows_per_step, tt), lambda g: (0, b(g)[1], b(g)[0]))
    spec_keys = pl.BlockSpec((PEER_HEADS * PEER_NKEYS, tt), lambda g: (0, b(g)[0]))
    return pl.pallas_call(
        functools.partial(_peer_dense_kernel, rows_per_step=rows_per_step, n_blocks=n_blocks, n_work=n_work),
        out_shape=jax.ShapeDtypeStruct((t, d), F32),
        grid=(n_work + 2,),
        in_specs=[pl.BlockSpec((tt, d), lambda g: (a(g)[0], 0)),
                  pl.BlockSpec((tt, d), lambda g: (c(g)[0], 0)),
                  pl.BlockSpec((1, d), lambda g: (0, 0)),
                  spec_rows, spec_rows, spec_keys, spec_keys,
                  pl.BlockSpec((eb, d), lambda g: (a(g)[1], 0)),
                  pl.BlockSpec((eb, d), lambda g: (c(g)[1], 0))],
        out_specs=pl.BlockSpec((tt, d), lambda g: (c(g)[0], 0)),
        scratch_shapes=[pltpu.VMEM((tt, d), BF16), pltpu.VMEM((eb, tt), F32), pltpu.VMEM((eb, tt), F32),
                        pltpu.VMEM((tt, eb), BF16), pltpu.VMEM((tt, eb), BF16), pltpu.VMEM((tt, d), F32),
                        pltpu.VMEM((PEER_HEADS * PEER_NKEYS, tt), BF16),
                        pltpu.VMEM((PEER_HEADS * PEER_NKEYS, tt), BF16)],
        compiler_params=_params(("arbitrary",)),
        name="peer_dense",
    )(x, x, lnw.reshape(1, d), *route, u, v)


def _peer_ffn(x, lnw, wq, keys1, keys2, u, v):
    tt = min(512, x.shape[0])
    q = _norm_matmul(x, lnw, wq)
    route = _peer_route(q, keys1, keys2, tt)
    return _peer_dense(x, lnw, route, u, v, tt, rows_per_step=16)


def _pad_lanes(vec, offset):
    return jnp.zeros((SMALL_W,), F32).at[offset:offset + vec.shape[0]].set(vec.astype(F32))


def _conv_history(buf):
    return jnp.pad(buf, ((0, 0), (CONV_PAD - (CONV_W - 1), 0), (0, 0)))


def kernel(x_prompt, x_sample, state_dn_conv, state_dn, state_ssm_conv, state_ssm, state_gla, ln_mix, ln_ffn, ln_final, w_in_even, w_out_even, dn_conv_w, dn_a_log, dn_dt_bias, dn_norm_w, ssm_conv_w, ssm_conv_b, ssm_a_log, ssm_dt_bias, ssm_d, ssm_norm_w, w_in_odd, gla_w_gate2, gla_b_gate, gla_norm_w, w_out_odd, peer_w_q, peer_keys1, peer_keys2, peer_u, peer_v):
    we = w_in_even[0]
    o = 0
    cuts = {}
    for name, size in (("qkv", DN_CONV_DIM), ("a", DN_HEADS), ("b", DN_HEADS), ("gate", DN_V),
                       ("z", SSM_INNER), ("xbc", SSM_CONV_DIM), ("dt", SSM_HEADS)):
        cuts[name] = we[:, o:o + size]
        o += size
    small_w = jnp.concatenate([cuts["a"], cuts["b"], cuts["dt"]], axis=1)
    small_w = jnp.pad(small_w, ((0, 0), (0, SMALL_W - small_w.shape[1])))
    w_even = jnp.concatenate([cuts["qkv"], cuts["gate"], cuts["xbc"], cuts["z"], small_w], axis=1).astype(BF16)
    wo = w_in_odd[0]
    qkv_w = 2 * GLA_K + GLA_V
    glr_w = jnp.pad(wo[:, qkv_w:qkv_w + GLA_LR], ((0, 0), (0, SMALL_W - GLA_LR)))
    w_odd = jnp.concatenate([wo[:, :qkv_w], wo[:, qkv_w + GLA_LR:], glr_w], axis=1).astype(BF16)
    w_out_dn = w_out_even[0, :DN_V].astype(BF16)
    w_out_ssm = w_out_even[0, DN_V:].astype(BF16)
    w_out_gla = w_out_odd[0].astype(BF16)
    wg = jnp.pad(gla_w_gate2[0], ((0, SMALL_W - GLA_LR), (0, 0))).astype(BF16)
    wq = peer_w_q.astype(BF16)
    k1 = peer_keys1.astype(BF16)
    k2 = peer_keys2.astype(BF16)
    pu = [peer_u[i].astype(BF16) for i in range(peer_u.shape[0])]
    pv = [peer_v[i].astype(BF16) for i in range(peer_v.shape[0])]

    zero8 = jnp.zeros((8 - 2, SMALL_W), F32)
    dn_par = jnp.concatenate([_pad_lanes(dn_a_log[0], 0)[None], _pad_lanes(dn_dt_bias[0], 0)[None], zero8])
    lane0 = 2 * DN_HEADS
    ssm_par = jnp.concatenate([_pad_lanes(ssm_a_log[0], lane0)[None], _pad_lanes(ssm_dt_bias[0], lane0)[None],
                               _pad_lanes(ssm_d[0], lane0)[None], zero8[1:]])

    def trunk(x, batch, seq_len, conv_dn0, s_dn0, conv_ssm0, s_ssm0, s_gla0):
        chunk = PROMPT_CHUNK if seq_len >= PROMPT_CHUNK else STEP_CHUNK
        nc = -(-seq_len // chunk)
        padded = nc * chunk != seq_len
        dn_nb = max(1, min(batch, 256 // (DN_HEADS * chunk)))
        while batch % dn_nb:
            dn_nb -= 1

        def to_seq(a):
            if not padded:
                return a
            a = a.reshape(batch, seq_len, a.shape[-1])
            return jnp.pad(a, ((0, 0), (0, nc * chunk - seq_len), (0, 0))).reshape(batch * nc * chunk, -1)

        def from_seq(a):
            if not padded:
                return a
            return a.reshape(batch, nc * chunk, a.shape[-1])[:, :seq_len].reshape(batch * seq_len, -1)

        def last_rows(buf, proj, col, width):
            tail = proj.reshape(batch, seq_len, proj.shape[-1])[:, -min(seq_len, CONV_W - 1):, col:col + width]
            return jnp.concatenate([buf, tail], axis=1)[:, -(CONV_W - 1):]

        seq = dict(batch=batch, seq_len=seq_len, chunk=chunk)

        proj = _norm_matmul(x, ln_mix[0], w_even)
        proj_seq = to_seq(proj)
        o_dn, s_dn = _dn_mixer(proj_seq.reshape(batch, nc * chunk, EV_WIDTH), _conv_history(conv_dn0[0]),
                               s_dn0[0], dn_conv_w[0], dn_par, dn_norm_w[0].reshape(1, DN_DV),
                               nb=dn_nb, **seq)
        o_dn = o_dn.reshape(batch * nc * chunk, DN_V)
        y_ssm, s_ssm = _ssd_mixer(proj_seq, _conv_history(conv_ssm0[0]), s_ssm0[0], ssm_conv_w[0],
                                  ssm_conv_b[0].reshape(1, SSM_CONV_DIM), ssm_par,
                                  ssm_norm_w[0].reshape(1, SSM_INNER), **seq)
        conv_dn = last_rows(conv_dn0[0], proj, EV_QKV, DN_CONV_DIM)
        conv_ssm = last_rows(conv_ssm0[0], proj, EV_XBC, SSM_CONV_DIM)
        x = _proj_residual(x, [from_seq(o_dn), from_seq(y_ssm)], [w_out_dn, w_out_ssm])
        x = _peer_ffn(x, ln_ffn[0], wq[0], k1[0], k2[0], pu[0], pv[0])

        proj = _norm_matmul(x, ln_mix[1], w_odd)
        o_gla, s_gla = _gla_mixer(to_seq(proj), s_gla0[0], wg, gla_b_gate[0].reshape(1, GLA_K),
                                  gla_norm_w[0].reshape(1, GLA_DV), **seq)
        x = _proj_residual(x, [from_seq(o_gla)], [w_out_gla])
        x = _peer_ffn(x, ln_ffn[1], wq[1], k1[1], k2[1], pu[1], pv[1])

        y = _final_norm(x, ln_final).reshape(batch, seq_len, D_MODEL)
        return y, conv_dn[None], s_dn[None], conv_ssm[None], s_ssm[None], s_gla[None]

    bp, lp = x_prompt.shape[:2]
    bs, ls = x_sample.shape[:2]

    def zeros_like_state(s):
        return jnp.zeros((s.shape[0], bp) + s.shape[2:], s.dtype)

    prompt = trunk(x_prompt.reshape(bp * lp, D_MODEL), bp, lp, zeros_like_state(state_dn_conv),
                   zeros_like_state(state_dn), zeros_like_state(state_ssm_conv),
                   zeros_like_state(state_ssm), zeros_like_state(state_gla))
    sample = trunk(x_sample.reshape(bs * ls, D_MODEL), bs, ls, state_dn_conv, state_dn,
                   state_ssm_conv, state_ssm, state_gla)
    return (prompt[0], sample[0]) + prompt[1:] + sample[1:]
```

```python
import functools

import jax
import jax.numpy as jnp
from jax import lax
from jax.experimental import pallas as pl
from jax.experimental.pallas import tpu as pltpu

F32 = jnp.float32
BF16 = jnp.bfloat16
EPS = 1e-6

D_MODEL = 1024
CONV_W = 4
CONV_PAD = 8
PROMPT_CHUNK = 64
STEP_CHUNK = 8

DN_HEADS, DN_DK, DN_DV = 4, 128, 128
DN_QK = DN_HEADS * DN_DK
DN_V = DN_HEADS * DN_DV
DN_CONV_DIM = 2 * DN_QK + DN_V
SSM_HEADS, SSM_P, SSM_GROUPS, SSM_N = 8, 64, 2, 128
SSM_INNER = SSM_HEADS * SSM_P
SSM_CONV_DIM = SSM_INNER + 2 * SSM_GROUPS * SSM_N
GLA_HEADS, GLA_DK, GLA_DV, GLA_LR = 4, 128, 256, 16
GLA_K = GLA_HEADS * GLA_DK
GLA_V = GLA_HEADS * GLA_DV
GLA_NORMALIZER = 16.0
PEER_HEADS, PEER_DQ, PEER_NKEYS, PEER_TOPK = 8, 256, 128, 16
PEER_HALF = PEER_DQ // 2
PEER_N = PEER_NKEYS * PEER_NKEYS
LANES = 128
BF16_ROWS = 16
SMALL_W = LANES

EV_QKV, EV_GATE, EV_XBC, EV_Z, EV_SMALL = 0, 1536, 2048, 3072, 3584
EV_WIDTH = EV_SMALL + SMALL_W
OD_QKV, OD_OG, OD_GLR = 0, 2048, 3072
OD_WIDTH = OD_GLR + SMALL_W

VMEM_LIMIT = 56 * 1024 * 1024

_NT = (((1,), (1,)), ((), ()))
_TN = (((0,), (0,)), ((), ()))


def _dot(a, b):
    return jnp.dot(a.astype(BF16), b.astype(BF16), preferred_element_type=F32)


def _dot_nt(a, b):
    return lax.dot_general(a.astype(BF16), b.astype(BF16), _NT, preferred_element_type=F32)


def _dot_tn(a, b):
    return lax.dot_general(a.astype(BF16), b.astype(BF16), _TN, preferred_element_type=F32)


def _params(sem):
    return pltpu.CompilerParams(dimension_semantics=sem, vmem_limit_bytes=VMEM_LIMIT)


def _silu(x):
    return x * jax.nn.sigmoid(x)


def _gelu_tanh(x):
    c0 = 0.7978845608028654
    hx = 0.5 * x
    return hx + hx * jnp.tanh(x * (c0 + (c0 * 0.044715) * (x * x)))


def _softplus(x):
    return jnp.maximum(x, 0.0) + jnp.log(1.0 + jnp.exp(-jnp.abs(x)))


def _cumsum_rows(x):
    n = x.shape[0]
    row = lax.broadcasted_iota(jnp.int32, x.shape, 0)
    s = 1
    while s < n:
        x = x + jnp.where(row >= s, pltpu.roll(x, s, axis=0), 0.0)
        s *= 2
    return x


def _col_to_row(col, eye):
    return jnp.sum(jnp.where(eye, col, 0.0), axis=0, keepdims=True)


def _row_to_col(row, eye):
    return jnp.sum(jnp.where(eye, row, 0.0), axis=1, keepdims=True)


def _tri_masks(n):
    ri = lax.broadcasted_iota(jnp.int32, (n, n), 0)
    ci = lax.broadcasted_iota(jnp.int32, (n, n), 1)
    return ri, ci, ri == ci, ri >= ci, ri > ci


def _valid_rows(chunk, seq_len, shape):
    if seq_len % chunk == 0:
        return None
    pos = pl.program_id(1) * chunk + lax.broadcasted_iota(jnp.int32, shape, 0)
    return pos < seq_len


def _causal_conv(x_ref, xext, cw_ref, chunk):
    x = x_ref[...]
    xext[CONV_PAD:CONV_PAD + chunk, :] = x
    cw = cw_ref[...]
    pre = cw[CONV_W - 1:CONV_W, :] * x
    for j in range(1, CONV_W):
        pre = pre + cw[CONV_W - 1 - j:CONV_W - j, :] * xext[CONV_PAD - j:CONV_PAD - j + chunk, :]
    xext[0:CONV_PAD, :] = xext[chunk:chunk + CONV_PAD, :]
    return pre


def _norm_matmul_kernel(x_ref, lnw_ref, w_ref, o_ref):
    x = x_ref[...]
    y = x * lax.rsqrt(jnp.mean(x * x, axis=-1, keepdims=True) + EPS)
    h = (y * lnw_ref[...]).astype(BF16)
    o_ref[...] = jnp.dot(h, w_ref[...], preferred_element_type=F32)


def _norm_matmul(x, lnw, w):
    t, d = x.shape
    n = w.shape[1]
    tm = min(512, t)
    return pl.pallas_call(
        _norm_matmul_kernel,
        out_shape=jax.ShapeDtypeStruct((t, n), F32),
        grid=(t // tm,),
        in_specs=[pl.BlockSpec((tm, d), lambda i: (i, 0)),
                  pl.BlockSpec((1, d), lambda i: (0, 0)),
                  pl.BlockSpec((d, n), lambda i: (0, 0))],
        out_specs=pl.BlockSpec((tm, n), lambda i: (i, 0)),
        compiler_params=_params(("parallel",)),
        name="norm_matmul",
    )(x, lnw.reshape(1, d), w)


def _proj_residual_kernel(*refs, n_in):
    x_ref, o_ref = refs[0], refs[-1]
    acc = x_ref[...]
    for a_ref, w_ref in zip(refs[1:1 + n_in], refs[1 + n_in:1 + 2 * n_in]):
        acc = acc + jnp.dot(a_ref[...].astype(BF16), w_ref[...], preferred_element_type=F32)
    o_ref[...] = acc


def _proj_residual(x, acts, weights):
    t, d = x.shape
    tm = min(512, t)
    n_in = len(acts)
    in_specs = [pl.BlockSpec((tm, d), lambda i: (i, 0))]
    in_specs += [pl.BlockSpec((tm, a.shape[1]), lambda i: (i, 0)) for a in acts]
    in_specs += [pl.BlockSpec(w.shape, lambda i: (0, 0)) for w in weights]
    return pl.pallas_call(
        functools.partial(_proj_residual_kernel, n_in=n_in),
        out_shape=jax.ShapeDtypeStruct((t, d), F32),
        grid=(t // tm,),
        in_specs=in_specs,
        out_specs=pl.BlockSpec((tm, d), lambda i: (i, 0)),
        compiler_params=_params(("parallel",)),
        name="proj_residual",
    )(x, *acts, *weights)


def _final_norm_kernel(x_ref, w_ref, o_ref):
    x = x_ref[...]
    o_ref[...] = x * lax.rsqrt(jnp.mean(x * x, axis=-1, keepdims=True) + EPS) * w_ref[...]


def _final_norm(x, w):
    t, d = x.shape
    tm = min(512, t)
    return pl.pallas_call(
        _final_norm_kernel,
        out_shape=jax.ShapeDtypeStruct((t, d), F32),
        grid=(t // tm,),
        in_specs=[pl.BlockSpec((tm, d), lambda i: (i, 0)), pl.BlockSpec((1, d), lambda i: (0, 0))],
        out_specs=pl.BlockSpec((tm, d), lambda i: (i, 0)),
        compiler_params=_params(("parallel",)),
        name="final_norm",
    )(x, w.reshape(1, d))


def _unit_lower_inverse(m, ri, ci, eye, block):
    t = jnp.where(eye, 1.0, 0.0)
    level = 0
    while (1 << level) < block:
        pair = ((ri >> (level + 1)) == (ci >> (level + 1))) & ((ri >> level) != (ci >> level))
        c = jnp.where(pair, m, 0.0)
        t = t - c if level == 0 else t - _dot(_dot(t, c), t)
        level += 1
    return t


def _dn_kernel(qkv_ref, gate_ref, small_ref, conv0_ref, s0_ref, cw_ref, par_ref, nw_ref,
               o_ref, s_out_ref, xext, state, *, chunk, seq_len, nb):
    c_idx = pl.program_id(1)
    groups = nb * DN_HEADS
    log2c = chunk.bit_length() - 1
    group_lanes = lambda g: slice(g * DN_DV, (g + 1) * DN_DV)

    @pl.when(c_idx == 0)
    def _():
        for s in range(nb):
            xext[s, 0:CONV_PAD, :] = conv0_ref[s]
            for h in range(DN_HEADS):
                state[:, group_lanes(s * DN_HEADS + h)] = s0_ref[s, h]

    par = par_ref[...]
    qs, ks, vs, gcs, betas, gates = [], [], [], [], [], []
    for s in range(nb):
        act = _silu(_causal_conv(qkv_ref.at[s], xext.at[s], cw_ref, chunk))
        small = small_ref[s]
        g_all = -jnp.exp(par[0:1, :]) * _softplus(small + par[1:2, :])
        beta_all = jax.nn.sigmoid(small)
        valid = _valid_rows(chunk, seq_len, small.shape)
        if valid is not None:
            g_all = jnp.where(valid, g_all, 0.0)
            beta_all = jnp.where(valid, beta_all, 0.0)
        gc_all = _cumsum_rows(g_all)
        for h in range(DN_HEADS):
            xq = act[:, h * DN_DK:(h + 1) * DN_DK]
            xk = act[:, DN_QK + h * DN_DK:DN_QK + (h + 1) * DN_DK]
            qs.append(xq * lax.rsqrt(jnp.sum(xq * xq, axis=-1, keepdims=True) + EPS) * (DN_DK ** -0.5))
            ks.append(xk * lax.rsqrt(jnp.sum(xk * xk, axis=-1, keepdims=True) + EPS))
            vs.append(act[:, 2 * DN_QK + h * DN_DV:2 * DN_QK + (h + 1) * DN_DV])
            gcs.append(gc_all[:, h:h + 1])
            betas.append(beta_all[:, DN_HEADS + h:DN_HEADS + h + 1])
            gates.append(gate_ref[s, :, h * DN_DV:(h + 1) * DN_DV])
    q, k, v = (jnp.concatenate(t, axis=0) for t in (qs, ks, vs))
    gc, beta = (jnp.concatenate(t, axis=0) for t in (gcs, betas))
    gc_last = jnp.concatenate([jnp.broadcast_to(t[chunk - 1:chunk, :], (chunk, 1)) for t in gcs], axis=0)
    last_decay = jnp.concatenate(
        [jnp.broadcast_to(jnp.exp(t[chunk - 1:chunk, :]), (1, DN_DV)) for t in gcs], axis=1)

    rows = groups * chunk
    ri, ci, eye, lower, strict_lower = _tri_masks(rows)
    same = (ri >> log2c) == (ci >> log2c)
    causal = same & lower
    strict = same & strict_lower
    decay = jnp.where(causal, jnp.exp(gc - _col_to_row(gc, eye)), 0.0)
    kb = k * beta
    eg = jnp.exp(gc)
    m = jnp.where(strict, _dot_nt(kb, k) * decay, 0.0)
    t_inv = _unit_lower_inverse(m, ri, ci, eye, chunk)
    sol = _dot(t_inv, jnp.concatenate([v * beta, kb * eg], axis=1))
    u, w = sol[:, :DN_DV], sol[:, DN_DV:]
    qk = jnp.where(causal, _dot_nt(q, k) * decay, 0.0)

    def own_block(x):
        return jnp.concatenate(
            [x[g * chunk:(g + 1) * chunk, group_lanes(g)] for g in range(groups)], axis=0)

    s_all = state[...]
    v_new = u - own_block(_dot(w, s_all))
    o = own_block(_dot(q * eg, s_all)) + _dot(qk, v_new)
    wide = (rows, groups * DN_DV)
    row_group = lax.broadcasted_iota(jnp.int32, wide, 0) >> log2c
    lane_group = lax.broadcasted_iota(jnp.int32, wide, 1) // DN_DV
    v_wide = jnp.where(row_group == lane_group, jnp.concatenate([v_new] * groups, axis=1), 0.0)
    state[...] = s_all * last_decay + _dot_tn(k * jnp.exp(gc_last - gc), v_wide)

    o = o * lax.rsqrt(jnp.mean(o * o, axis=-1, keepdims=True) + EPS) * nw_ref[...]
    y = o * _silu(jnp.concatenate(gates, axis=0))
    for s in range(nb):
        for h in range(DN_HEADS):
            g = s * DN_HEADS + h
            o_ref[s, :, h * DN_DV:(h + 1) * DN_DV] = y[g * chunk:(g + 1) * chunk, :]

    @pl.when(c_idx == pl.num_programs(1) - 1)
    def _():
        for s in range(nb):
            for h in range(DN_HEADS):
                s_out_ref[s, h] = state[:, group_lanes(s * DN_HEADS + h)]


def _dn_mixer(proj, conv0, s0, cw, par, nw, *, batch, seq_len, chunk, nb):
    nc = proj.shape[1] // chunk
    return pl.pallas_call(
        functools.partial(_dn_kernel, chunk=chunk, seq_len=seq_len, nb=nb),
        out_shape=(jax.ShapeDtypeStruct((batch, nc * chunk, DN_V), F32),
                   jax.ShapeDtypeStruct((batch, DN_HEADS, DN_DK, DN_DV), F32)),
        grid=(batch // nb, nc),
        in_specs=[pl.BlockSpec((nb, chunk, DN_CONV_DIM), lambda b, c: (b, c, EV_QKV // DN_CONV_DIM)),
                  pl.BlockSpec((nb, chunk, DN_V), lambda b, c: (b, c, EV_GATE // DN_V)),
                  pl.BlockSpec((nb, chunk, SMALL_W), lambda b, c: (b, c, EV_SMALL // SMALL_W)),
                  pl.BlockSpec((nb, CONV_PAD, DN_CONV_DIM), lambda b, c: (b, 0, 0)),
                  pl.BlockSpec((nb, DN_HEADS, DN_DK, DN_DV), lambda b, c: (b, 0, 0, 0)),
                  pl.BlockSpec((CONV_W, DN_CONV_DIM), lambda b, c: (0, 0)),
                  pl.BlockSpec((8, SMALL_W), lambda b, c: (0, 0)),
                  pl.BlockSpec((1, DN_DV), lambda b, c: (0, 0))],
        out_specs=(pl.BlockSpec((nb, chunk, DN_V), lambda b, c: (b, c, 0)),
                   pl.BlockSpec((nb, DN_HEADS, DN_DK, DN_DV), lambda b, c: (b, 0, 0, 0))),
        scratch_shapes=[pltpu.VMEM((nb, chunk + CONV_PAD, DN_CONV_DIM), F32),
                        pltpu.VMEM((DN_DK, nb * DN_HEADS * DN_DV), F32)],
        compiler_params=_params(("parallel", "arbitrary")),
        name="dn_mixer",
    )(proj, proj, proj, conv0, s0, cw, par, nw)


def _ssd_kernel(xbc_ref, z_ref, small_ref, conv0_ref, s0_ref, cw_ref, cb_ref, par_ref, nw_ref,
                y_ref, s_out_ref, xext, state, ybuf, *, chunk, seq_len, nb):
    c_idx = pl.program_id(1)
    groups = nb * SSM_HEADS
    log2c = chunk.bit_length() - 1
    group_rows = lambda g: slice(g * SSM_P, (g + 1) * SSM_P)

    @pl.when(c_idx == 0)
    def _():
        for s in range(nb):
            xext[s, 0:CONV_PAD, :] = conv0_ref[s]
            for h in range(SSM_HEADS):
                state[group_rows(s * SSM_HEADS + h), :] = s0_ref[s, h]

    par = par_ref[...]
    gw = SSM_GROUPS * SSM_N
    heads_per_group = SSM_HEADS // SSM_GROUPS
    xs, bms, cms, dts, acs, ds = [], [], [], [], [], []
    for s in range(nb):
        xbc = _silu(_causal_conv(xbc_ref.at[s], xext.at[s], cw_ref, chunk) + cb_ref[...])
        small = small_ref[s]
        dt_all = _softplus(small + par[1:2, :])
        valid = _valid_rows(chunk, seq_len, small.shape)
        if valid is not None:
            dt_all = jnp.where(valid, dt_all, 0.0)
        ac_all = _cumsum_rows(dt_all * -jnp.exp(par[0:1, :]))
        for h in range(SSM_HEADS):
            g = h // heads_per_group
            lane = 2 * DN_HEADS + h
            xs.append(xbc[:, h * SSM_P:(h + 1) * SSM_P])
            bms.append(xbc[:, SSM_INNER + g * SSM_N:SSM_INNER + (g + 1) * SSM_N])
            cms.append(xbc[:, SSM_INNER + gw + g * SSM_N:SSM_INNER + gw + (g + 1) * SSM_N])
            dts.append(dt_all[:, lane:lane + 1])
            acs.append(ac_all[:, lane:lane + 1])
            ds.append(jnp.broadcast_to(par[2:3, lane:lane + 1], (chunk, 1)))
    per_stack = max(1, min(groups, 256 // chunk))
    rows = per_stack * chunk
    ri, ci, eye, lower, strict_lower = _tri_masks(rows)
    causal = ((ri >> log2c) == (ci >> log2c)) & lower
    wide = (rows, per_stack * SSM_P)
    row_group = lax.broadcasted_iota(jnp.int32, wide, 0) >> log2c
    lane_group = lax.broadcasted_iota(jnp.int32, wide, 1) // SSM_P
    y_groups = []
    for lo in range(0, groups, per_stack):
        part = slice(lo, lo + per_stack)
        x, bm, cm = (jnp.concatenate(t[part], axis=0) for t in (xs, bms, cms))
        dt, ac, d_skip = (jnp.concatenate(t[part], axis=0) for t in (dts, acs, ds))
        ac_last = jnp.concatenate(
            [jnp.broadcast_to(t[chunk - 1:chunk, :], (chunk, 1)) for t in acs[part]], axis=0)
        last_decay = jnp.concatenate(
            [jnp.broadcast_to(jnp.exp(t[chunk - 1:chunk, :]), (SSM_P, 1)) for t in acs[part]], axis=0)
        decay = jnp.where(causal, jnp.exp(ac - _col_to_row(ac, eye)), 0.0)
        xdt = x * dt
        state_rows = slice(lo * SSM_P, (lo + per_stack) * SSM_P)
        s_all = state[state_rows, :]
        off = _dot_nt(cm * jnp.exp(ac), s_all)
        y = _dot(_dot_nt(cm, bm) * decay, xdt) + jnp.concatenate(
            [off[g * chunk:(g + 1) * chunk, group_rows(g)] for g in range(per_stack)], axis=0)
        xdt_pair = jnp.concatenate([xdt, xdt], axis=1)
        xdt_wide = jnp.where(row_group == lane_group,
                             jnp.concatenate([xdt_pair] * (per_stack // 2), axis=1), 0.0)
        state[state_rows, :] = s_all * last_decay + _dot_tn(xdt_wide, bm * jnp.exp(ac_last - ac))
        y = y + d_skip * x
        y_groups += [y[g * chunk:(g + 1) * chunk, :] for g in range(per_stack)]

    nw = nw_ref[...]
    gn = SSM_INNER // SSM_GROUPS
    for s in range(nb):
        for h in range(SSM_HEADS):
            ybuf[:, h * SSM_P:(h + 1) * SSM_P] = y_groups[s * SSM_HEADS + h]
        ys = ybuf[...] * _silu(z_ref[s])
        for g in range(SSM_GROUPS):
            yg = ys[:, g * gn:(g + 1) * gn]
            yg = yg * lax.rsqrt(jnp.mean(yg * yg, axis=-1, keepdims=True) + EPS)
            y_ref[s, :, g * gn:(g + 1) * gn] = yg * nw[:, g * gn:(g + 1) * gn]

    @pl.when(c_idx == pl.num_programs(1) - 1)
    def _():
        for s in range(nb):
            for h in range(SSM_HEADS):
                s_out_ref[s, h] = state[group_rows(s * SSM_HEADS + h), :]


def _ssd_mixer(proj, conv0, s0, cw, cb, par, nw, *, batch, seq_len, chunk, nb):
    nc = proj.shape[1] // chunk
    return pl.pallas_call(
        functools.partial(_ssd_kernel, chunk=chunk, seq_len=seq_len, nb=nb),
        out_shape=(jax.ShapeDtypeStruct((batch, nc * chunk, SSM_INNER), F32),
                   jax.ShapeDtypeStruct((batch, SSM_HEADS, SSM_P, SSM_N), F32)),
        grid=(batch // nb, nc),
        in_specs=[pl.BlockSpec((nb, chunk, SSM_CONV_DIM), lambda b, c: (b, c, EV_XBC // SSM_CONV_DIM)),
                  pl.BlockSpec((nb, chunk, SSM_INNER), lambda b, c: (b, c, EV_Z // SSM_INNER)),
                  pl.BlockSpec((nb, chunk, SMALL_W), lambda b, c: (b, c, EV_SMALL // SMALL_W)),
                  pl.BlockSpec((nb, CONV_PAD, SSM_CONV_DIM), lambda b, c: (b, 0, 0)),
                  pl.BlockSpec((nb, SSM_HEADS, SSM_P, SSM_N), lambda b, c: (b, 0, 0, 0)),
                  pl.BlockSpec((CONV_W, SSM_CONV_DIM), lambda b, c: (0, 0)),
                  pl.BlockSpec((1, SSM_CONV_DIM), lambda b, c: (0, 0)),
                  pl.BlockSpec((8, SMALL_W), lambda b, c: (0, 0)),
                  pl.BlockSpec((1, SSM_INNER), lambda b, c: (0, 0))],
        out_specs=(pl.BlockSpec((nb, chunk, SSM_INNER), lambda b, c: (b, c, 0)),
                   pl.BlockSpec((nb, SSM_HEADS, SSM_P, SSM_N), lambda b, c: (b, 0, 0, 0))),
        scratch_shapes=[pltpu.VMEM((nb, chunk + CONV_PAD, SSM_CONV_DIM), F32),
                        pltpu.VMEM((nb * SSM_HEADS * SSM_P, SSM_N), F32),
                        pltpu.VMEM((chunk, SSM_INNER), F32)],
        compiler_params=_params(("parallel", "arbitrary")),
        name="ssd_mixer",
    )(proj, proj, proj, conv0, s0, cw, cb, par, nw)


def _gla_kernel(qkv_ref, og_ref, glr_ref, s0_ref, wg_ref, bg_ref, nw_ref,
                o_ref, s_out_ref, state, *, chunk, seq_len, nb):
    c_idx = pl.program_id(1)
    groups = nb * GLA_HEADS
    log2c = chunk.bit_length() - 1
    group_lanes = lambda g: slice(g * GLA_DV, (g + 1) * GLA_DV)

    @pl.when(c_idx == 0)
    def _():
        for s in range(nb):
            for h in range(GLA_HEADS):
                state[:, group_lanes(s * GLA_HEADS + h)] = s0_ref[s, h]

    qs, ks, vs, bs, gates = [], [], [], [], []
    for s in range(nb):
        x = bg_ref[...] + jnp.dot(glr_ref[s].astype(BF16), wg_ref[...], preferred_element_type=F32)
        gk = (jnp.minimum(x, 0.0) - jnp.log(1.0 + jnp.exp(-jnp.abs(x)))) / GLA_NORMALIZER
        valid = _valid_rows(chunk, seq_len, gk.shape)
        if valid is not None:
            gk = jnp.where(valid, gk, 0.0)
        b_all = _cumsum_rows(gk)
        for h in range(GLA_HEADS):
            qs.append(qkv_ref[s, :, h * GLA_DK:(h + 1) * GLA_DK] * (GLA_DK ** -0.5))
            ks.append(qkv_ref[s, :, GLA_K + h * GLA_DK:GLA_K + (h + 1) * GLA_DK])
            vs.append(qkv_ref[s, :, 2 * GLA_K + h * GLA_DV:2 * GLA_K + (h + 1) * GLA_DV])
            bs.append(b_all[:, h * GLA_DK:(h + 1) * GLA_DK])
            gates.append(og_ref[s, :, h * GLA_DV:(h + 1) * GLA_DV])
    q, k, b = (jnp.concatenate(t, axis=0) for t in (qs, ks, bs))
    v = jnp.concatenate(vs, axis=0)
    b_last = jnp.concatenate([jnp.broadcast_to(t[chunk - 1:chunk, :], (chunk, GLA_DK)) for t in bs], axis=0)
    eye_k = _tri_masks(GLA_DK)[2]
    last_decay = jnp.concatenate(
        [jnp.broadcast_to(_row_to_col(jnp.exp(t[chunk - 1:chunk, :]), eye_k), (GLA_DK, GLA_DV)) for t in bs],
        axis=1)

    rows = groups * chunk
    ri, ci, eye, lower, strict_lower = _tri_masks(rows)
    causal = ((ri >> log2c) == (ci >> log2c)) & lower
    qe = q * jnp.exp(b)
    att = jnp.where(causal, _dot_nt(qe, k * jnp.exp(-b)), 0.0)

    s_all = state[...]
    qs_all = _dot(qe, s_all)
    o = jnp.concatenate([qs_all[g * chunk:(g + 1) * chunk, group_lanes(g)] for g in range(groups)], axis=0)
    o = o + _dot(att, v)
    wide = (rows, groups * GLA_DV)
    row_group = lax.broadcasted_iota(jnp.int32, wide, 0) >> log2c
    lane_group = lax.broadcasted_iota(jnp.int32, wide, 1) // GLA_DV
    v_wide = jnp.where(row_group == lane_group, jnp.concatenate([v] * groups, axis=1), 0.0)
    state[...] = s_all * last_decay + _dot_tn(k * jnp.exp(b_last - b), v_wide)

    o = o * lax.rsqrt(jnp.mean(o * o, axis=-1, keepdims=True) + EPS) * nw_ref[...]
    y = o * _silu(jnp.concatenate(gates, axis=0))
    for s in range(nb):
        for h in range(GLA_HEADS):
            g = s * GLA_HEADS + h
            o_ref[s, :, h * GLA_DV:(h + 1) * GLA_DV] = y[g * chunk:(g + 1) * chunk, :]

    @pl.when(c_idx == pl.num_programs(1) - 1)
    def _():
        for s in range(nb):
            for h in range(GLA_HEADS):
                s_out_ref[s, h] = state[:, group_lanes(s * GLA_HEADS + h)]


def _gla_mixer(proj, s0, wg, bg, nw, *, batch, seq_len, chunk, nb):
    nc = proj.shape[1] // chunk
    qkv_w = 2 * GLA_K + GLA_V
    return pl.pallas_call(
        functools.partial(_gla_kernel, chunk=chunk, seq_len=seq_len, nb=nb),
        out_shape=(jax.ShapeDtypeStruct((batch, nc * chunk, GLA_V), F32),
                   jax.ShapeDtypeStruct((batch, GLA_HEADS, GLA_DK, GLA_DV), F32)),
        grid=(batch // nb, nc),
        in_specs=[pl.BlockSpec((nb, chunk, qkv_w), lambda b, c: (b, c, OD_QKV // qkv_w)),
                  pl.BlockSpec((nb, chunk, GLA_V), lambda b, c: (b, c, OD_OG // GLA_V)),
                  pl.BlockSpec((nb, chunk, SMALL_W), lambda b, c: (b, c, OD_GLR // SMALL_W)),
                  pl.BlockSpec((nb, GLA_HEADS, GLA_DK, GLA_DV), lambda b, c: (b, 0, 0, 0)),
                  pl.BlockSpec((SMALL_W, GLA_K), lambda b, c: (0, 0)),
                  pl.BlockSpec((1, GLA_K), lambda b, c: (0, 0)),
                  pl.BlockSpec((1, GLA_DV), lambda b, c: (0, 0))],
        out_specs=(pl.BlockSpec((nb, chunk, GLA_V), lambda b, c: (b, c, 0)),
                   pl.BlockSpec((nb, GLA_HEADS, GLA_DK, GLA_DV), lambda b, c: (b, 0, 0, 0))),
        scratch_shapes=[pltpu.VMEM((GLA_DK, nb * GLA_HEADS * GLA_DV), F32)],
        compiler_params=_params(("parallel", "arbitrary")),
        name="gla_mixer",
    )(proj, proj, proj, s0, wg, bg, nw)


def _take_top(x, key, count):
    nr, nt = x.shape
    big = jnp.int32(2 ** 30)
    slot = lax.broadcasted_iota(jnp.int32, (count, nt), 0)

    def body(r, carry):
        x, rank, vals = carry
        m = jnp.max(x, axis=0, keepdims=True)
        first = jnp.min(jnp.where(x == m, key, big), axis=0, keepdims=True)
        hit = key == first
        return (jnp.where(hit, -jnp.inf, x), jnp.where(hit, r, rank), jnp.where(slot == r, m, vals))

    init = (x, jnp.full((nr, nt), count, jnp.int32), jnp.zeros((count, nt), F32))
    _, rank, vals = lax.fori_loop(0, count, body, init)
    return vals, rank


def _pair_candidates(v1, v2):
    k = PEER_TOPK
    nt = v1.shape[1]
    j8 = lax.broadcasted_iota(jnp.int32, (8, nt), 0)
    j16 = lax.broadcasted_iota(jnp.int32, (k, nt), 0)
    sums, pos = [v1[0:1, :] + v2], [j16]
    for i in range(1, 8):
        sums.append(jnp.where(j8 < k // (i + 1), v1[i:i + 1, :] + v2[0:8, :], -jnp.inf))
        pos.append(i * k + j8)
    sums.append(v1[8:k, :] + v2[0:1, :])
    pos.append((8 + j8) * k)
    return jnp.concatenate(sums, axis=0), jnp.concatenate(pos, axis=0)


def _sorting_network(n):
    size = 1 << (n - 1).bit_length()
    pairs = []

    def merge(lo, hi, r):
        step = r * 2
        if step < hi - lo:
            merge(lo, hi, step)
            merge(lo + r, hi, step)
            pairs.extend((i, i + r) for i in range(lo + r, hi - r, step))
        else:
            pairs.append((lo, lo + r))

    def sort(lo, hi):
        if hi - lo >= 1:
            mid = lo + (hi - lo) // 2
            sort(lo, mid)
            sort(mid + 1, hi)
            merge(lo, hi, 1)

    sort(0, size - 1)
    return [(i, j) for i, j in pairs if j < n]


def _take_top_values(x, count):
    sub = 8
    tiles = [x[i:i + sub, :] for i in range(0, x.shape[0], sub)]
    for i, j in _sorting_network(len(tiles)):
        tiles[i], tiles[j] = jnp.maximum(tiles[i], tiles[j]), jnp.minimum(tiles[i], tiles[j])
    slot = lax.broadcasted_iota(jnp.int32, (count, x.shape[1]), 0)
    vals = jnp.zeros((count, x.shape[1]), F32)
    for r in range(count):
        m = jnp.max(tiles[0], axis=0, keepdims=True)
        vals = jnp.where(slot == r, m, vals)
        pop = tiles[0] == m
        depth = min(len(tiles), count - r)
        for k in range(depth - 1):
            tiles[k] = jnp.where(pop, tiles[k + 1], tiles[k])
        tiles[depth - 1] = jnp.where(pop, -jnp.inf, tiles[depth - 1])
    return vals


def _count_rows(mask):
    return jnp.sum(jnp.where(mask, 1.0, 0.0), axis=0, keepdims=True)


def _per_first_key(sel):
    k = PEER_TOPK
    cnt = jnp.where(sel, 1.0, 0.0)
    out = [jnp.sum(cnt[0:k, :], axis=0, keepdims=True)]
    out += [jnp.sum(cnt[k + 8 * (i - 1):k + 8 * i, :], axis=0, keepdims=True) for i in range(1, 8)]
    out += [cnt[k + 56 + i:k + 57 + i, :] for i in range(k - 8)]
    return out


def _route_tables_exact(s1, s2):
    k = PEER_TOPK
    key = lax.broadcasted_iota(jnp.int32, s1.shape, 0)
    v1, rank1 = _take_top(s1, key, k)
    v2, rank2 = _take_top(s2, key, k)
    cand, pos = _pair_candidates(v1, v2)
    _, crank = _take_top(cand, pos, k)
    sel = crank < k
    z = jnp.sum(jnp.where(sel, jnp.exp(cand - (v1[0:1, :] + v2[0:1, :])), 0.0), axis=0, keepdims=True)
    n = jnp.zeros(s1.shape, F32)
    for i, n_i in enumerate(_per_first_key(sel)):
        n = jnp.where(rank1 == i, n_i, n)
    e1 = jnp.where(rank1 < k, jnp.exp(s1 - v1[0:1, :]) / z, 0.0)
    e2 = jnp.where(rank2 < k, jnp.exp(s2 - v2[0:1, :]), 0.0)
    return n, e1, rank2.astype(F32), e2


def _route_tables_distinct(s1, s2):
    k = PEER_TOPK
    v1 = _take_top_values(s1, k)
    v2 = _take_top_values(s2, k)
    in1 = s1 >= v1[k - 1:k, :]
    in2 = s2 >= v2[k - 1:k, :]
    cand, _ = _pair_candidates(v1, v2)
    cv = _take_top_values(cand, k)
    sel = cand >= cv[k - 1:k, :]
    ok = (_count_rows(in1) == k) & (_count_rows(in2) == k) & (_count_rows(sel) == k)
    z = jnp.sum(jnp.where(sel, jnp.exp(cand - (v1[0:1, :] + v2[0:1, :])), 0.0), axis=0, keepdims=True)
    n = jnp.zeros(s1.shape, F32)
    rank2 = jnp.full(s2.shape, float(k), F32)
    for i, n_i in enumerate(_per_first_key(sel)):
        n = jnp.where(s1 == v1[i:i + 1, :], n_i, n)
        rank2 = jnp.where(s2 == v2[i:i + 1, :], float(i), rank2)
    e1 = jnp.where(in1, jnp.exp(s1 - v1[0:1, :]) / z, 0.0)
    e2 = jnp.where(in2, jnp.exp(s2 - v2[0:1, :]), 0.0)
    return (n, e1, rank2, e2), ok


def _peer_route_kernel(q_ref, k1_ref, k2_ref, n_ref, e1_ref, r2_ref, e2_ref, s1_ref, s2_ref):
    q = q_ref[...]
    s1_ref[...] = _dot_nt(k1_ref[0], q[:, :PEER_HALF])
    s2_ref[...] = _dot_nt(k2_ref[0], q[:, PEER_HALF:])
    tt = q.shape[0]
    lw = min(LANES, tt)
    for lc in range(tt // lw):
        sl = slice(lc * lw, (lc + 1) * lw)

        def store(tables):
            n, e1, rank2, e2 = tables
            n_ref[0, :, sl] = n
            e1_ref[0, :, sl] = e1
            r2_ref[:, sl] = rank2.astype(BF16)
            e2_ref[:, sl] = e2.astype(BF16)

        tables, ok = _route_tables_distinct(s1_ref[:, sl], s2_ref[:, sl])
        store(tables)

        @pl.when(jnp.min(jnp.where(ok, 1, 0)) == 0)
        def _():
            store(_route_tables_exact(s1_ref[:, sl], s2_ref[:, sl]))


def _peer_route(q, keys1, keys2, tt):
    t = q.shape[0]
    out_a = jax.ShapeDtypeStruct((PEER_HEADS, PEER_NKEYS, t), F32)
    out_b = jax.ShapeDtypeStruct((PEER_HEADS * PEER_NKEYS, t), BF16)
    spec_a = pl.BlockSpec((1, PEER_NKEYS, tt), lambda i, h: (h, 0, i))
    spec_b = pl.BlockSpec((PEER_NKEYS, tt), lambda i, h: (h, i))
    key_spec = pl.BlockSpec((1, PEER_NKEYS, PEER_HALF), lambda i, h: (h, 0, 0))
    return pl.pallas_call(
        _peer_route_kernel,
        out_shape=(out_a, out_a, out_b, out_b),
        grid=(t // tt, PEER_HEADS),
        in_specs=[pl.BlockSpec((tt, PEER_DQ), lambda i, h: (i, h)), key_spec, key_spec],
        out_specs=(spec_a, spec_a, spec_b, spec_b),
        scratch_shapes=[pltpu.VMEM((PEER_NKEYS, tt), F32), pltpu.VMEM((PEER_NKEYS, tt), F32)],
        compiler_params=_params(("parallel", "arbitrary")),
        name="peer_route",
    )(q, keys1, keys2)


def _peer_dense_kernel(xa_ref, xc_ref, lnw_ref, n_ref, e1_ref, r2_in_ref, e2_in_ref, u_ref, v_ref, o_ref,
                       h_ref, pre_ref, pre2_ref, act_ref, act2_ref, acc_ref, r2_ref, e2_ref,
                       *, rows_per_step, n_blocks, n_work):
    g = pl.program_id(0)
    last = n_work - 1
    ja = jnp.minimum(g, last) % n_blocks
    jb = jnp.clip(g - 1, 0, last) % n_blocks
    jc = jnp.clip(g - 2, 0, last) % n_blocks

    @pl.when(g == 0)
    def _():
        for ref in (pre_ref, pre2_ref, act_ref, act2_ref):
            ref[...] = jnp.zeros_like(ref)

    @pl.when(ja == 0)
    def _():
        x = xa_ref[...]
        y = x * lax.rsqrt(jnp.mean(x * x, axis=-1, keepdims=True) + EPS)
        h_ref[...] = (y * lnw_ref[...]).astype(BF16)

    @pl.when(jb == 0)
    def _():
        r2_ref[...] = r2_in_ref[...]
        e2_ref[...] = e2_in_ref[...]

    @pl.when(jc == 0)
    def _():
        acc_ref[...] = jnp.zeros_like(acc_ref)

    tt = h_ref.shape[0]
    lw = min(LANES, tt)
    reps = PEER_NKEYS // BF16_ROWS

    def stages(pre_w, pre_r, act_w, act_r):
        zero = jnp.zeros((PEER_NKEYS, lw), BF16)
        mxu_width = min(256, tt)

        def stage_a(part):
            tok = slice(part * mxu_width, (part + 1) * mxu_width)
            pre_w[:, tok] = lax.dot_general(u_ref[...], h_ref[tok, :], _NT, preferred_element_type=F32)

        def stage_b(lc, aa):
            sl = slice(lc * lw, (lc + 1) * lw)
            rows = slice(aa * PEER_NKEYS, (aa + 1) * PEER_NKEYS)
            gate = zero
            for h in range(PEER_HEADS):
                n_row = jnp.broadcast_to(n_ref[h, aa:aa + 1, sl], (BF16_ROWS, lw)).astype(BF16)
                e1_row = jnp.broadcast_to(e1_ref[h, aa:aa + 1, sl], (BF16_ROWS, lw)).astype(BF16)
                n_tile = jnp.concatenate([n_row] * reps, axis=0)
                e1_tile = jnp.concatenate([e1_row] * reps, axis=0)
                keys = slice(h * PEER_NKEYS, (h + 1) * PEER_NKEYS)
                gate = gate + jnp.where(r2_ref[keys, sl] < n_tile, e2_ref[keys, sl], zero) * e1_tile
            act = _gelu_tanh(pre_r[rows, sl].astype(BF16)) * gate
            act_w[sl, rows] = act.T

        def stage_c(part):
            col = slice(part * 256, (part + 1) * 256)
            acc_ref[:, col] += jnp.dot(act_r[...], v_ref[:, col], preferred_element_type=F32)

        mxu = [functools.partial(stage_a, p) for p in range(tt // mxu_width)]
        mxu += [functools.partial(stage_c, p) for p in range(acc_ref.shape[1] // 256)]
        vpu = [functools.partial(stage_b, lc, aa) for lc in range(tt // lw) for aa in range(rows_per_step)]
        per = -(-len(vpu) // len(mxu))
        for i, piece in enumerate(mxu):
            piece()
            for tile in vpu[i * per:(i + 1) * per]:
                tile()

    @pl.when(g % 2 == 0)
    def _():
        stages(pre_ref, pre2_ref, act2_ref, act_ref)

    @pl.when(g % 2 == 1)
    def _():
        stages(pre2_ref, pre_ref, act_ref, act2_ref)

    @pl.when((jc == n_blocks - 1) & (g >= 2))
    def _():
        o_ref[...] = xc_ref[...] + acc_ref[...]


def _peer_dense(x, lnw, route, u, v, tt, rows_per_step):
    t, d = x.shape
    eb = rows_per_step * PEER_NKEYS
    n_blocks = PEER_N // eb
    n_work = (t // tt) * n_blocks

    def item(lag):
        def at(g):
            w = jnp.clip(g - lag, 0, n_work - 1)
            return w // n_blocks, w % n_blocks
        return at

    a, b, c = item(0), item(1), item(2)
    spec_rows = pl.BlockSpec((PEER_HEADS, rows_per_step, tt), lambda g: (0, b(g)[1], b(g)[0]))
    spec_keys = pl.BlockSpec((PEER_HEADS * PEER_NKEYS, tt), lambda g: (0, b(g)[0]))
    return pl.pallas_call(
        functools.partial(_peer_dense_kernel, rows_per_step=rows_per_step, n_blocks=n_blocks, n_work=n_work),
        out_shape=jax.ShapeDtypeStruct((t, d), F32),
        grid=(n_work + 2,),
        in_specs=[pl.BlockSpec((tt, d), lambda g: (a(g)[0], 0)),
                  pl.BlockSpec((tt, d), lambda g: (c(g)[0], 0)),
                  pl.BlockSpec((1, d), lambda g: (0, 0)),
                  spec_rows, spec_rows, spec_keys, spec_keys,
                  pl.BlockSpec((eb, d), lambda g: (a(g)[1], 0)),
                  pl.BlockSpec((eb, d), lambda g: (c(g)[1], 0))],
        out_specs=pl.BlockSpec((tt, d), lambda g: (c(g)[0], 0)),
        scratch_shapes=[pltpu.VMEM((tt, d), BF16), pltpu.VMEM((eb, tt), F32), pltpu.VMEM((eb, tt), F32),
                        pltpu.VMEM((tt, eb), BF16), pltpu.VMEM((tt, eb), BF16), pltpu.VMEM((tt, d), F32),
                        pltpu.VMEM((PEER_HEADS * PEER_NKEYS, tt), BF16),
                        pltpu.VMEM((PEER_HEADS * PEER_NKEYS, tt), BF16)],
        compiler_params=_params(("arbitrary",)),
        name="peer_dense",
    )(x, x, lnw.reshape(1, d), *route, u, v)


def _peer_ffn(x, lnw, wq, keys1, keys2, u, v):
    tt = min(512, x.shape[0])
    q = _norm_matmul(x, lnw, wq)
    route = _peer_route(q, keys1, keys2, tt)
    return _peer_dense(x, lnw, route, u, v, tt, rows_per_step=16)


def _pad_lanes(vec, offset):
    return jnp.zeros((SMALL_W,), F32).at[offset:offset + vec.shape[0]].set(vec.astype(F32))


def _conv_history(buf):
    return jnp.pad(buf, ((0, 0), (CONV_PAD - (CONV_W - 1), 0), (0, 0)))


def kernel(x_prompt, x_sample, state_dn_conv, state_dn, state_ssm_conv, state_ssm, state_gla, ln_mix, ln_ffn, ln_final, w_in_even, w_out_even, dn_conv_w, dn_a_log, dn_dt_bias, dn_norm_w, ssm_conv_w, ssm_conv_b, ssm_a_log, ssm_dt_bias, ssm_d, ssm_norm_w, w_in_odd, gla_w_gate2, gla_b_gate, gla_norm_w, w_out_odd, peer_w_q, peer_keys1, peer_keys2, peer_u, peer_v):
    we = w_in_even[0]
    o = 0
    cuts = {}
    for name, size in (("qkv", DN_CONV_DIM), ("a", DN_HEADS), ("b", DN_HEADS), ("gate", DN_V),
                       ("z", SSM_INNER), ("xbc", SSM_CONV_DIM), ("dt", SSM_HEADS)):
        cuts[name] = we[:, o:o + size]
        o += size
    small_w = jnp.concatenate([cuts["a"], cuts["b"], cuts["dt"]], axis=1)
    small_w = jnp.pad(small_w, ((0, 0), (0, SMALL_W - small_w.shape[1])))
    w_even = jnp.concatenate([cuts["qkv"], cuts["gate"], cuts["xbc"], cuts["z"], small_w], axis=1).astype(BF16)
    wo = w_in_odd[0]
    qkv_w = 2 * GLA_K + GLA_V
    glr_w = jnp.pad(wo[:, qkv_w:qkv_w + GLA_LR], ((0, 0), (0, SMALL_W - GLA_LR)))
    w_odd = jnp.concatenate([wo[:, :qkv_w], wo[:, qkv_w + GLA_LR:], glr_w], axis=1).astype(BF16)
    w_out_dn = w_out_even[0, :DN_V].astype(BF16)
    w_out_ssm = w_out_even[0, DN_V:].astype(BF16)
    w_out_gla = w_out_odd[0].astype(BF16)
    wg = jnp.pad(gla_w_gate2[0], ((0, SMALL_W - GLA_LR), (0, 0))).astype(BF16)
    wq = peer_w_q.astype(BF16)
    k1 = peer_keys1.astype(BF16)
    k2 = peer_keys2.astype(BF16)
    pu = [peer_u[i].astype(BF16) for i in range(peer_u.shape[0])]
    pv = [peer_v[i].astype(BF16) for i in range(peer_v.shape[0])]

    zero8 = jnp.zeros((8 - 2, SMALL_W), F32)
    dn_par = jnp.concatenate([_pad_lanes(dn_a_log[0], 0)[None], _pad_lanes(dn_dt_bias[0], 0)[None], zero8])
    lane0 = 2 * DN_HEADS
    ssm_par = jnp.concatenate([_pad_lanes(ssm_a_log[0], lane0)[None], _pad_lanes(ssm_dt_bias[0], lane0)[None],
                               _pad_lanes(ssm_d[0], lane0)[None], zero8[1:]])

    def trunk(x, batch, seq_len, conv_dn0, s_dn0, conv_ssm0, s_ssm0, s_gla0):
        chunk = PROMPT_CHUNK if seq_len >= PROMPT_CHUNK else STEP_CHUNK
        nc = -(-seq_len // chunk)
        padded = nc * chunk != seq_len

        def stacked(heads):
            nb = max(1, min(batch, 256 // (heads * chunk)))
            while batch % nb:
                nb -= 1
            return nb

        dn_nb, ssd_nb = stacked(DN_HEADS), stacked(SSM_HEADS)

        def to_seq(a):
            if not padded:
                return a
            a = a.reshape(batch, seq_len, a.shape[-1])
            return jnp.pad(a, ((0, 0), (0, nc * chunk - seq_len), (0, 0))).reshape(batch * nc * chunk, -1)

        def from_seq(a):
            if not padded:
                return a
            return a.reshape(batch, nc * chunk, a.shape[-1])[:, :seq_len].reshape(batch * seq_len, -1)

        def last_rows(buf, proj, col, width):
            tail = proj.reshape(batch, seq_len, proj.shape[-1])[:, -min(seq_len, CONV_W - 1):, col:col + width]
            return jnp.concatenate([buf, tail], axis=1)[:, -(CONV_W - 1):]

        seq = dict(batch=batch, seq_len=seq_len, chunk=chunk)

        proj = _norm_matmul(x, ln_mix[0], w_even)
        proj_seq = to_seq(proj)
        o_dn, s_dn = _dn_mixer(proj_seq.reshape(batch, nc * chunk, EV_WIDTH), _conv_history(conv_dn0[0]),
                               s_dn0[0], dn_conv_w[0], dn_par, dn_norm_w[0].reshape(1, DN_DV),
                               nb=dn_nb, **seq)
        o_dn = o_dn.reshape(batch * nc * chunk, DN_V)
        y_ssm, s_ssm = _ssd_mixer(proj_seq.reshape(batch, nc * chunk, EV_WIDTH), _conv_history(conv_ssm0[0]),
                                  s_ssm0[0], ssm_conv_w[0], ssm_conv_b[0].reshape(1, SSM_CONV_DIM), ssm_par,
                                  ssm_norm_w[0].reshape(1, SSM_INNER), nb=ssd_nb, **seq)
        y_ssm = y_ssm.reshape(batch * nc * chunk, SSM_INNER)
        conv_dn = last_rows(conv_dn0[0], proj, EV_QKV, DN_CONV_DIM)
        conv_ssm = last_rows(conv_ssm0[0], proj, EV_XBC, SSM_CONV_DIM)
        x = _proj_residual(x, [from_seq(o_dn), from_seq(y_ssm)], [w_out_dn, w_out_ssm])
        x = _peer_ffn(x, ln_ffn[0], wq[0], k1[0], k2[0], pu[0], pv[0])

        proj = _norm_matmul(x, ln_mix[1], w_odd)
        o_gla, s_gla = _gla_mixer(to_seq(proj).reshape(batch, nc * chunk, OD_WIDTH), s_gla0[0], wg,
                                  gla_b_gate[0].reshape(1, GLA_K), gla_norm_w[0].reshape(1, GLA_DV),
                                  nb=dn_nb, **seq)
        x = _proj_residual(x, [from_seq(o_gla.reshape(batch * nc * chunk, GLA_V))], [w_out_gla])
        x = _peer_ffn(x, ln_ffn[1], wq[1], k1[1], k2[1], pu[1], pv[1])

        y = _final_norm(x, ln_final).reshape(batch, seq_len, D_MODEL)
        return y, conv_dn[None], s_dn[None], conv_ssm[None], s_ssm[None], s_gla[None]

    bp, lp = x_prompt.shape[:2]
    bs, ls = x_sample.shape[:2]

    def zeros_like_state(s):
        return jnp.zeros((s.shape[0], bp) + s.shape[2:], s.dtype)

    prompt = trunk(x_prompt.reshape(bp * lp, D_MODEL), bp, lp, zeros_like_state(state_dn_conv),
                   zeros_like_state(state_dn), zeros_like_state(state_ssm_conv),
                   zeros_like_state(state_ssm), zeros_like_state(state_gla))
    sample = trunk(x_sample.reshape(bs * ls, D_MODEL), bs, ls, state_dn_conv, state_dn,
                   state_ssm_conv, state_ssm, state_gla)
    return (prompt[0], sample[0]) + prompt[1:] + sample[1:]
```

```python
import functools

import jax
import jax.numpy as jnp
from jax import lax
from jax.experimental import pallas as pl
from jax.experimental.pallas import tpu as pltpu

F32 = jnp.float32
BF16 = jnp.bfloat16
EPS = 1e-6

D_MODEL = 1024
CONV_W = 4
CONV_PAD = 8
PROMPT_CHUNK = 64
STEP_CHUNK = 8

DN_HEADS, DN_DK, DN_DV = 4, 128, 128
DN_QK = DN_HEADS * DN_DK
DN_V = DN_HEADS * DN_DV
DN_CONV_DIM = 2 * DN_QK + DN_V
SSM_HEADS, SSM_P, SSM_GROUPS, SSM_N = 8, 64, 2, 128
SSM_INNER = SSM_HEADS * SSM_P
SSM_CONV_DIM = SSM_INNER + 2 * SSM_GROUPS * SSM_N
GLA_HEADS, GLA_DK, GLA_DV, GLA_LR = 4, 128, 256, 16
GLA_K = GLA_HEADS * GLA_DK
GLA_V = GLA_HEADS * GLA_DV
GLA_NORMALIZER = 16.0
PEER_HEADS, PEER_DQ, PEER_NKEYS, PEER_TOPK = 8, 256, 128, 16
PEER_HALF = PEER_DQ // 2
PEER_N = PEER_NKEYS * PEER_NKEYS
LANES = 128
BF16_ROWS = 16
SMALL_W = LANES

EV_QKV, EV_GATE, EV_XBC, EV_Z, EV_SMALL = 0, 1536, 2048, 3072, 3584
EV_WIDTH = EV_SMALL + SMALL_W
OD_QKV, OD_OG, OD_GLR = 0, 2048, 3072
OD_WIDTH = OD_GLR + SMALL_W

VMEM_LIMIT = 56 * 1024 * 1024

_NT = (((1,), (1,)), ((), ()))
_TN = (((0,), (0,)), ((), ()))


def _dot(a, b):
    return jnp.dot(a.astype(BF16), b.astype(BF16), preferred_element_type=F32)


def _dot_nt(a, b):
    return lax.dot_general(a.astype(BF16), b.astype(BF16), _NT, preferred_element_type=F32)


def _dot_tn(a, b):
    return lax.dot_general(a.astype(BF16), b.astype(BF16), _TN, preferred_element_type=F32)


def _params(sem):
    return pltpu.CompilerParams(dimension_semantics=sem, vmem_limit_bytes=VMEM_LIMIT)


def _silu(x):
    return x * jax.nn.sigmoid(x)


def _gelu_tanh(x):
    c0 = 0.7978845608028654
    hx = 0.5 * x
    return hx + hx * jnp.tanh(x * (c0 + (c0 * 0.044715) * (x * x)))


def _softplus(x):
    return jnp.maximum(x, 0.0) + jnp.log(1.0 + jnp.exp(-jnp.abs(x)))


def _cumsum_rows(x):
    n = x.shape[0]
    row = lax.broadcasted_iota(jnp.int32, x.shape, 0)
    s = 1
    while s < n:
        x = x + jnp.where(row >= s, pltpu.roll(x, s, axis=0), 0.0)
        s *= 2
    return x


def _col_to_row(col, eye):
    return jnp.sum(jnp.where(eye, col, 0.0), axis=0, keepdims=True)


def _row_to_col(row, eye):
    return jnp.sum(jnp.where(eye, row, 0.0), axis=1, keepdims=True)


def _tri_masks(n):
    ri = lax.broadcasted_iota(jnp.int32, (n, n), 0)
    ci = lax.broadcasted_iota(jnp.int32, (n, n), 1)
    return ri, ci, ri == ci, ri >= ci, ri > ci


def _valid_rows(chunk, seq_len, shape):
    if seq_len % chunk == 0:
        return None
    pos = pl.program_id(1) * chunk + lax.broadcasted_iota(jnp.int32, shape, 0)
    return pos < seq_len


def _causal_conv(x_ref, xext, cw_ref, chunk):
    x = x_ref[...]
    xext[CONV_PAD:CONV_PAD + chunk, :] = x
    cw = cw_ref[...]
    pre = cw[CONV_W - 1:CONV_W, :] * x
    for j in range(1, CONV_W):
        pre = pre + cw[CONV_W - 1 - j:CONV_W - j, :] * xext[CONV_PAD - j:CONV_PAD - j + chunk, :]
    xext[0:CONV_PAD, :] = xext[chunk:chunk + CONV_PAD, :]
    return pre


def _norm_matmul_kernel(x_ref, lnw_ref, w_ref, o_ref):
    x = x_ref[...]
    y = x * lax.rsqrt(jnp.mean(x * x, axis=-1, keepdims=True) + EPS)
    h = (y * lnw_ref[...]).astype(BF16)
    o_ref[...] = jnp.dot(h, w_ref[...], preferred_element_type=F32)


def _norm_matmul(x, lnw, w):
    t, d = x.shape
    n = w.shape[1]
    tm = min(512, t)
    return pl.pallas_call(
        _norm_matmul_kernel,
        out_shape=jax.ShapeDtypeStruct((t, n), F32),
        grid=(t // tm,),
        in_specs=[pl.BlockSpec((tm, d), lambda i: (i, 0)),
                  pl.BlockSpec((1, d), lambda i: (0, 0)),
                  pl.BlockSpec((d, n), lambda i: (0, 0))],
        out_specs=pl.BlockSpec((tm, n), lambda i: (i, 0)),
        compiler_params=_params(("parallel",)),
        name="norm_matmul",
    )(x, lnw.reshape(1, d), w)


def _proj_residual_kernel(*refs, n_in):
    x_ref, o_ref = refs[0], refs[-1]
    acc = x_ref[...]
    for a_ref, w_ref in zip(refs[1:1 + n_in], refs[1 + n_in:1 + 2 * n_in]):
        acc = acc + jnp.dot(a_ref[...].astype(BF16), w_ref[...], preferred_element_type=F32)
    o_ref[...] = acc


def _proj_residual(x, acts, weights):
    t, d = x.shape
    tm = min(512, t)
    n_in = len(acts)
    in_specs = [pl.BlockSpec((tm, d), lambda i: (i, 0))]
    in_specs += [pl.BlockSpec((tm, a.shape[1]), lambda i: (i, 0)) for a in acts]
    in_specs += [pl.BlockSpec(w.shape, lambda i: (0, 0)) for w in weights]
    return pl.pallas_call(
        functools.partial(_proj_residual_kernel, n_in=n_in),
        out_shape=jax.ShapeDtypeStruct((t, d), F32),
        grid=(t // tm,),
        in_specs=in_specs,
        out_specs=pl.BlockSpec((tm, d), lambda i: (i, 0)),
        compiler_params=_params(("parallel",)),
        name="proj_residual",
    )(x, *acts, *weights)


def _unit_lower_inverse(m, ri, ci, eye, block):
    t = jnp.where(eye, 1.0, 0.0)
    level = 0
    while (1 << level) < block:
        pair = ((ri >> (level + 1)) == (ci >> (level + 1))) & ((ri >> level) != (ci >> level))
        c = jnp.where(pair, m, 0.0)
        t = t - c if level == 0 else t - _dot(_dot(t, c), t)
        level += 1
    return t


def _dn_kernel(qkv_ref, gate_ref, small_ref, conv0_ref, s0_ref, cw_ref, par_ref, nw_ref,
               o_ref, s_out_ref, xext, state, *, chunk, seq_len, nb):
    c_idx = pl.program_id(1)
    groups = nb * DN_HEADS
    log2c = chunk.bit_length() - 1
    group_lanes = lambda g: slice(g * DN_DV, (g + 1) * DN_DV)

    @pl.when(c_idx == 0)
    def _():
        for s in range(nb):
            xext[s, 0:CONV_PAD, :] = conv0_ref[s]
            for h in range(DN_HEADS):
                state[:, group_lanes(s * DN_HEADS + h)] = s0_ref[s, h]

    par = par_ref[...]
    qs, ks, vs, gcs, betas, gates = [], [], [], [], [], []
    for s in range(nb):
        act = _silu(_causal_conv(qkv_ref.at[s], xext.at[s], cw_ref, chunk))
        small = small_ref[s]
        g_all = -jnp.exp(par[0:1, :]) * _softplus(small + par[1:2, :])
        beta_all = jax.nn.sigmoid(small)
        valid = _valid_rows(chunk, seq_len, small.shape)
        if valid is not None:
            g_all = jnp.where(valid, g_all, 0.0)
            beta_all = jnp.where(valid, beta_all, 0.0)
        gc_all = _cumsum_rows(g_all)
        for h in range(DN_HEADS):
            xq = act[:, h * DN_DK:(h + 1) * DN_DK]
            xk = act[:, DN_QK + h * DN_DK:DN_QK + (h + 1) * DN_DK]
            qs.append(xq * lax.rsqrt(jnp.sum(xq * xq, axis=-1, keepdims=True) + EPS) * (DN_DK ** -0.5))
            ks.append(xk * lax.rsqrt(jnp.sum(xk * xk, axis=-1, keepdims=True) + EPS))
            vs.append(act[:, 2 * DN_QK + h * DN_DV:2 * DN_QK + (h + 1) * DN_DV])
            gcs.append(gc_all[:, h:h + 1])
            betas.append(beta_all[:, DN_HEADS + h:DN_HEADS + h + 1])
            gates.append(gate_ref[s, :, h * DN_DV:(h + 1) * DN_DV])
    q, k, v = (jnp.concatenate(t, axis=0) for t in (qs, ks, vs))
    gc, beta = (jnp.concatenate(t, axis=0) for t in (gcs, betas))
    gc_last = jnp.concatenate([jnp.broadcast_to(t[chunk - 1:chunk, :], (chunk, 1)) for t in gcs], axis=0)
    last_decay = jnp.concatenate(
        [jnp.broadcast_to(jnp.exp(t[chunk - 1:chunk, :]), (1, DN_DV)) for t in gcs], axis=1)

    rows = groups * chunk
    ri, ci, eye, lower, strict_lower = _tri_masks(rows)
    same = (ri >> log2c) == (ci >> log2c)
    causal = same & lower
    strict = same & strict_lower
    decay = jnp.where(causal, jnp.exp(gc - _col_to_row(gc, eye)), 0.0)
    kb = k * beta
    eg = jnp.exp(gc)
    m = jnp.where(strict, _dot_nt(kb, k) * decay, 0.0)
    t_inv = _unit_lower_inverse(m, ri, ci, eye, chunk)
    sol = _dot(t_inv, jnp.concatenate([v * beta, kb * eg], axis=1))
    u, w = sol[:, :DN_DV], sol[:, DN_DV:]
    qk = jnp.where(causal, _dot_nt(q, k) * decay, 0.0)

    def own_block(x):
        return jnp.concatenate(
            [x[g * chunk:(g + 1) * chunk, group_lanes(g)] for g in range(groups)], axis=0)

    s_all = state[...]
    v_new = u - own_block(_dot(w, s_all))
    o = own_block(_dot(q * eg, s_all)) + _dot(qk, v_new)
    wide = (rows, groups * DN_DV)
    row_group = lax.broadcasted_iota(jnp.int32, wide, 0) >> log2c
    lane_group = lax.broadcasted_iota(jnp.int32, wide, 1) // DN_DV
    v_wide = jnp.where(row_group == lane_group, jnp.concatenate([v_new] * groups, axis=1), 0.0)
    state[...] = s_all * last_decay + _dot_tn(k * jnp.exp(gc_last - gc), v_wide)

    o = o * lax.rsqrt(jnp.mean(o * o, axis=-1, keepdims=True) + EPS) * nw_ref[...]
    y = o * _silu(jnp.concatenate(gates, axis=0))
    for s in range(nb):
        for h in range(DN_HEADS):
            g = s * DN_HEADS + h
            o_ref[s, :, h * DN_DV:(h + 1) * DN_DV] = y[g * chunk:(g + 1) * chunk, :]

    @pl.when(c_idx == pl.num_programs(1) - 1)
    def _():
        for s in range(nb):
            for h in range(DN_HEADS):
                s_out_ref[s, h] = state[:, group_lanes(s * DN_HEADS + h)]


def _dn_mixer(proj, conv0, s0, cw, par, nw, *, batch, seq_len, chunk, nb):
    nc = proj.shape[1] // chunk
    return pl.pallas_call(
        functools.partial(_dn_kernel, chunk=chunk, seq_len=seq_len, nb=nb),
        out_shape=(jax.ShapeDtypeStruct((batch, nc * chunk, DN_V), F32),
                   jax.ShapeDtypeStruct((batch, DN_HEADS, DN_DK, DN_DV), F32)),
        grid=(batch // nb, nc),
        in_specs=[pl.BlockSpec((nb, chunk, DN_CONV_DIM), lambda b, c: (b, c, EV_QKV // DN_CONV_DIM)),
                  pl.BlockSpec((nb, chunk, DN_V), lambda b, c: (b, c, EV_GATE // DN_V)),
                  pl.BlockSpec((nb, chunk, SMALL_W), lambda b, c: (b, c, EV_SMALL // SMALL_W)),
                  pl.BlockSpec((nb, CONV_PAD, DN_CONV_DIM), lambda b, c: (b, 0, 0)),
                  pl.BlockSpec((nb, DN_HEADS, DN_DK, DN_DV), lambda b, c: (b, 0, 0, 0)),
                  pl.BlockSpec((CONV_W, DN_CONV_DIM), lambda b, c: (0, 0)),
                  pl.BlockSpec((8, SMALL_W), lambda b, c: (0, 0)),
                  pl.BlockSpec((1, DN_DV), lambda b, c: (0, 0))],
        out_specs=(pl.BlockSpec((nb, chunk, DN_V), lambda b, c: (b, c, 0)),
                   pl.BlockSpec((nb, DN_HEADS, DN_DK, DN_DV), lambda b, c: (b, 0, 0, 0))),
        scratch_shapes=[pltpu.VMEM((nb, chunk + CONV_PAD, DN_CONV_DIM), F32),
                        pltpu.VMEM((DN_DK, nb * DN_HEADS * DN_DV), F32)],
        compiler_params=_params(("parallel", "arbitrary")),
        name="dn_mixer",
    )(proj, proj, proj, conv0, s0, cw, par, nw)


def _ssd_kernel(xbc_ref, z_ref, small_ref, conv0_ref, s0_ref, cw_ref, cb_ref, par_ref, nw_ref,
                y_ref, s_out_ref, xext, state, ybuf, *, chunk, seq_len, nb):
    c_idx = pl.program_id(1)
    groups = nb * SSM_HEADS
    log2c = chunk.bit_length() - 1
    group_rows = lambda g: slice(g * SSM_P, (g + 1) * SSM_P)

    @pl.when(c_idx == 0)
    def _():
        for s in range(nb):
            xext[s, 0:CONV_PAD, :] = conv0_ref[s]
            for h in range(SSM_HEADS):
                state[group_rows(s * SSM_HEADS + h), :] = s0_ref[s, h]

    par = par_ref[...]
    gw = SSM_GROUPS * SSM_N
    heads_per_group = SSM_HEADS // SSM_GROUPS
    xs, bms, cms, dts, acs, ds = [], [], [], [], [], []
    for s in range(nb):
        xbc = _silu(_causal_conv(xbc_ref.at[s], xext.at[s], cw_ref, chunk) + cb_ref[...])
        small = small_ref[s]
        dt_all = _softplus(small + par[1:2, :])
        valid = _valid_rows(chunk, seq_len, small.shape)
        if valid is not None:
            dt_all = jnp.where(valid, dt_all, 0.0)
        ac_all = _cumsum_rows(dt_all * -jnp.exp(par[0:1, :]))
        for h in range(SSM_HEADS):
            g = h // heads_per_group
            lane = 2 * DN_HEADS + h
            xs.append(xbc[:, h * SSM_P:(h + 1) * SSM_P])
            bms.append(xbc[:, SSM_INNER + g * SSM_N:SSM_INNER + (g + 1) * SSM_N])
            cms.append(xbc[:, SSM_INNER + gw + g * SSM_N:SSM_INNER + gw + (g + 1) * SSM_N])
            dts.append(dt_all[:, lane:lane + 1])
            acs.append(ac_all[:, lane:lane + 1])
            ds.append(jnp.broadcast_to(par[2:3, lane:lane + 1], (chunk, 1)))
    per_stack = max(1, min(groups, 256 // chunk))
    rows = per_stack * chunk
    ri, ci, eye, lower, strict_lower = _tri_masks(rows)
    causal = ((ri >> log2c) == (ci >> log2c)) & lower
    wide = (rows, per_stack * SSM_P)
    row_group = lax.broadcasted_iota(jnp.int32, wide, 0) >> log2c
    lane_group = lax.broadcasted_iota(jnp.int32, wide, 1) // SSM_P
    y_groups = []
    for lo in range(0, groups, per_stack):
        part = slice(lo, lo + per_stack)
        x, bm, cm = (jnp.concatenate(t[part], axis=0) for t in (xs, bms, cms))
        dt, ac, d_skip = (jnp.concatenate(t[part], axis=0) for t in (dts, acs, ds))
        ac_last = jnp.concatenate(
            [jnp.broadcast_to(t[chunk - 1:chunk, :], (chunk, 1)) for t in acs[part]], axis=0)
        last_decay = jnp.concatenate(
            [jnp.broadcast_to(jnp.exp(t[chunk - 1:chunk, :]), (SSM_P, 1)) for t in acs[part]], axis=0)
        decay = jnp.where(causal, jnp.exp(ac - _col_to_row(ac, eye)), 0.0)
        xdt = x * dt
        state_rows = slice(lo * SSM_P, (lo + per_stack) * SSM_P)
        s_all = state[state_rows, :]
        off = _dot_nt(cm * jnp.exp(ac), s_all)
        y = _dot(_dot_nt(cm, bm) * decay, xdt) + jnp.concatenate(
            [off[g * chunk:(g + 1) * chunk, group_rows(g)] for g in range(per_stack)], axis=0)
        xdt_pair = jnp.concatenate([xdt, xdt], axis=1)
        xdt_wide = jnp.where(row_group == lane_group,
                             jnp.concatenate([xdt_pair] * (per_stack // 2), axis=1), 0.0)
        state[state_rows, :] = s_all * last_decay + _dot_tn(xdt_wide, bm * jnp.exp(ac_last - ac))
        y = y + d_skip * x
        y_groups += [y[g * chunk:(g + 1) * chunk, :] for g in range(per_stack)]

    nw = nw_ref[...]
    gn = SSM_INNER // SSM_GROUPS
    for s in range(nb):
        for h in range(SSM_HEADS):
            ybuf[:, h * SSM_P:(h + 1) * SSM_P] = y_groups[s * SSM_HEADS + h]
        ys = ybuf[...] * _silu(z_ref[s])
        for g in range(SSM_GROUPS):
            yg = ys[:, g * gn:(g + 1) * gn]
            yg = yg * lax.rsqrt(jnp.mean(yg * yg, axis=-1, keepdims=True) + EPS)
            y_ref[s, :, g * gn:(g + 1) * gn] = yg * nw[:, g * gn:(g + 1) * gn]

    @pl.when(c_idx == pl.num_programs(1) - 1)
    def _():
        for s in range(nb):
            for h in range(SSM_HEADS):
                s_out_ref[s, h] = state[group_rows(s * SSM_HEADS + h), :]


def _ssd_mixer(proj, conv0, s0, cw, cb, par, nw, *, batch, seq_len, chunk, nb):
    nc = proj.shape[1] // chunk
    return pl.pallas_call(
        functools.partial(_ssd_kernel, chunk=chunk, seq_len=seq_len, nb=nb),
        out_shape=(jax.ShapeDtypeStruct((batch, nc * chunk, SSM_INNER), F32),
                   jax.ShapeDtypeStruct((batch, SSM_HEADS, SSM_P, SSM_N), F32)),
        grid=(batch // nb, nc),
        in_specs=[pl.BlockSpec((nb, chunk, SSM_CONV_DIM), lambda b, c: (b, c, EV_XBC // SSM_CONV_DIM)),
                  pl.BlockSpec((nb, chunk, SSM_INNER), lambda b, c: (b, c, EV_Z // SSM_INNER)),
                  pl.BlockSpec((nb, chunk, SMALL_W), lambda b, c: (b, c, EV_SMALL // SMALL_W)),
                  pl.BlockSpec((nb, CONV_PAD, SSM_CONV_DIM), lambda b, c: (b, 0, 0)),
                  pl.BlockSpec((nb, SSM_HEADS, SSM_P, SSM_N), lambda b, c: (b, 0, 0, 0)),
                  pl.BlockSpec((CONV_W, SSM_CONV_DIM), lambda b, c: (0, 0)),
                  pl.BlockSpec((1, SSM_CONV_DIM), lambda b, c: (0, 0)),
                  pl.BlockSpec((8, SMALL_W), lambda b, c: (0, 0)),
                  pl.BlockSpec((1, SSM_INNER), lambda b, c: (0, 0))],
        out_specs=(pl.BlockSpec((nb, chunk, SSM_INNER), lambda b, c: (b, c, 0)),
                   pl.BlockSpec((nb, SSM_HEADS, SSM_P, SSM_N), lambda b, c: (b, 0, 0, 0))),
        scratch_shapes=[pltpu.VMEM((nb, chunk + CONV_PAD, SSM_CONV_DIM), F32),
                        pltpu.VMEM((nb * SSM_HEADS * SSM_P, SSM_N), F32),
                        pltpu.VMEM((chunk, SSM_INNER), F32)],
        compiler_params=_params(("parallel", "arbitrary")),
        name="ssd_mixer",
    )(proj, proj, proj, conv0, s0, cw, cb, par, nw)


def _gla_kernel(qkv_ref, og_ref, glr_ref, s0_ref, wg_ref, bg_ref, nw_ref,
                o_ref, s_out_ref, state, *, chunk, seq_len, nb):
    c_idx = pl.program_id(1)
    groups = nb * GLA_HEADS
    log2c = chunk.bit_length() - 1
    group_lanes = lambda g: slice(g * GLA_DV, (g + 1) * GLA_DV)

    @pl.when(c_idx == 0)
    def _():
        for s in range(nb):
            for h in range(GLA_HEADS):
                state[:, group_lanes(s * GLA_HEADS + h)] = s0_ref[s, h]

    qs, ks, vs, bs, gates = [], [], [], [], []
    for s in range(nb):
        x = bg_ref[...] + jnp.dot(glr_ref[s].astype(BF16), wg_ref[...], preferred_element_type=F32)
        gk = (jnp.minimum(x, 0.0) - jnp.log(1.0 + jnp.exp(-jnp.abs(x)))) / GLA_NORMALIZER
        valid = _valid_rows(chunk, seq_len, gk.shape)
        if valid is not None:
            gk = jnp.where(valid, gk, 0.0)
        b_all = _cumsum_rows(gk)
        for h in range(GLA_HEADS):
            qs.append(qkv_ref[s, :, h * GLA_DK:(h + 1) * GLA_DK] * (GLA_DK ** -0.5))
            ks.append(qkv_ref[s, :, GLA_K + h * GLA_DK:GLA_K + (h + 1) * GLA_DK])
            vs.append(qkv_ref[s, :, 2 * GLA_K + h * GLA_DV:2 * GLA_K + (h + 1) * GLA_DV])
            bs.append(b_all[:, h * GLA_DK:(h + 1) * GLA_DK])
            gates.append(og_ref[s, :, h * GLA_DV:(h + 1) * GLA_DV])
    q, k, b = (jnp.concatenate(t, axis=0) for t in (qs, ks, bs))
    v = jnp.concatenate(vs, axis=0)
    b_last = jnp.concatenate([jnp.broadcast_to(t[chunk - 1:chunk, :], (chunk, GLA_DK)) for t in bs], axis=0)
    eye_k = _tri_masks(GLA_DK)[2]
    last_decay = jnp.concatenate(
        [jnp.broadcast_to(_row_to_col(jnp.exp(t[chunk - 1:chunk, :]), eye_k), (GLA_DK, GLA_DV)) for t in bs],
        axis=1)

    rows = groups * chunk
    ri, ci, eye, lower, strict_lower = _tri_masks(rows)
    causal = ((ri >> log2c) == (ci >> log2c)) & lower
    qe = q * jnp.exp(b)
    att = jnp.where(causal, _dot_nt(qe, k * jnp.exp(-b)), 0.0)

    s_all = state[...]
    qs_all = _dot(qe, s_all)
    o = jnp.concatenate([qs_all[g * chunk:(g + 1) * chunk, group_lanes(g)] for g in range(groups)], axis=0)
    o = o + _dot(att, v)
    wide = (rows, groups * GLA_DV)
    row_group = lax.broadcasted_iota(jnp.int32, wide, 0) >> log2c
    lane_group = lax.broadcasted_iota(jnp.int32, wide, 1) // GLA_DV
    v_wide = jnp.where(row_group == lane_group, jnp.concatenate([v] * groups, axis=1), 0.0)
    state[...] = s_all * last_decay + _dot_tn(k * jnp.exp(b_last - b), v_wide)

    o = o * lax.rsqrt(jnp.mean(o * o, axis=-1, keepdims=True) + EPS) * nw_ref[...]
    y = o * _silu(jnp.concatenate(gates, axis=0))
    for s in range(nb):
        for h in range(GLA_HEADS):
            g = s * GLA_HEADS + h
            o_ref[s, :, h * GLA_DV:(h + 1) * GLA_DV] = y[g * chunk:(g + 1) * chunk, :]

    @pl.when(c_idx == pl.num_programs(1) - 1)
    def _():
        for s in range(nb):
            for h in range(GLA_HEADS):
                s_out_ref[s, h] = state[:, group_lanes(s * GLA_HEADS + h)]


def _gla_mixer(proj, s0, wg, bg, nw, *, batch, seq_len, chunk, nb):
    nc = proj.shape[1] // chunk
    qkv_w = 2 * GLA_K + GLA_V
    return pl.pallas_call(
        functools.partial(_gla_kernel, chunk=chunk, seq_len=seq_len, nb=nb),
        out_shape=(jax.ShapeDtypeStruct((batch, nc * chunk, GLA_V), F32),
                   jax.ShapeDtypeStruct((batch, GLA_HEADS, GLA_DK, GLA_DV), F32)),
        grid=(batch // nb, nc),
        in_specs=[pl.BlockSpec((nb, chunk, qkv_w), lambda b, c: (b, c, OD_QKV // qkv_w)),
                  pl.BlockSpec((nb, chunk, GLA_V), lambda b, c: (b, c, OD_OG // GLA_V)),
                  pl.BlockSpec((nb, chunk, SMALL_W), lambda b, c: (b, c, OD_GLR // SMALL_W)),
                  pl.BlockSpec((nb, GLA_HEADS, GLA_DK, GLA_DV), lambda b, c: (b, 0, 0, 0)),
                  pl.BlockSpec((SMALL_W, GLA_K), lambda b, c: (0, 0)),
                  pl.BlockSpec((1, GLA_K), lambda b, c: (0, 0)),
                  pl.BlockSpec((1, GLA_DV), lambda b, c: (0, 0))],
        out_specs=(pl.BlockSpec((nb, chunk, GLA_V), lambda b, c: (b, c, 0)),
                   pl.BlockSpec((nb, GLA_HEADS, GLA_DK, GLA_DV), lambda b, c: (b, 0, 0, 0))),
        scratch_shapes=[pltpu.VMEM((GLA_DK, nb * GLA_HEADS * GLA_DV), F32)],
        compiler_params=_params(("parallel", "arbitrary")),
        name="gla_mixer",
    )(proj, proj, proj, s0, wg, bg, nw)


def _take_top(x, key, count):
    nr, nt = x.shape
    big = jnp.int32(2 ** 30)
    slot = lax.broadcasted_iota(jnp.int32, (count, nt), 0)

    def body(r, carry):
        x, rank, vals = carry
        m = jnp.max(x, axis=0, keepdims=True)
        first = jnp.min(jnp.where(x == m, key, big), axis=0, keepdims=True)
        hit = key == first
        return (jnp.where(hit, -jnp.inf, x), jnp.where(hit, r, rank), jnp.where(slot == r, m, vals))

    init = (x, jnp.full((nr, nt), count, jnp.int32), jnp.zeros((count, nt), F32))
    _, rank, vals = lax.fori_loop(0, count, body, init)
    return vals, rank


def _pair_candidates(v1, v2):
    k = PEER_TOPK
    nt = v1.shape[1]
    j8 = lax.broadcasted_iota(jnp.int32, (8, nt), 0)
    j16 = lax.broadcasted_iota(jnp.int32, (k, nt), 0)
    sums, pos = [v1[0:1, :] + v2], [j16]
    for i in range(1, 8):
        sums.append(jnp.where(j8 < k // (i + 1), v1[i:i + 1, :] + v2[0:8, :], -jnp.inf))
        pos.append(i * k + j8)
    sums.append(v1[8:k, :] + v2[0:1, :])
    pos.append((8 + j8) * k)
    return jnp.concatenate(sums, axis=0), jnp.concatenate(pos, axis=0)


def _sorting_network(n):
    size = 1 << (n - 1).bit_length()
    pairs = []

    def merge(lo, hi, r):
        step = r * 2
        if step < hi - lo:
            merge(lo, hi, step)
            merge(lo + r, hi, step)
            pairs.extend((i, i + r) for i in range(lo + r, hi - r, step))
        else:
            pairs.append((lo, lo + r))

    def sort(lo, hi):
        if hi - lo >= 1:
            mid = lo + (hi - lo) // 2
            sort(lo, mid)
            sort(mid + 1, hi)
            merge(lo, hi, 1)

    sort(0, size - 1)
    return [(i, j) for i, j in pairs if j < n]


def _take_top_values(x, count):
    sub = 8
    tiles = [x[i:i + sub, :] for i in range(0, x.shape[0], sub)]
    for i, j in _sorting_network(len(tiles)):
        tiles[i], tiles[j] = jnp.maximum(tiles[i], tiles[j]), jnp.minimum(tiles[i], tiles[j])
    slot = lax.broadcasted_iota(jnp.int32, (count, x.shape[1]), 0)
    vals = jnp.zeros((count, x.shape[1]), F32)
    for r in range(count):
        m = jnp.max(tiles[0], axis=0, keepdims=True)
        vals = jnp.where(slot == r, m, vals)
        pop = tiles[0] == m
        depth = min(len(tiles), count - r)
        for k in range(depth - 1):
            tiles[k] = jnp.where(pop, tiles[k + 1], tiles[k])
        tiles[depth - 1] = jnp.where(pop, -jnp.inf, tiles[depth - 1])
    return vals


def _count_rows(mask):
    return jnp.sum(jnp.where(mask, 1.0, 0.0), axis=0, keepdims=True)


def _per_first_key(sel):
    k = PEER_TOPK
    cnt = jnp.where(sel, 1.0, 0.0)
    out = [jnp.sum(cnt[0:k, :], axis=0, keepdims=True)]
    out += [jnp.sum(cnt[k + 8 * (i - 1):k + 8 * i, :], axis=0, keepdims=True) for i in range(1, 8)]
    out += [cnt[k + 56 + i:k + 57 + i, :] for i in range(k - 8)]
    return out


def _route_tables_exact(s1, s2):
    k = PEER_TOPK
    key = lax.broadcasted_iota(jnp.int32, s1.shape, 0)
    v1, rank1 = _take_top(s1, key, k)
    v2, rank2 = _take_top(s2, key, k)
    cand, pos = _pair_candidates(v1, v2)
    _, crank = _take_top(cand, pos, k)
    sel = crank < k
    z = jnp.sum(jnp.where(sel, jnp.exp(cand - (v1[0:1, :] + v2[0:1, :])), 0.0), axis=0, keepdims=True)
    n = jnp.zeros(s1.shape, F32)
    for i, n_i in enumerate(_per_first_key(sel)):
        n = jnp.where(rank1 == i, n_i, n)
    e1 = jnp.where(rank1 < k, jnp.exp(s1 - v1[0:1, :]) / z, 0.0)
    e2 = jnp.where(rank2 < k, jnp.exp(s2 - v2[0:1, :]), 0.0)
    return n, e1, rank2.astype(F32), e2


def _route_tables_distinct(s1, s2):
    k = PEER_TOPK
    v1 = _take_top_values(s1, k)
    v2 = _take_top_values(s2, k)
    in1 = s1 >= v1[k - 1:k, :]
    in2 = s2 >= v2[k - 1:k, :]
    cand, _ = _pair_candidates(v1, v2)
    cv = _take_top_values(cand, k)
    sel = cand >= cv[k - 1:k, :]
    ok = (_count_rows(in1) == k) & (_count_rows(in2) == k) & (_count_rows(sel) == k)
    z = jnp.sum(jnp.where(sel, jnp.exp(cand - (v1[0:1, :] + v2[0:1, :])), 0.0), axis=0, keepdims=True)
    n = jnp.zeros(s1.shape, F32)
    rank2 = jnp.full(s2.shape, float(k), F32)
    for i, n_i in enumerate(_per_first_key(sel)):
        n = jnp.where(s1 == v1[i:i + 1, :], n_i, n)
        rank2 = jnp.where(s2 == v2[i:i + 1, :], float(i), rank2)
    e1 = jnp.where(in1, jnp.exp(s1 - v1[0:1, :]) / z, 0.0)
    e2 = jnp.where(in2, jnp.exp(s2 - v2[0:1, :]), 0.0)
    return (n, e1, rank2, e2), ok


def _peer_route_kernel(q_ref, k1_ref, k2_ref, n_ref, e1_ref, r2_ref, e2_ref, s1_ref, s2_ref):
    q = q_ref[...]
    s1_ref[...] = _dot_nt(k1_ref[0], q[:, :PEER_HALF])
    s2_ref[...] = _dot_nt(k2_ref[0], q[:, PEER_HALF:])
    tt = q.shape[0]
    lw = min(LANES, tt)
    for lc in range(tt // lw):
        sl = slice(lc * lw, (lc + 1) * lw)

        def store(tables):
            n, e1, rank2, e2 = tables
            n_ref[0, :, sl] = n
            e1_ref[0, :, sl] = e1
            r2_ref[:, sl] = rank2.astype(BF16)
            e2_ref[:, sl] = e2.astype(BF16)

        tables, ok = _route_tables_distinct(s1_ref[:, sl], s2_ref[:, sl])
        store(tables)

        @pl.when(jnp.min(jnp.where(ok, 1, 0)) == 0)
        def _():
            store(_route_tables_exact(s1_ref[:, sl], s2_ref[:, sl]))


def _peer_route(q, keys1, keys2, tt):
    t = q.shape[0]
    out_a = jax.ShapeDtypeStruct((PEER_HEADS, PEER_NKEYS, t), F32)
    out_b = jax.ShapeDtypeStruct((PEER_HEADS * PEER_NKEYS, t), BF16)
    spec_a = pl.BlockSpec((1, PEER_NKEYS, tt), lambda i, h: (h, 0, i))
    spec_b = pl.BlockSpec((PEER_NKEYS, tt), lambda i, h: (h, i))
    key_spec = pl.BlockSpec((1, PEER_NKEYS, PEER_HALF), lambda i, h: (h, 0, 0))
    return pl.pallas_call(
        _peer_route_kernel,
        out_shape=(out_a, out_a, out_b, out_b),
        grid=(t // tt, PEER_HEADS),
        in_specs=[pl.BlockSpec((tt, PEER_DQ), lambda i, h: (i, h)), key_spec, key_spec],
        out_specs=(spec_a, spec_a, spec_b, spec_b),
        scratch_shapes=[pltpu.VMEM((PEER_NKEYS, tt), F32), pltpu.VMEM((PEER_NKEYS, tt), F32)],
        compiler_params=_params(("parallel", "arbitrary")),
        name="peer_route",
    )(q, keys1, keys2)


def _peer_dense_kernel(xa_ref, xc_ref, lnw_ref, fw_ref, n_ref, e1_ref, r2_in_ref, e2_in_ref, u_ref, v_ref,
                       o_ref, h_ref, pre_ref, pre2_ref, act_ref, act2_ref, acc_ref, r2_ref, e2_ref,
                       *, rows_per_step, n_blocks, n_work, final_norm):
    g = pl.program_id(0)
    last = n_work - 1
    ja = jnp.minimum(g, last) % n_blocks
    jb = jnp.clip(g - 1, 0, last) % n_blocks
    jc = jnp.clip(g - 2, 0, last) % n_blocks

    @pl.when(g == 0)
    def _():
        for ref in (pre_ref, pre2_ref, act_ref, act2_ref):
            ref[...] = jnp.zeros_like(ref)

    @pl.when(ja == 0)
    def _():
        x = xa_ref[...]
        y = x * lax.rsqrt(jnp.mean(x * x, axis=-1, keepdims=True) + EPS)
        h_ref[...] = (y * lnw_ref[...]).astype(BF16)

    @pl.when(jb == 0)
    def _():
        r2_ref[...] = r2_in_ref[...]
        e2_ref[...] = e2_in_ref[...]

    @pl.when(jc == 0)
    def _():
        acc_ref[...] = jnp.zeros_like(acc_ref)

    tt = h_ref.shape[0]
    lw = min(LANES, tt)
    reps = PEER_NKEYS // BF16_ROWS

    def stages(pre_w, pre_r, act_w, act_r):
        zero = jnp.zeros((PEER_NKEYS, lw), BF16)
        mxu_width = min(256, tt)

        def stage_a(part):
            tok = slice(part * mxu_width, (part + 1) * mxu_width)
            pre_w[:, tok] = lax.dot_general(u_ref[...], h_ref[tok, :], _NT, preferred_element_type=F32)

        def stage_b(lc, aa):
            sl = slice(lc * lw, (lc + 1) * lw)
            rows = slice(aa * PEER_NKEYS, (aa + 1) * PEER_NKEYS)
            gate = zero
            for h in range(PEER_HEADS):
                n_row = jnp.broadcast_to(n_ref[h, aa:aa + 1, sl], (BF16_ROWS, lw)).astype(BF16)
                e1_row = jnp.broadcast_to(e1_ref[h, aa:aa + 1, sl], (BF16_ROWS, lw)).astype(BF16)
                n_tile = jnp.concatenate([n_row] * reps, axis=0)
                e1_tile = jnp.concatenate([e1_row] * reps, axis=0)
                keys = slice(h * PEER_NKEYS, (h + 1) * PEER_NKEYS)
                gate = gate + jnp.where(r2_ref[keys, sl] < n_tile, e2_ref[keys, sl], zero) * e1_tile
            act = _gelu_tanh(pre_r[rows, sl].astype(BF16)) * gate
            act_w[sl, rows] = act.T

        def stage_c(part):
            col = slice(part * 256, (part + 1) * 256)
            acc_ref[:, col] += jnp.dot(act_r[...], v_ref[:, col], preferred_element_type=F32)

        mxu = [functools.partial(stage_a, p) for p in range(tt // mxu_width)]
        mxu += [functools.partial(stage_c, p) for p in range(acc_ref.shape[1] // 256)]
        vpu = [functools.partial(stage_b, lc, aa) for lc in range(tt // lw) for aa in range(rows_per_step)]
        per = -(-len(vpu) // len(mxu))
        for i, piece in enumerate(mxu):
            piece()
            for tile in vpu[i * per:(i + 1) * per]:
                tile()

    @pl.when(g % 2 == 0)
    def _():
        stages(pre_ref, pre2_ref, act2_ref, act_ref)

    @pl.when(g % 2 == 1)
    def _():
        stages(pre2_ref, pre_ref, act_ref, act2_ref)

    @pl.when((jc == n_blocks - 1) & (g >= 2))
    def _():
        r = xc_ref[...] + acc_ref[...]
        if final_norm:
            r = r * lax.rsqrt(jnp.mean(r * r, axis=-1, keepdims=True) + EPS) * fw_ref[...]
        o_ref[...] = r


def _peer_dense(x, lnw, route, u, v, layer, tt, rows_per_step, final_w):
    t, d = x.shape
    eb = rows_per_step * PEER_NKEYS
    n_blocks = PEER_N // eb
    n_work = (t // tt) * n_blocks

    def item(lag):
        def at(g):
            w = jnp.clip(g - lag, 0, n_work - 1)
            return w // n_blocks, w % n_blocks
        return at

    a, b, c = item(0), item(1), item(2)
    spec_rows = pl.BlockSpec((PEER_HEADS, rows_per_step, tt), lambda g: (0, b(g)[1], b(g)[0]))
    spec_keys = pl.BlockSpec((PEER_HEADS * PEER_NKEYS, tt), lambda g: (0, b(g)[0]))
    return pl.pallas_call(
        functools.partial(_peer_dense_kernel, rows_per_step=rows_per_step, n_blocks=n_blocks, n_work=n_work,
                          final_norm=final_w is not None),
        out_shape=jax.ShapeDtypeStruct((t, d), F32),
        grid=(n_work + 2,),
        in_specs=[pl.BlockSpec((tt, d), lambda g: (a(g)[0], 0)),
                  pl.BlockSpec((tt, d), lambda g: (c(g)[0], 0)),
                  pl.BlockSpec((1, d), lambda g: (0, 0)),
                  pl.BlockSpec((1, d), lambda g: (0, 0)),
                  spec_rows, spec_rows, spec_keys, spec_keys,
                  pl.BlockSpec((None, eb, d), lambda g: (layer, a(g)[1], 0)),
                  pl.BlockSpec((None, eb, d), lambda g: (layer, c(g)[1], 0))],
        out_specs=pl.BlockSpec((tt, d), lambda g: (c(g)[0], 0)),
        scratch_shapes=[pltpu.VMEM((tt, d), BF16), pltpu.VMEM((eb, tt), F32), pltpu.VMEM((eb, tt), F32),
                        pltpu.VMEM((tt, eb), BF16), pltpu.VMEM((tt, eb), BF16), pltpu.VMEM((tt, d), F32),
                        pltpu.VMEM((PEER_HEADS * PEER_NKEYS, tt), BF16),
                        pltpu.VMEM((PEER_HEADS * PEER_NKEYS, tt), BF16)],
        compiler_params=_params(("arbitrary",)),
        name="peer_dense",
    )(x, x, lnw.reshape(1, d), (lnw if final_w is None else final_w).reshape(1, d), *route, u, v)


def _peer_ffn(x, lnw, wq, keys1, keys2, u, v, layer, final_w=None):
    tt = min(512, x.shape[0])
    q = _norm_matmul(x, lnw, wq)
    route = _peer_route(q, keys1, keys2, tt)
    return _peer_dense(x, lnw, route, u, v, layer, tt, rows_per_step=16, final_w=final_w)


def _pad_lanes(vec, offset):
    return jnp.zeros((SMALL_W,), F32).at[offset:offset + vec.shape[0]].set(vec.astype(F32))


def _conv_history(buf):
    return jnp.pad(buf, ((0, 0), (CONV_PAD - (CONV_W - 1), 0), (0, 0)))


def kernel(x_prompt, x_sample, state_dn_conv, state_dn, state_ssm_conv, state_ssm, state_gla, ln_mix, ln_ffn, ln_final, w_in_even, w_out_even, dn_conv_w, dn_a_log, dn_dt_bias, dn_norm_w, ssm_conv_w, ssm_conv_b, ssm_a_log, ssm_dt_bias, ssm_d, ssm_norm_w, w_in_odd, gla_w_gate2, gla_b_gate, gla_norm_w, w_out_odd, peer_w_q, peer_keys1, peer_keys2, peer_u, peer_v):
    we = w_in_even[0]
    o = 0
    cuts = {}
    for name, size in (("qkv", DN_CONV_DIM), ("a", DN_HEADS), ("b", DN_HEADS), ("gate", DN_V),
                       ("z", SSM_INNER), ("xbc", SSM_CONV_DIM), ("dt", SSM_HEADS)):
        cuts[name] = we[:, o:o + size]
        o += size
    small_w = jnp.concatenate([cuts["a"], cuts["b"], cuts["dt"]], axis=1)
    small_w = jnp.pad(small_w, ((0, 0), (0, SMALL_W - small_w.shape[1])))
    w_even = jnp.concatenate([cuts["qkv"], cuts["gate"], cuts["xbc"], cuts["z"], small_w], axis=1).astype(BF16)
    wo = w_in_odd[0]
    qkv_w = 2 * GLA_K + GLA_V
    glr_w = jnp.pad(wo[:, qkv_w:qkv_w + GLA_LR], ((0, 0), (0, SMALL_W - GLA_LR)))
    w_odd = jnp.concatenate([wo[:, :qkv_w], wo[:, qkv_w + GLA_LR:], glr_w], axis=1).astype(BF16)
    w_out_dn = w_out_even[0, :DN_V].astype(BF16)
    w_out_ssm = w_out_even[0, DN_V:].astype(BF16)
    w_out_gla = w_out_odd[0].astype(BF16)
    wg = jnp.pad(gla_w_gate2[0], ((0, SMALL_W - GLA_LR), (0, 0))).astype(BF16)
    wq = peer_w_q.astype(BF16)
    k1 = peer_keys1.astype(BF16)
    k2 = peer_keys2.astype(BF16)
    pu = peer_u.astype(BF16)
    pv = peer_v.astype(BF16)

    zero8 = jnp.zeros((8 - 2, SMALL_W), F32)
    dn_par = jnp.concatenate([_pad_lanes(dn_a_log[0], 0)[None], _pad_lanes(dn_dt_bias[0], 0)[None], zero8])
    lane0 = 2 * DN_HEADS
    ssm_par = jnp.concatenate([_pad_lanes(ssm_a_log[0], lane0)[None], _pad_lanes(ssm_dt_bias[0], lane0)[None],
                               _pad_lanes(ssm_d[0], lane0)[None], zero8[1:]])

    def trunk(x, batch, seq_len, conv_dn0, s_dn0, conv_ssm0, s_ssm0, s_gla0):
        chunk = PROMPT_CHUNK if seq_len >= PROMPT_CHUNK else STEP_CHUNK
        nc = -(-seq_len // chunk)
        padded = nc * chunk != seq_len

        def stacked(heads):
            nb = max(1, min(batch, 256 // (heads * chunk)))
            while batch % nb:
                nb -= 1
            return nb

        dn_nb, ssd_nb = stacked(DN_HEADS), stacked(SSM_HEADS)

        def to_seq(a):
            if not padded:
                return a
            a = a.reshape(batch, seq_len, a.shape[-1])
            return jnp.pad(a, ((0, 0), (0, nc * chunk - seq_len), (0, 0))).reshape(batch * nc * chunk, -1)

        def from_seq(a):
            if not padded:
                return a
            return a.reshape(batch, nc * chunk, a.shape[-1])[:, :seq_len].reshape(batch * seq_len, -1)

        def last_rows(buf, proj, col, width):
            tail = proj.reshape(batch, seq_len, proj.shape[-1])[:, -min(seq_len, CONV_W - 1):, col:col + width]
            return jnp.concatenate([buf, tail], axis=1)[:, -(CONV_W - 1):]

        seq = dict(batch=batch, seq_len=seq_len, chunk=chunk)

        proj = _norm_matmul(x, ln_mix[0], w_even)
        proj_seq = to_seq(proj)
        o_dn, s_dn = _dn_mixer(proj_seq.reshape(batch, nc * chunk, EV_WIDTH), _conv_history(conv_dn0[0]),
                               s_dn0[0], dn_conv_w[0], dn_par, dn_norm_w[0].reshape(1, DN_DV),
                               nb=dn_nb, **seq)
        o_dn = o_dn.reshape(batch * nc * chunk, DN_V)
        y_ssm, s_ssm = _ssd_mixer(proj_seq.reshape(batch, nc * chunk, EV_WIDTH), _conv_history(conv_ssm0[0]),
                                  s_ssm0[0], ssm_conv_w[0], ssm_conv_b[0].reshape(1, SSM_CONV_DIM), ssm_par,
                                  ssm_norm_w[0].reshape(1, SSM_INNER), nb=ssd_nb, **seq)
        y_ssm = y_ssm.reshape(batch * nc * chunk, SSM_INNER)
        conv_dn = last_rows(conv_dn0[0], proj, EV_QKV, DN_CONV_DIM)
        conv_ssm = last_rows(conv_ssm0[0], proj, EV_XBC, SSM_CONV_DIM)
        x = _proj_residual(x, [from_seq(o_dn), from_seq(y_ssm)], [w_out_dn, w_out_ssm])
        x = _peer_ffn(x, ln_ffn[0], wq[0], k1[0], k2[0], pu, pv, 0)

        proj = _norm_matmul(x, ln_mix[1], w_odd)
        o_gla, s_gla = _gla_mixer(to_seq(proj).reshape(batch, nc * chunk, OD_WIDTH), s_gla0[0], wg,
                                  gla_b_gate[0].reshape(1, GLA_K), gla_norm_w[0].reshape(1, GLA_DV),
                                  nb=dn_nb, **seq)
        x = _proj_residual(x, [from_seq(o_gla.reshape(batch * nc * chunk, GLA_V))], [w_out_gla])
        x = _peer_ffn(x, ln_ffn[1], wq[1], k1[1], k2[1], pu, pv, 1, final_w=ln_final)

        y = x.reshape(batch, seq_len, D_MODEL)
        return y, conv_dn[None], s_dn[None], conv_ssm[None], s_ssm[None], s_gla[None]

    bp, lp = x_prompt.shape[:2]
    bs, ls = x_sample.shape[:2]

    def zeros_like_state(s):
        return jnp.zeros((s.shape[0], bp) + s.shape[2:], s.dtype)

    prompt = trunk(x_prompt.reshape(bp * lp, D_MODEL), bp, lp, zeros_like_state(state_dn_conv),
                   zeros_like_state(state_dn), zeros_like_state(state_ssm_conv),
                   zeros_like_state(state_ssm), zeros_like_state(state_gla))
    sample = trunk(x_sample.reshape(bs * ls, D_MODEL), bs, ls, state_dn_conv, state_dn,
                   state_ssm_conv, state_ssm, state_gla)
    return (prompt[0], sample[0]) + prompt[1:] + sample[1:]
```

```python
import functools

import jax
import jax.numpy as jnp
from jax import lax
from jax.experimental import pallas as pl
from jax.experimental.pallas import tpu as pltpu

F32 = jnp.float32
BF16 = jnp.bfloat16
EPS = 1e-6

D_MODEL = 1024
CONV_W = 4
CONV_PAD = 8
PROMPT_CHUNK = 64
STEP_CHUNK = 8

DN_HEADS, DN_DK, DN_DV = 4, 128, 128
DN_QK = DN_HEADS * DN_DK
DN_V = DN_HEADS * DN_DV
DN_CONV_DIM = 2 * DN_QK + DN_V
SSM_HEADS, SSM_P, SSM_GROUPS, SSM_N = 8, 64, 2, 128
SSM_INNER = SSM_HEADS * SSM_P
SSM_CONV_DIM = SSM_INNER + 2 * SSM_GROUPS * SSM_N
GLA_HEADS, GLA_DK, GLA_DV, GLA_LR = 4, 128, 256, 16
GLA_K = GLA_HEADS * GLA_DK
GLA_V = GLA_HEADS * GLA_DV
GLA_NORMALIZER = 16.0
PEER_HEADS, PEER_DQ, PEER_NKEYS, PEER_TOPK = 8, 256, 128, 16
PEER_HALF = PEER_DQ // 2
PEER_N = PEER_NKEYS * PEER_NKEYS
LANES = 128
BF16_ROWS = 16
SMALL_W = LANES

EV_QKV, EV_GATE, EV_XBC, EV_Z, EV_SMALL = 0, 1536, 2048, 3072, 3584
EV_WIDTH = EV_SMALL + SMALL_W
OD_QKV, OD_OG, OD_GLR = 0, 2048, 3072
OD_WIDTH = OD_GLR + SMALL_W

VMEM_LIMIT = 56 * 1024 * 1024

_NT = (((1,), (1,)), ((), ()))
_TN = (((0,), (0,)), ((), ()))


def _dot(a, b):
    return jnp.dot(a.astype(BF16), b.astype(BF16), preferred_element_type=F32)


def _dot_nt(a, b):
    return lax.dot_general(a.astype(BF16), b.astype(BF16), _NT, preferred_element_type=F32)


def _dot_tn(a, b):
    return lax.dot_general(a.astype(BF16), b.astype(BF16), _TN, preferred_element_type=F32)


def _params(sem):
    return pltpu.CompilerParams(dimension_semantics=sem, vmem_limit_bytes=VMEM_LIMIT)


def _silu(x):
    return x * jax.nn.sigmoid(x)


def _gelu_tanh(x):
    c0 = 0.7978845608028654
    hx = 0.5 * x
    return hx + hx * jnp.tanh(x * (c0 + (c0 * 0.044715) * (x * x)))


def _softplus(x):
    return jnp.maximum(x, 0.0) + jnp.log(1.0 + jnp.exp(-jnp.abs(x)))


def _cumsum_rows(x):
    n = x.shape[0]
    row = lax.broadcasted_iota(jnp.int32, x.shape, 0)
    s = 1
    while s < n:
        x = x + jnp.where(row >= s, pltpu.roll(x, s, axis=0), 0.0)
        s *= 2
    return x


def _col_to_row(col, eye):
    return jnp.sum(jnp.where(eye, col, 0.0), axis=0, keepdims=True)


def _row_to_col(row, eye):
    return jnp.sum(jnp.where(eye, row, 0.0), axis=1, keepdims=True)


def _tri_masks(n):
    ri = lax.broadcasted_iota(jnp.int32, (n, n), 0)
    ci = lax.broadcasted_iota(jnp.int32, (n, n), 1)
    return ri, ci, ri == ci, ri >= ci, ri > ci


def _valid_rows(chunk, seq_len, shape):
    if seq_len % chunk == 0:
        return None
    pos = pl.program_id(1) * chunk + lax.broadcasted_iota(jnp.int32, shape, 0)
    return pos < seq_len


def _causal_conv(x_ref, xext, cw_ref, chunk):
    x = x_ref[...]
    xext[CONV_PAD:CONV_PAD + chunk, :] = x
    cw = cw_ref[...]
    pre = cw[CONV_W - 1:CONV_W, :] * x
    for j in range(1, CONV_W):
        pre = pre + cw[CONV_W - 1 - j:CONV_W - j, :] * xext[CONV_PAD - j:CONV_PAD - j + chunk, :]
    xext[0:CONV_PAD, :] = xext[chunk:chunk + CONV_PAD, :]
    return pre


def _norm_matmul_kernel(x_ref, lnw_ref, w_ref, o_ref):
    x = x_ref[...]
    y = x * lax.rsqrt(jnp.mean(x * x, axis=-1, keepdims=True) + EPS)
    h = (y * lnw_ref[...]).astype(BF16)
    o_ref[...] = jnp.dot(h, w_ref[...], preferred_element_type=F32)


def _norm_matmul(x, lnw, w):
    t, d = x.shape
    n = w.shape[1]
    tm = min(512, t)
    return pl.pallas_call(
        _norm_matmul_kernel,
        out_shape=jax.ShapeDtypeStruct((t, n), F32),
        grid=(t // tm,),
        in_specs=[pl.BlockSpec((tm, d), lambda i: (i, 0)),
                  pl.BlockSpec((1, d), lambda i: (0, 0)),
                  pl.BlockSpec((d, n), lambda i: (0, 0))],
        out_specs=pl.BlockSpec((tm, n), lambda i: (i, 0)),
        compiler_params=_params(("parallel",)),
        name="norm_matmul",
    )(x, lnw.reshape(1, d), w)


def _proj_residual_kernel(*refs, n_in):
    x_ref, o_ref = refs[0], refs[-1]
    acc = x_ref[...]
    for a_ref, w_ref in zip(refs[1:1 + n_in], refs[1 + n_in:1 + 2 * n_in]):
        acc = acc + jnp.dot(a_ref[...].astype(BF16), w_ref[...], preferred_element_type=F32)
    o_ref[...] = acc


def _proj_residual(x, acts, weights):
    t, d = x.shape
    tm = min(512, t)
    n_in = len(acts)
    in_specs = [pl.BlockSpec((tm, d), lambda i: (i, 0))]
    in_specs += [pl.BlockSpec((tm, a.shape[1]), lambda i: (i, 0)) for a in acts]
    in_specs += [pl.BlockSpec(w.shape, lambda i: (0, 0)) for w in weights]
    return pl.pallas_call(
        functools.partial(_proj_residual_kernel, n_in=n_in),
        out_shape=jax.ShapeDtypeStruct((t, d), F32),
        grid=(t // tm,),
        in_specs=in_specs,
        out_specs=pl.BlockSpec((tm, d), lambda i: (i, 0)),
        compiler_params=_params(("parallel",)),
        name="proj_residual",
    )(x, *acts, *weights)


def _unit_lower_inverse(m, ri, ci, eye, block):
    t = jnp.where(eye, 1.0, 0.0)
    level = 0
    while (1 << level) < block:
        pair = ((ri >> (level + 1)) == (ci >> (level + 1))) & ((ri >> level) != (ci >> level))
        c = jnp.where(pair, m, 0.0)
        t = t - c if level == 0 else t - _dot(_dot(t, c), t)
        level += 1
    return t


def _dn_kernel(qkv_ref, gate_ref, small_ref, conv0_ref, s0_ref, cw_ref, par_ref, nw_ref,
               o_ref, s_out_ref, xext, state, *, chunk, seq_len, nb):
    c_idx = pl.program_id(1)
    groups = nb * DN_HEADS
    log2c = chunk.bit_length() - 1
    group_lanes = lambda g: slice(g * DN_DV, (g + 1) * DN_DV)

    @pl.when(c_idx == 0)
    def _():
        for s in range(nb):
            xext[s, 0:CONV_PAD, :] = conv0_ref[s]
            for h in range(DN_HEADS):
                state[:, group_lanes(s * DN_HEADS + h)] = s0_ref[s, h]

    par = par_ref[...]
    qs, ks, vs, gcs, betas, gates = [], [], [], [], [], []
    for s in range(nb):
        act = _silu(_causal_conv(qkv_ref.at[s], xext.at[s], cw_ref, chunk))
        small = small_ref[s]
        g_all = -jnp.exp(par[0:1, :]) * _softplus(small + par[1:2, :])
        beta_all = jax.nn.sigmoid(small)
        valid = _valid_rows(chunk, seq_len, small.shape)
        if valid is not None:
            g_all = jnp.where(valid, g_all, 0.0)
            beta_all = jnp.where(valid, beta_all, 0.0)
        gc_all = _cumsum_rows(g_all)
        for h in range(DN_HEADS):
            xq = act[:, h * DN_DK:(h + 1) * DN_DK]
            xk = act[:, DN_QK + h * DN_DK:DN_QK + (h + 1) * DN_DK]
            qs.append(xq * lax.rsqrt(jnp.sum(xq * xq, axis=-1, keepdims=True) + EPS) * (DN_DK ** -0.5))
            ks.append(xk * lax.rsqrt(jnp.sum(xk * xk, axis=-1, keepdims=True) + EPS))
            vs.append(act[:, 2 * DN_QK + h * DN_DV:2 * DN_QK + (h + 1) * DN_DV])
            gcs.append(gc_all[:, h:h + 1])
            betas.append(beta_all[:, DN_HEADS + h:DN_HEADS + h + 1])
            gates.append(gate_ref[s, :, h * DN_DV:(h + 1) * DN_DV])
    q, k, v = (jnp.concatenate(t, axis=0) for t in (qs, ks, vs))
    gc, beta = (jnp.concatenate(t, axis=0) for t in (gcs, betas))
    gc_last = jnp.concatenate([jnp.broadcast_to(t[chunk - 1:chunk, :], (chunk, 1)) for t in gcs], axis=0)
    last_decay = jnp.concatenate(
        [jnp.broadcast_to(jnp.exp(t[chunk - 1:chunk, :]), (1, DN_DV)) for t in gcs], axis=1)

    rows = groups * chunk
    ri, ci, eye, lower, strict_lower = _tri_masks(rows)
    same = (ri >> log2c) == (ci >> log2c)
    causal = same & lower
    strict = same & strict_lower
    decay = jnp.where(causal, jnp.exp(gc - _col_to_row(gc, eye)), 0.0)
    kb = k * beta
    eg = jnp.exp(gc)
    m = jnp.where(strict, _dot_nt(kb, k) * decay, 0.0)
    t_inv = _unit_lower_inverse(m, ri, ci, eye, chunk)
    sol = _dot(t_inv, jnp.concatenate([v * beta, kb * eg], axis=1))
    u, w = sol[:, :DN_DV], sol[:, DN_DV:]
    qk = jnp.where(causal, _dot_nt(q, k) * decay, 0.0)

    def own_block(x):
        return jnp.concatenate(
            [x[g * chunk:(g + 1) * chunk, group_lanes(g)] for g in range(groups)], axis=0)

    s_all = state[...]
    v_new = u - own_block(_dot(w, s_all))
    o = own_block(_dot(q * eg, s_all)) + _dot(qk, v_new)
    wide = (rows, groups * DN_DV)
    row_group = lax.broadcasted_iota(jnp.int32, wide, 0) >> log2c
    lane_group = lax.broadcasted_iota(jnp.int32, wide, 1) // DN_DV
    v_wide = jnp.where(row_group == lane_group, jnp.concatenate([v_new] * groups, axis=1), 0.0)
    state[...] = s_all * last_decay + _dot_tn(k * jnp.exp(gc_last - gc), v_wide)

    o = o * lax.rsqrt(jnp.mean(o * o, axis=-1, keepdims=True) + EPS) * nw_ref[...]
    y = o * _silu(jnp.concatenate(gates, axis=0))
    for s in range(nb):
        for h in range(DN_HEADS):
            g = s * DN_HEADS + h
            o_ref[s, :, h * DN_DV:(h + 1) * DN_DV] = y[g * chunk:(g + 1) * chunk, :]

    @pl.when(c_idx == pl.num_programs(1) - 1)
    def _():
        for s in range(nb):
            for h in range(DN_HEADS):
                s_out_ref[s, h] = state[:, group_lanes(s * DN_HEADS + h)]


def _dn_mixer(proj, conv0, s0, cw, par, nw, *, batch, seq_len, chunk, nb):
    nc = proj.shape[1] // chunk
    return pl.pallas_call(
        functools.partial(_dn_kernel, chunk=chunk, seq_len=seq_len, nb=nb),
        out_shape=(jax.ShapeDtypeStruct((batch, nc * chunk, DN_V), F32),
                   jax.ShapeDtypeStruct((batch, DN_HEADS, DN_DK, DN_DV), F32)),
        grid=(batch // nb, nc),
        in_specs=[pl.BlockSpec((nb, chunk, DN_CONV_DIM), lambda b, c: (b, c, EV_QKV // DN_CONV_DIM)),
                  pl.BlockSpec((nb, chunk, DN_V), lambda b, c: (b, c, EV_GATE // DN_V)),
                  pl.BlockSpec((nb, chunk, SMALL_W), lambda b, c: (b, c, EV_SMALL // SMALL_W)),
                  pl.BlockSpec((nb, CONV_PAD, DN_CONV_DIM), lambda b, c: (b, 0, 0)),
                  pl.BlockSpec((nb, DN_HEADS, DN_DK, DN_DV), lambda b, c: (b, 0, 0, 0)),
                  pl.BlockSpec((CONV_W, DN_CONV_DIM), lambda b, c: (0, 0)),
                  pl.BlockSpec((8, SMALL_W), lambda b, c: (0, 0)),
                  pl.BlockSpec((1, DN_DV), lambda b, c: (0, 0))],
        out_specs=(pl.BlockSpec((nb, chunk, DN_V), lambda b, c: (b, c, 0)),
                   pl.BlockSpec((nb, DN_HEADS, DN_DK, DN_DV), lambda b, c: (b, 0, 0, 0))),
        scratch_shapes=[pltpu.VMEM((nb, chunk + CONV_PAD, DN_CONV_DIM), F32),
                        pltpu.VMEM((DN_DK, nb * DN_HEADS * DN_DV), F32)],
        compiler_params=_params(("parallel", "arbitrary")),
        name="dn_mixer",
    )(proj, proj, proj, conv0, s0, cw, par, nw)


def _ssd_kernel(xbc_ref, z_ref, small_ref, conv0_ref, s0_ref, cw_ref, cb_ref, par_ref, nw_ref,
                y_ref, s_out_ref, xext, state, ybuf, *, chunk, seq_len, nb):
    c_idx = pl.program_id(1)
    groups = nb * SSM_HEADS
    log2c = chunk.bit_length() - 1
    group_rows = lambda g: slice(g * SSM_P, (g + 1) * SSM_P)

    @pl.when(c_idx == 0)
    def _():
        for s in range(nb):
            xext[s, 0:CONV_PAD, :] = conv0_ref[s]
            for h in range(SSM_HEADS):
                state[group_rows(s * SSM_HEADS + h), :] = s0_ref[s, h]

    par = par_ref[...]
    gw = SSM_GROUPS * SSM_N
    heads_per_group = SSM_HEADS // SSM_GROUPS
    xs, bms, cms, dts, acs, ds = [], [], [], [], [], []
    for s in range(nb):
        xbc = _silu(_causal_conv(xbc_ref.at[s], xext.at[s], cw_ref, chunk) + cb_ref[...])
        small = small_ref[s]
        dt_all = _softplus(small + par[1:2, :])
        valid = _valid_rows(chunk, seq_len, small.shape)
        if valid is not None:
            dt_all = jnp.where(valid, dt_all, 0.0)
        ac_all = _cumsum_rows(dt_all * -jnp.exp(par[0:1, :]))
        for h in range(SSM_HEADS):
            g = h // heads_per_group
            lane = 2 * DN_HEADS + h
            xs.append(xbc[:, h * SSM_P:(h + 1) * SSM_P])
            bms.append(xbc[:, SSM_INNER + g * SSM_N:SSM_INNER + (g + 1) * SSM_N])
            cms.append(xbc[:, SSM_INNER + gw + g * SSM_N:SSM_INNER + gw + (g + 1) * SSM_N])
            dts.append(dt_all[:, lane:lane + 1])
            acs.append(ac_all[:, lane:lane + 1])
            ds.append(jnp.broadcast_to(par[2:3, lane:lane + 1], (chunk, 1)))
    per_stack = max(1, min(groups, 256 // chunk))
    rows = per_stack * chunk
    ri, ci, eye, lower, strict_lower = _tri_masks(rows)
    causal = ((ri >> log2c) == (ci >> log2c)) & lower
    wide = (rows, per_stack * SSM_P)
    row_group = lax.broadcasted_iota(jnp.int32, wide, 0) >> log2c
    lane_group = lax.broadcasted_iota(jnp.int32, wide, 1) // SSM_P
    y_groups = []
    for lo in range(0, groups, per_stack):
        part = slice(lo, lo + per_stack)
        x, bm, cm = (jnp.concatenate(t[part], axis=0) for t in (xs, bms, cms))
        dt, ac, d_skip = (jnp.concatenate(t[part], axis=0) for t in (dts, acs, ds))
        ac_last = jnp.concatenate(
            [jnp.broadcast_to(t[chunk - 1:chunk, :], (chunk, 1)) for t in acs[part]], axis=0)
        last_decay = jnp.concatenate(
            [jnp.broadcast_to(jnp.exp(t[chunk - 1:chunk, :]), (SSM_P, 1)) for t in acs[part]], axis=0)
        decay = jnp.where(causal, jnp.exp(ac - _col_to_row(ac, eye)), 0.0)
        xdt = x * dt
        state_rows = slice(lo * SSM_P, (lo + per_stack) * SSM_P)
        s_all = state[state_rows, :]
        off = _dot_nt(cm * jnp.exp(ac), s_all)
        y = _dot(_dot_nt(cm, bm) * decay, xdt) + jnp.concatenate(
            [off[g * chunk:(g + 1) * chunk, group_rows(g)] for g in range(per_stack)], axis=0)
        xdt_pair = jnp.concatenate([xdt, xdt], axis=1)
        xdt_wide = jnp.where(row_group == lane_group,
                             jnp.concatenate([xdt_pair] * (per_stack // 2), axis=1), 0.0)
        state[state_rows, :] = s_all * last_decay + _dot_tn(xdt_wide, bm * jnp.exp(ac_last - ac))
        y = y + d_skip * x
        y_groups += [y[g * chunk:(g + 1) * chunk, :] for g in range(per_stack)]

    nw = nw_ref[...]
    gn = SSM_INNER // SSM_GROUPS
    for s in range(nb):
        for h in range(SSM_HEADS):
            ybuf[:, h * SSM_P:(h + 1) * SSM_P] = y_groups[s * SSM_HEADS + h]
        ys = ybuf[...] * _silu(z_ref[s])
        for g in range(SSM_GROUPS):
            yg = ys[:, g * gn:(g + 1) * gn]
            yg = yg * lax.rsqrt(jnp.mean(yg * yg, axis=-1, keepdims=True) + EPS)
            y_ref[s, :, g * gn:(g + 1) * gn] = yg * nw[:, g * gn:(g + 1) * gn]

    @pl.when(c_idx == pl.num_programs(1) - 1)
    def _():
        for s in range(nb):
            for h in range(SSM_HEADS):
                s_out_ref[s, h] = state[group_rows(s * SSM_HEADS + h), :]


def _ssd_mixer(proj, conv0, s0, cw, cb, par, nw, *, batch, seq_len, chunk, nb):
    nc = proj.shape[1] // chunk
    return pl.pallas_call(
        functools.partial(_ssd_kernel, chunk=chunk, seq_len=seq_len, nb=nb),
        out_shape=(jax.ShapeDtypeStruct((batch, nc * chunk, SSM_INNER), F32),
                   jax.ShapeDtypeStruct((batch, SSM_HEADS, SSM_P, SSM_N), F32)),
        grid=(batch // nb, nc),
        in_specs=[pl.BlockSpec((nb, chunk, SSM_CONV_DIM), lambda b, c: (b, c, EV_XBC // SSM_CONV_DIM)),
                  pl.BlockSpec((nb, chunk, SSM_INNER), lambda b, c: (b, c, EV_Z // SSM_INNER)),
                  pl.BlockSpec((nb, chunk, SMALL_W), lambda b, c: (b, c, EV_SMALL // SMALL_W)),
                  pl.BlockSpec((nb, CONV_PAD, SSM_CONV_DIM), lambda b, c: (b, 0, 0)),
                  pl.BlockSpec((nb, SSM_HEADS, SSM_P, SSM_N), lambda b, c: (b, 0, 0, 0)),
                  pl.BlockSpec((CONV_W, SSM_CONV_DIM), lambda b, c: (0, 0)),
                  pl.BlockSpec((1, SSM_CONV_DIM), lambda b, c: (0, 0)),
                  pl.BlockSpec((8, SMALL_W), lambda b, c: (0, 0)),
                  pl.BlockSpec((1, SSM_INNER), lambda b, c: (0, 0))],
        out_specs=(pl.BlockSpec((nb, chunk, SSM_INNER), lambda b, c: (b, c, 0)),
                   pl.BlockSpec((nb, SSM_HEADS, SSM_P, SSM_N), lambda b, c: (b, 0, 0, 0))),
        scratch_shapes=[pltpu.VMEM((nb, chunk + CONV_PAD, SSM_CONV_DIM), F32),
                        pltpu.VMEM((nb * SSM_HEADS * SSM_P, SSM_N), F32),
                        pltpu.VMEM((chunk, SSM_INNER), F32)],
        compiler_params=_params(("parallel", "arbitrary")),
        name="ssd_mixer",
    )(proj, proj, proj, conv0, s0, cw, cb, par, nw)


def _gla_kernel(qkv_ref, og_ref, glr_ref, s0_ref, wg_ref, bg_ref, nw_ref,
                o_ref, s_out_ref, state, *, chunk, seq_len, nb):
    c_idx = pl.program_id(1)
    groups = nb * GLA_HEADS
    log2c = chunk.bit_length() - 1
    group_lanes = lambda g: slice(g * GLA_DV, (g + 1) * GLA_DV)

    @pl.when(c_idx == 0)
    def _():
        for s in range(nb):
            for h in range(GLA_HEADS):
                state[:, group_lanes(s * GLA_HEADS + h)] = s0_ref[s, h]

    qs, ks, vs, bs, gates = [], [], [], [], []
    for s in range(nb):
        x = bg_ref[...] + jnp.dot(glr_ref[s].astype(BF16), wg_ref[...], preferred_element_type=F32)
        gk = (jnp.minimum(x, 0.0) - jnp.log(1.0 + jnp.exp(-jnp.abs(x)))) / GLA_NORMALIZER
        valid = _valid_rows(chunk, seq_len, gk.shape)
        if valid is not None:
            gk = jnp.where(valid, gk, 0.0)
        b_all = _cumsum_rows(gk)
        for h in range(GLA_HEADS):
            qs.append(qkv_ref[s, :, h * GLA_DK:(h + 1) * GLA_DK] * (GLA_DK ** -0.5))
            ks.append(qkv_ref[s, :, GLA_K + h * GLA_DK:GLA_K + (h + 1) * GLA_DK])
            vs.append(qkv_ref[s, :, 2 * GLA_K + h * GLA_DV:2 * GLA_K + (h + 1) * GLA_DV])
            bs.append(b_all[:, h * GLA_DK:(h + 1) * GLA_DK])
            gates.append(og_ref[s, :, h * GLA_DV:(h + 1) * GLA_DV])
    q, k, b = (jnp.concatenate(t, axis=0) for t in (qs, ks, bs))
    v = jnp.concatenate(vs, axis=0)
    b_last = jnp.concatenate([jnp.broadcast_to(t[chunk - 1:chunk, :], (chunk, GLA_DK)) for t in bs], axis=0)
    eye_k = _tri_masks(GLA_DK)[2]
    last_decay = jnp.concatenate(
        [jnp.broadcast_to(_row_to_col(jnp.exp(t[chunk - 1:chunk, :]), eye_k), (GLA_DK, GLA_DV)) for t in bs],
        axis=1)

    rows = groups * chunk
    ri, ci, eye, lower, strict_lower = _tri_masks(rows)
    causal = ((ri >> log2c) == (ci >> log2c)) & lower
    qe = q * jnp.exp(b)
    att = jnp.where(causal, _dot_nt(qe, k * jnp.exp(-b)), 0.0)

    s_all = state[...]
    qs_all = _dot(qe, s_all)
    o = jnp.concatenate([qs_all[g * chunk:(g + 1) * chunk, group_lanes(g)] for g in range(groups)], axis=0)
    o = o + _dot(att, v)
    wide = (rows, groups * GLA_DV)
    row_group = lax.broadcasted_iota(jnp.int32, wide, 0) >> log2c
    lane_group = lax.broadcasted_iota(jnp.int32, wide, 1) // GLA_DV
    v_wide = jnp.where(row_group == lane_group, jnp.concatenate([v] * groups, axis=1), 0.0)
    state[...] = s_all * last_decay + _dot_tn(k * jnp.exp(b_last - b), v_wide)

    o = o * lax.rsqrt(jnp.mean(o * o, axis=-1, keepdims=True) + EPS) * nw_ref[...]
    y = o * _silu(jnp.concatenate(gates, axis=0))
    for s in range(nb):
        for h in range(GLA_HEADS):
            g = s * GLA_HEADS + h
            o_ref[s, :, h * GLA_DV:(h + 1) * GLA_DV] = y[g * chunk:(g + 1) * chunk, :]

    @pl.when(c_idx == pl.num_programs(1) - 1)
    def _():
        for s in range(nb):
            for h in range(GLA_HEADS):
                s_out_ref[s, h] = state[:, group_lanes(s * GLA_HEADS + h)]


def _gla_mixer(proj, s0, wg, bg, nw, *, batch, seq_len, chunk, nb):
    nc = proj.shape[1] // chunk
    qkv_w = 2 * GLA_K + GLA_V
    return pl.pallas_call(
        functools.partial(_gla_kernel, chunk=chunk, seq_len=seq_len, nb=nb),
        out_shape=(jax.ShapeDtypeStruct((batch, nc * chunk, GLA_V), F32),
                   jax.ShapeDtypeStruct((batch, GLA_HEADS, GLA_DK, GLA_DV), F32)),
        grid=(batch // nb, nc),
        in_specs=[pl.BlockSpec((nb, chunk, qkv_w), lambda b, c: (b, c, OD_QKV // qkv_w)),
                  pl.BlockSpec((nb, chunk, GLA_V), lambda b, c: (b, c, OD_OG // GLA_V)),
                  pl.BlockSpec((nb, chunk, SMALL_W), lambda b, c: (b, c, OD_GLR // SMALL_W)),
                  pl.BlockSpec((nb, GLA_HEADS, GLA_DK, GLA_DV), lambda b, c: (b, 0, 0, 0)),
                  pl.BlockSpec((SMALL_W, GLA_K), lambda b, c: (0, 0)),
                  pl.BlockSpec((1, GLA_K), lambda b, c: (0, 0)),
                  pl.BlockSpec((1, GLA_DV), lambda b, c: (0, 0))],
        out_specs=(pl.BlockSpec((nb, chunk, GLA_V), lambda b, c: (b, c, 0)),
                   pl.BlockSpec((nb, GLA_HEADS, GLA_DK, GLA_DV), lambda b, c: (b, 0, 0, 0))),
        scratch_shapes=[pltpu.VMEM((GLA_DK, nb * GLA_HEADS * GLA_DV), F32)],
        compiler_params=_params(("parallel", "arbitrary")),
        name="gla_mixer",
    )(proj, proj, proj, s0, wg, bg, nw)


def _take_top(x, key, count):
    nr, nt = x.shape
    big = jnp.int32(2 ** 30)
    slot = lax.broadcasted_iota(jnp.int32, (count, nt), 0)

    def body(r, carry):
        x, rank, vals = carry
        m = jnp.max(x, axis=0, keepdims=True)
        first = jnp.min(jnp.where(x == m, key, big), axis=0, keepdims=True)
        hit = key == first
        return (jnp.where(hit, -jnp.inf, x), jnp.where(hit, r, rank), jnp.where(slot == r, m, vals))

    init = (x, jnp.full((nr, nt), count, jnp.int32), jnp.zeros((count, nt), F32))
    _, rank, vals = lax.fori_loop(0, count, body, init)
    return vals, rank


def _pair_candidates(v1, v2):
    k = PEER_TOPK
    nt = v1.shape[1]
    j8 = lax.broadcasted_iota(jnp.int32, (8, nt), 0)
    j16 = lax.broadcasted_iota(jnp.int32, (k, nt), 0)
    sums, pos = [v1[0:1, :] + v2], [j16]
    for i in range(1, 8):
        sums.append(jnp.where(j8 < k // (i + 1), v1[i:i + 1, :] + v2[0:8, :], -jnp.inf))
        pos.append(i * k + j8)
    sums.append(v1[8:k, :] + v2[0:1, :])
    pos.append((8 + j8) * k)
    return jnp.concatenate(sums, axis=0), jnp.concatenate(pos, axis=0)


def _sorting_network(n):
    size = 1 << (n - 1).bit_length()
    pairs = []

    def merge(lo, hi, r):
        step = r * 2
        if step < hi - lo:
            merge(lo, hi, step)
            merge(lo + r, hi, step)
            pairs.extend((i, i + r) for i in range(lo + r, hi - r, step))
        else:
            pairs.append((lo, lo + r))

    def sort(lo, hi):
        if hi - lo >= 1:
            mid = lo + (hi - lo) // 2
            sort(lo, mid)
            sort(mid + 1, hi)
            merge(lo, hi, 1)

    sort(0, size - 1)
    return [(i, j) for i, j in pairs if j < n]


def _take_top_values(x, count):
    sub = 8
    tiles = [x[i:i + sub, :] for i in range(0, x.shape[0], sub)]
    for i, j in _sorting_network(len(tiles)):
        tiles[i], tiles[j] = jnp.maximum(tiles[i], tiles[j]), jnp.minimum(tiles[i], tiles[j])
    slot = lax.broadcasted_iota(jnp.int32, (count, x.shape[1]), 0)
    vals = jnp.zeros((count, x.shape[1]), F32)
    for r in range(count):
        m = jnp.max(tiles[0], axis=0, keepdims=True)
        vals = jnp.where(slot == r, m, vals)
        pop = tiles[0] == m
        depth = min(len(tiles), count - r)
        for k in range(depth - 1):
            tiles[k] = jnp.where(pop, tiles[k + 1], tiles[k])
        tiles[depth - 1] = jnp.where(pop, -jnp.inf, tiles[depth - 1])
    return vals


def _count_rows(mask):
    return jnp.sum(jnp.where(mask, 1.0, 0.0), axis=0, keepdims=True)


def _per_first_key(sel):
    k = PEER_TOPK
    cnt = jnp.where(sel, 1.0, 0.0)
    out = [jnp.sum(cnt[0:k, :], axis=0, keepdims=True)]
    out += [jnp.sum(cnt[k + 8 * (i - 1):k + 8 * i, :], axis=0, keepdims=True) for i in range(1, 8)]
    out += [cnt[k + 56 + i:k + 57 + i, :] for i in range(k - 8)]
    return out


def _route_tables_exact(s1, s2):
    k = PEER_TOPK
    key = lax.broadcasted_iota(jnp.int32, s1.shape, 0)
    v1, rank1 = _take_top(s1, key, k)
    v2, rank2 = _take_top(s2, key, k)
    cand, pos = _pair_candidates(v1, v2)
    _, crank = _take_top(cand, pos, k)
    sel = crank < k
    z = jnp.sum(jnp.where(sel, jnp.exp(cand - (v1[0:1, :] + v2[0:1, :])), 0.0), axis=0, keepdims=True)
    n = jnp.zeros(s1.shape, F32)
    for i, n_i in enumerate(_per_first_key(sel)):
        n = jnp.where(rank1 == i, n_i, n)
    e1 = jnp.where(rank1 < k, jnp.exp(s1 - v1[0:1, :]) / z, 0.0)
    e2 = jnp.where(rank2 < k, jnp.exp(s2 - v2[0:1, :]), 0.0)
    return n, e1, rank2.astype(F32), e2


def _route_tables_distinct(s1, s2):
    k = PEER_TOPK
    v1 = _take_top_values(s1, k)
    v2 = _take_top_values(s2, k)
    in1 = s1 >= v1[k - 1:k, :]
    in2 = s2 >= v2[k - 1:k, :]
    cand, _ = _pair_candidates(v1, v2)
    cv = _take_top_values(cand, k)
    sel = cand >= cv[k - 1:k, :]
    ok = (_count_rows(in1) == k) & (_count_rows(in2) == k) & (_count_rows(sel) == k)
    z = jnp.sum(jnp.where(sel, jnp.exp(cand - (v1[0:1, :] + v2[0:1, :])), 0.0), axis=0, keepdims=True)
    n = jnp.zeros(s1.shape, F32)
    rank2 = jnp.full(s2.shape, float(k), F32)
    for i, n_i in enumerate(_per_first_key(sel)):
        n = jnp.where(s1 == v1[i:i + 1, :], n_i, n)
        rank2 = jnp.where(s2 == v2[i:i + 1, :], float(i), rank2)
    e1 = jnp.where(in1, jnp.exp(s1 - v1[0:1, :]) / z, 0.0)
    e2 = jnp.where(in2, jnp.exp(s2 - v2[0:1, :]), 0.0)
    return (n, e1, rank2, e2), ok


def _peer_route_kernel(q_ref, k1_ref, k2_ref, n_ref, e1_ref, r2_ref, e2_ref, s1_ref, s2_ref):
    q = q_ref[...]
    s1_ref[...] = _dot_nt(k1_ref[0], q[:, :PEER_HALF])
    s2_ref[...] = _dot_nt(k2_ref[0], q[:, PEER_HALF:])
    tt = q.shape[0]
    lw = min(LANES, tt)
    def store(sl, tables):
        n, e1, rank2, e2 = tables
        n_ref[0, :, sl] = n
        e1_ref[0, :, sl] = e1
        r2_ref[:, sl] = rank2.astype(BF16)
        e2_ref[:, sl] = e2.astype(BF16)

    blocks = [slice(lc * lw, (lc + 1) * lw) for lc in range(tt // lw)]
    checks = []
    for sl in blocks:
        tables, ok = _route_tables_distinct(s1_ref[:, sl], s2_ref[:, sl])
        store(sl, tables)
        checks.append(jnp.min(jnp.where(ok, 1, 0)) == 0)

    for sl, redo in zip(blocks, checks):
        @pl.when(redo)
        def _(sl=sl):
            store(sl, _route_tables_exact(s1_ref[:, sl], s2_ref[:, sl]))


def _peer_route(q, keys1, keys2, tt):
    t = q.shape[0]
    out_a = jax.ShapeDtypeStruct((PEER_HEADS, PEER_NKEYS, t), F32)
    out_b = jax.ShapeDtypeStruct((PEER_HEADS * PEER_NKEYS, t), BF16)
    spec_a = pl.BlockSpec((1, PEER_NKEYS, tt), lambda i, h: (h, 0, i))
    spec_b = pl.BlockSpec((PEER_NKEYS, tt), lambda i, h: (h, i))
    key_spec = pl.BlockSpec((1, PEER_NKEYS, PEER_HALF), lambda i, h: (h, 0, 0))
    return pl.pallas_call(
        _peer_route_kernel,
        out_shape=(out_a, out_a, out_b, out_b),
        grid=(t // tt, PEER_HEADS),
        in_specs=[pl.BlockSpec((tt, PEER_DQ), lambda i, h: (i, h)), key_spec, key_spec],
        out_specs=(spec_a, spec_a, spec_b, spec_b),
        scratch_shapes=[pltpu.VMEM((PEER_NKEYS, tt), F32), pltpu.VMEM((PEER_NKEYS, tt), F32)],
        compiler_params=_params(("parallel", "arbitrary")),
        name="peer_route",
    )(q, keys1, keys2)


def _peer_dense_kernel(xa_ref, xc_ref, lnw_ref, fw_ref, n_ref, e1_ref, r2_in_ref, e2_in_ref, u_ref, v_ref,
                       o_ref, h_ref, pre_ref, pre2_ref, act_ref, act2_ref, acc_ref, r2_ref, e2_ref,
                       *, rows_per_step, n_blocks, n_work, final_norm):
    g = pl.program_id(0)
    last = n_work - 1
    ja = jnp.minimum(g, last) % n_blocks
    jb = jnp.clip(g - 1, 0, last) % n_blocks
    jc = jnp.clip(g - 2, 0, last) % n_blocks

    @pl.when(g == 0)
    def _():
        for ref in (pre_ref, pre2_ref, act_ref, act2_ref):
            ref[...] = jnp.zeros_like(ref)

    @pl.when(ja == 0)
    def _():
        x = xa_ref[...]
        y = x * lax.rsqrt(jnp.mean(x * x, axis=-1, keepdims=True) + EPS)
        h_ref[...] = (y * lnw_ref[...]).astype(BF16)

    @pl.when(jb == 0)
    def _():
        r2_ref[...] = r2_in_ref[...]
        e2_ref[...] = e2_in_ref[...]

    @pl.when(jc == 0)
    def _():
        acc_ref[...] = jnp.zeros_like(acc_ref)

    tt = h_ref.shape[0]
    lw = min(LANES, tt)
    reps = PEER_NKEYS // BF16_ROWS

    def stages(pre_w, pre_r, act_w, act_r):
        zero = jnp.zeros((PEER_NKEYS, lw), BF16)
        mxu_width = min(256, tt)

        def stage_a(part):
            tok = slice(part * mxu_width, (part + 1) * mxu_width)
            pre_w[:, tok] = lax.dot_general(u_ref[...], h_ref[tok, :], _NT, preferred_element_type=F32)

        def stage_b(lc, aa):
            sl = slice(lc * lw, (lc + 1) * lw)
            rows = slice(aa * PEER_NKEYS, (aa + 1) * PEER_NKEYS)
            gate = zero
            for h in range(PEER_HEADS):
                n_row = jnp.broadcast_to(n_ref[h, aa:aa + 1, sl], (BF16_ROWS, lw)).astype(BF16)
                e1_row = jnp.broadcast_to(e1_ref[h, aa:aa + 1, sl], (BF16_ROWS, lw)).astype(BF16)
                n_tile = jnp.concatenate([n_row] * reps, axis=0)
                e1_tile = jnp.concatenate([e1_row] * reps, axis=0)
                keys = slice(h * PEER_NKEYS, (h + 1) * PEER_NKEYS)
                gate = gate + jnp.where(r2_ref[keys, sl] < n_tile, e2_ref[keys, sl], zero) * e1_tile
            act = _gelu_tanh(pre_r[rows, sl].astype(BF16)) * gate
            act_w[sl, rows] = act.T

        def stage_c(part):
            col = slice(part * 256, (part + 1) * 256)
            acc_ref[:, col] += jnp.dot(act_r[...], v_ref[:, col], preferred_element_type=F32)

        mxu = [functools.partial(stage_a, p) for p in range(tt // mxu_width)]
        mxu += [functools.partial(stage_c, p) for p in range(acc_ref.shape[1] // 256)]
        vpu = [functools.partial(stage_b, lc, aa) for lc in range(tt // lw) for aa in range(rows_per_step)]
        per = -(-len(vpu) // len(mxu))
        for i, piece in enumerate(mxu):
            piece()
            for tile in vpu[i * per:(i + 1) * per]:
                tile()

    @pl.when(g % 2 == 0)
    def _():
        stages(pre_ref, pre2_ref, act2_ref, act_ref)

    @pl.when(g % 2 == 1)
    def _():
        stages(pre2_ref, pre_ref, act_ref, act2_ref)

    @pl.when((jc == n_blocks - 1) & (g >= 2))
    def _():
        r = xc_ref[...] + acc_ref[...]
        if final_norm:
            r = r * lax.rsqrt(jnp.mean(r * r, axis=-1, keepdims=True) + EPS) * fw_ref[...]
        o_ref[...] = r


def _peer_dense(x, lnw, route, u, v, layer, tt, rows_per_step, final_w):
    t, d = x.shape
    eb = rows_per_step * PEER_NKEYS
    n_blocks = PEER_N // eb
    n_work = (t // tt) * n_blocks

    def item(lag):
        def at(g):
            w = jnp.clip(g - lag, 0, n_work - 1)
            return w // n_blocks, w % n_blocks
        return at

    a, b, c = item(0), item(1), item(2)
    spec_rows = pl.BlockSpec((PEER_HEADS, rows_per_step, tt), lambda g: (0, b(g)[1], b(g)[0]))
    spec_keys = pl.BlockSpec((PEER_HEADS * PEER_NKEYS, tt), lambda g: (0, b(g)[0]))
    return pl.pallas_call(
        functools.partial(_peer_dense_kernel, rows_per_step=rows_per_step, n_blocks=n_blocks, n_work=n_work,
                          final_norm=final_w is not None),
        out_shape=jax.ShapeDtypeStruct((t, d), F32),
        grid=(n_work + 2,),
        in_specs=[pl.BlockSpec((tt, d), lambda g: (a(g)[0], 0)),
                  pl.BlockSpec((tt, d), lambda g: (c(g)[0], 0)),
                  pl.BlockSpec((1, d), lambda g: (0, 0)),
                  pl.BlockSpec((1, d), lambda g: (0, 0)),
                  spec_rows, spec_rows, spec_keys, spec_keys,
                  pl.BlockSpec((None, eb, d), lambda g: (layer, a(g)[1], 0)),
                  pl.BlockSpec((None, eb, d), lambda g: (layer, c(g)[1], 0))],
        out_specs=pl.BlockSpec((tt, d), lambda g: (c(g)[0], 0)),
        scratch_shapes=[pltpu.VMEM((tt, d), BF16), pltpu.VMEM((eb, tt), F32), pltpu.VMEM((eb, tt), F32),
                        pltpu.VMEM((tt, eb), BF16), pltpu.VMEM((tt, eb), BF16), pltpu.VMEM((tt, d), F32),
                        pltpu.VMEM((PEER_HEADS * PEER_NKEYS, tt), BF16),
                        pltpu.VMEM((PEER_HEADS * PEER_NKEYS, tt), BF16)],
        compiler_params=_params(("arbitrary",)),
        name="peer_dense",
    )(x, x, lnw.reshape(1, d), (lnw if final_w is None else final_w).reshape(1, d), *route, u, v)


def _peer_ffn(x, lnw, wq, keys1, keys2, u, v, layer, final_w=None):
    tt = min(512, x.shape[0])
    q = _norm_matmul(x, lnw, wq)
    route = _peer_route(q, keys1, keys2, tt)
    return _peer_dense(x, lnw, route, u, v, layer, tt, rows_per_step=16, final_w=final_w)


def _pad_lanes(vec, offset):
    return jnp.zeros((SMALL_W,), F32).at[offset:offset + vec.shape[0]].set(vec.astype(F32))


def _conv_history(buf):
    return jnp.pad(buf, ((0, 0), (CONV_PAD - (CONV_W - 1), 0), (0, 0)))


def kernel(x_prompt, x_sample, state_dn_conv, state_dn, state_ssm_conv, state_ssm, state_gla, ln_mix, ln_ffn, ln_final, w_in_even, w_out_even, dn_conv_w, dn_a_log, dn_dt_bias, dn_norm_w, ssm_conv_w, ssm_conv_b, ssm_a_log, ssm_dt_bias, ssm_d, ssm_norm_w, w_in_odd, gla_w_gate2, gla_b_gate, gla_norm_w, w_out_odd, peer_w_q, peer_keys1, peer_keys2, peer_u, peer_v):
    we = w_in_even[0]
    o = 0
    cuts = {}
    for name, size in (("qkv", DN_CONV_DIM), ("a", DN_HEADS), ("b", DN_HEADS), ("gate", DN_V),
                       ("z", SSM_INNER), ("xbc", SSM_CONV_DIM), ("dt", SSM_HEADS)):
        cuts[name] = we[:, o:o + size]
        o += size
    small_w = jnp.concatenate([cuts["a"], cuts["b"], cuts["dt"]], axis=1)
    small_w = jnp.pad(small_w, ((0, 0), (0, SMALL_W - small_w.shape[1])))
    w_even = jnp.concatenate([cuts["qkv"], cuts["gate"], cuts["xbc"], cuts["z"], small_w], axis=1).astype(BF16)
    wo = w_in_odd[0]
    qkv_w = 2 * GLA_K + GLA_V
    glr_w = jnp.pad(wo[:, qkv_w:qkv_w + GLA_LR], ((0, 0), (0, SMALL_W - GLA_LR)))
    w_odd = jnp.concatenate([wo[:, :qkv_w], wo[:, qkv_w + GLA_LR:], glr_w], axis=1).astype(BF16)
    w_out_dn = w_out_even[0, :DN_V].astype(BF16)
    w_out_ssm = w_out_even[0, DN_V:].astype(BF16)
    w_out_gla = w_out_odd[0].astype(BF16)
    wg = jnp.pad(gla_w_gate2[0], ((0, SMALL_W - GLA_LR), (0, 0))).astype(BF16)
    wq = peer_w_q.astype(BF16)
    k1 = peer_keys1.astype(BF16)
    k2 = peer_keys2.astype(BF16)
    pu = peer_u.astype(BF16)
    pv = peer_v.astype(BF16)

    zero8 = jnp.zeros((8 - 2, SMALL_W), F32)
    dn_par = jnp.concatenate([_pad_lanes(dn_a_log[0], 0)[None], _pad_lanes(dn_dt_bias[0], 0)[None], zero8])
    lane0 = 2 * DN_HEADS
    ssm_par = jnp.concatenate([_pad_lanes(ssm_a_log[0], lane0)[None], _pad_lanes(ssm_dt_bias[0], lane0)[None],
                               _pad_lanes(ssm_d[0], lane0)[None], zero8[1:]])

    def trunk(x, batch, seq_len, conv_dn0, s_dn0, conv_ssm0, s_ssm0, s_gla0):
        chunk = PROMPT_CHUNK if seq_len >= PROMPT_CHUNK else STEP_CHUNK
        nc = -(-seq_len // chunk)
        padded = nc * chunk != seq_len

        def stacked(heads):
            nb = max(1, min(batch, 256 // (heads * chunk)))
            while batch % nb:
                nb -= 1
            return nb

        dn_nb, ssd_nb = stacked(DN_HEADS), stacked(SSM_HEADS)

        def to_seq(a):
            if not padded:
                return a
            a = a.reshape(batch, seq_len, a.shape[-1])
            return jnp.pad(a, ((0, 0), (0, nc * chunk - seq_len), (0, 0))).reshape(batch * nc * chunk, -1)

        def from_seq(a):
            if not padded:
                return a
            return a.reshape(batch, nc * chunk, a.shape[-1])[:, :seq_len].reshape(batch * seq_len, -1)

        def last_rows(buf, proj, col, width):
            tail = proj.reshape(batch, seq_len, proj.shape[-1])[:, -min(seq_len, CONV_W - 1):, col:col + width]
            return jnp.concatenate([buf, tail], axis=1)[:, -(CONV_W - 1):]

        seq = dict(batch=batch, seq_len=seq_len, chunk=chunk)

        proj = _norm_matmul(x, ln_mix[0], w_even)
        proj_seq = to_seq(proj)
        o_dn, s_dn = _dn_mixer(proj_seq.reshape(batch, nc * chunk, EV_WIDTH), _conv_history(conv_dn0[0]),
                               s_dn0[0], dn_conv_w[0], dn_par, dn_norm_w[0].reshape(1, DN_DV),
                               nb=dn_nb, **seq)
        o_dn = o_dn.reshape(batch * nc * chunk, DN_V)
        y_ssm, s_ssm = _ssd_mixer(proj_seq.reshape(batch, nc * chunk, EV_WIDTH), _conv_history(conv_ssm0[0]),
                                  s_ssm0[0], ssm_conv_w[0], ssm_conv_b[0].reshape(1, SSM_CONV_DIM), ssm_par,
                                  ssm_norm_w[0].reshape(1, SSM_INNER), nb=ssd_nb, **seq)
        y_ssm = y_ssm.reshape(batch * nc * chunk, SSM_INNER)
        conv_dn = last_rows(conv_dn0[0], proj, EV_QKV, DN_CONV_DIM)
        conv_ssm = last_rows(conv_ssm0[0], proj, EV_XBC, SSM_CONV_DIM)
        x = _proj_residual(x, [from_seq(o_dn), from_seq(y_ssm)], [w_out_dn, w_out_ssm])
        x = _peer_ffn(x, ln_ffn[0], wq[0], k1[0], k2[0], pu, pv, 0)

        proj = _norm_matmul(x, ln_mix[1], w_odd)
        o_gla, s_gla = _gla_mixer(to_seq(proj).reshape(batch, nc * chunk, OD_WIDTH), s_gla0[0], wg,
                                  gla_b_gate[0].reshape(1, GLA_K), gla_norm_w[0].reshape(1, GLA_DV),
                                  nb=dn_nb, **seq)
        x = _proj_residual(x, [from_seq(o_gla.reshape(batch * nc * chunk, GLA_V))], [w_out_gla])
        x = _peer_ffn(x, ln_ffn[1], wq[1], k1[1], k2[1], pu, pv, 1, final_w=ln_final)

        y = x.reshape(batch, seq_len, D_MODEL)
        return y, conv_dn[None], s_dn[None], conv_ssm[None], s_ssm[None], s_gla[None]

    bp, lp = x_prompt.shape[:2]
    bs, ls = x_sample.shape[:2]

    def zeros_like_state(s):
        return jnp.zeros((s.shape[0], bp) + s.shape[2:], s.dtype)

    prompt = trunk(x_prompt.reshape(bp * lp, D_MODEL), bp, lp, zeros_like_state(state_dn_conv),
                   zeros_like_state(state_dn), zeros_like_state(state_ssm_conv),
                   zeros_like_state(state_ssm), zeros_like_state(state_gla))
    sample = trunk(x_sample.reshape(bs * ls, D_MODEL), bs, ls, state_dn_conv, state_dn,
                   state_ssm_conv, state_ssm, state_gla)
    return (prompt[0], sample[0]) + prompt[1:] + sample[1:]
```

```python
import functools

import jax
import jax.numpy as jnp
from jax import lax
from jax.experimental import pallas as pl
from jax.experimental.pallas import tpu as pltpu

F32 = jnp.float32
BF16 = jnp.bfloat16
EPS = 1e-6

D_MODEL = 1024
CONV_W = 4
CONV_PAD = 8
PROMPT_CHUNK = 64
STEP_CHUNK = 8

DN_HEADS, DN_DK, DN_DV = 4, 128, 128
DN_QK = DN_HEADS * DN_DK
DN_V = DN_HEADS * DN_DV
DN_CONV_DIM = 2 * DN_QK + DN_V
SSM_HEADS, SSM_P, SSM_GROUPS, SSM_N = 8, 64, 2, 128
SSM_INNER = SSM_HEADS * SSM_P
SSM_CONV_DIM = SSM_INNER + 2 * SSM_GROUPS * SSM_N
GLA_HEADS, GLA_DK, GLA_DV, GLA_LR = 4, 128, 256, 16
GLA_K = GLA_HEADS * GLA_DK
GLA_V = GLA_HEADS * GLA_DV
GLA_NORMALIZER = 16.0
PEER_HEADS, PEER_DQ, PEER_NKEYS, PEER_TOPK = 8, 256, 128, 16
PEER_HALF = PEER_DQ // 2
PEER_N = PEER_NKEYS * PEER_NKEYS
LANES = 128
BF16_ROWS = 16
SMALL_W = LANES

EV_QKV, EV_GATE, EV_XBC, EV_Z, EV_SMALL = 0, 1536, 2048, 3072, 3584
EV_WIDTH = EV_SMALL + SMALL_W
OD_QKV, OD_OG, OD_GLR = 0, 2048, 3072
OD_WIDTH = OD_GLR + SMALL_W

VMEM_LIMIT = 56 * 1024 * 1024

_NT = (((1,), (1,)), ((), ()))
_TN = (((0,), (0,)), ((), ()))


def _dot(a, b):
    return jnp.dot(a.astype(BF16), b.astype(BF16), preferred_element_type=F32)


def _dot_nt(a, b):
    return lax.dot_general(a.astype(BF16), b.astype(BF16), _NT, preferred_element_type=F32)


def _dot_tn(a, b):
    return lax.dot_general(a.astype(BF16), b.astype(BF16), _TN, preferred_element_type=F32)


def _params(sem):
    return pltpu.CompilerParams(dimension_semantics=sem, vmem_limit_bytes=VMEM_LIMIT)


def _silu(x):
    return x * jax.nn.sigmoid(x)


def _gelu_tanh(x):
    c0 = 0.7978845608028654
    hx = 0.5 * x
    return hx + hx * jnp.tanh(x * (c0 + (c0 * 0.044715) * (x * x)))


def _softplus(x):
    return jnp.maximum(x, 0.0) + jnp.log(1.0 + jnp.exp(-jnp.abs(x)))


def _cumsum_rows(x):
    n = x.shape[0]
    row = lax.broadcasted_iota(jnp.int32, x.shape, 0)
    s = 1
    while s < n:
        x = x + jnp.where(row >= s, pltpu.roll(x, s, axis=0), 0.0)
        s *= 2
    return x


def _col_to_row(col, eye):
    return jnp.sum(jnp.where(eye, col, 0.0), axis=0, keepdims=True)


def _row_to_col(row, eye):
    return jnp.sum(jnp.where(eye, row, 0.0), axis=1, keepdims=True)


def _tri_masks(n):
    ri = lax.broadcasted_iota(jnp.int32, (n, n), 0)
    ci = lax.broadcasted_iota(jnp.int32, (n, n), 1)
    return ri, ci, ri == ci, ri >= ci, ri > ci


def _valid_rows(chunk, seq_len, shape):
    if seq_len % chunk == 0:
        return None
    pos = pl.program_id(1) * chunk + lax.broadcasted_iota(jnp.int32, shape, 0)
    return pos < seq_len


def _causal_conv(x_ref, xext, cw_ref, chunk):
    x = x_ref[...]
    xext[CONV_PAD:CONV_PAD + chunk, :] = x
    cw = cw_ref[...]
    pre = cw[CONV_W - 1:CONV_W, :] * x
    for j in range(1, CONV_W):
        pre = pre + cw[CONV_W - 1 - j:CONV_W - j, :] * xext[CONV_PAD - j:CONV_PAD - j + chunk, :]
    xext[0:CONV_PAD, :] = xext[chunk:chunk + CONV_PAD, :]
    return pre


def _norm_matmul_kernel(x_ref, lnw_ref, w_ref, o_ref):
    x = x_ref[...]
    y = x * lax.rsqrt(jnp.mean(x * x, axis=-1, keepdims=True) + EPS)
    h = (y * lnw_ref[...]).astype(BF16)
    o_ref[...] = jnp.dot(h, w_ref[...], preferred_element_type=F32)


def _norm_matmul(x, lnw, w):
    t, d = x.shape
    n = w.shape[1]
    tm = min(512, t)
    return pl.pallas_call(
        _norm_matmul_kernel,
        out_shape=jax.ShapeDtypeStruct((t, n), F32),
        grid=(t // tm,),
        in_specs=[pl.BlockSpec((tm, d), lambda i: (i, 0)),
                  pl.BlockSpec((1, d), lambda i: (0, 0)),
                  pl.BlockSpec((d, n), lambda i: (0, 0))],
        out_specs=pl.BlockSpec((tm, n), lambda i: (i, 0)),
        compiler_params=_params(("parallel",)),
        name="norm_matmul",
    )(x, lnw.reshape(1, d), w)


def _proj_residual_kernel(*refs, n_in):
    x_ref, o_ref = refs[0], refs[-1]
    acc = x_ref[...]
    for a_ref, w_ref in zip(refs[1:1 + n_in], refs[1 + n_in:1 + 2 * n_in]):
        acc = acc + jnp.dot(a_ref[...].astype(BF16), w_ref[...], preferred_element_type=F32)
    o_ref[...] = acc


def _proj_residual(x, acts, weights):
    t, d = x.shape
    tm = min(512, t)
    n_in = len(acts)
    in_specs = [pl.BlockSpec((tm, d), lambda i: (i, 0))]
    in_specs += [pl.BlockSpec((tm, a.shape[1]), lambda i: (i, 0)) for a in acts]
    in_specs += [pl.BlockSpec(w.shape, lambda i: (0, 0)) for w in weights]
    return pl.pallas_call(
        functools.partial(_proj_residual_kernel, n_in=n_in),
        out_shape=jax.ShapeDtypeStruct((t, d), F32),
        grid=(t // tm,),
        in_specs=in_specs,
        out_specs=pl.BlockSpec((tm, d), lambda i: (i, 0)),
        compiler_params=_params(("parallel",)),
        name="proj_residual",
    )(x, *acts, *weights)


def _unit_lower_inverse(m, ri, ci, eye, block):
    t = jnp.where(eye, 1.0, 0.0)
    level = 0
    while (1 << level) < block:
        pair = ((ri >> (level + 1)) == (ci >> (level + 1))) & ((ri >> level) != (ci >> level))
        c = jnp.where(pair, m, 0.0)
        t = t - c if level == 0 else t - _dot(_dot(t, c), t)
        level += 1
    return t


def _dn_kernel(qkv_ref, gate_ref, small_ref, conv0_ref, s0_ref, cw_ref, par_ref, nw_ref,
               o_ref, s_out_ref, xext, state, *, chunk, seq_len, nb):
    c_idx = pl.program_id(1)
    groups = nb * DN_HEADS
    log2c = chunk.bit_length() - 1
    group_lanes = lambda g: slice(g * DN_DV, (g + 1) * DN_DV)

    @pl.when(c_idx == 0)
    def _():
        for s in range(nb):
            xext[s, 0:CONV_PAD, :] = conv0_ref[s]
            for h in range(DN_HEADS):
                state[:, group_lanes(s * DN_HEADS + h)] = s0_ref[s, h]

    par = par_ref[...]
    qs, ks, vs, gcs, betas, gates = [], [], [], [], [], []
    for s in range(nb):
        act = _silu(_causal_conv(qkv_ref.at[s], xext.at[s], cw_ref, chunk))
        small = small_ref[s]
        g_all = -jnp.exp(par[0:1, :]) * _softplus(small + par[1:2, :])
        beta_all = jax.nn.sigmoid(small)
        valid = _valid_rows(chunk, seq_len, small.shape)
        if valid is not None:
            g_all = jnp.where(valid, g_all, 0.0)
            beta_all = jnp.where(valid, beta_all, 0.0)
        gc_all = _cumsum_rows(g_all)
        for h in range(DN_HEADS):
            xq = act[:, h * DN_DK:(h + 1) * DN_DK]
            xk = act[:, DN_QK + h * DN_DK:DN_QK + (h + 1) * DN_DK]
            qs.append(xq * lax.rsqrt(jnp.sum(xq * xq, axis=-1, keepdims=True) + EPS) * (DN_DK ** -0.5))
            ks.append(xk * lax.rsqrt(jnp.sum(xk * xk, axis=-1, keepdims=True) + EPS))
            vs.append(act[:, 2 * DN_QK + h * DN_DV:2 * DN_QK + (h + 1) * DN_DV])
            gcs.append(gc_all[:, h:h + 1])
            betas.append(beta_all[:, DN_HEADS + h:DN_HEADS + h + 1])
            gates.append(gate_ref[s, :, h * DN_DV:(h + 1) * DN_DV])
    q, k, v = (jnp.concatenate(t, axis=0) for t in (qs, ks, vs))
    gc, beta = (jnp.concatenate(t, axis=0) for t in (gcs, betas))
    gc_last = jnp.concatenate([jnp.broadcast_to(t[chunk - 1:chunk, :], (chunk, 1)) for t in gcs], axis=0)
    last_decay = jnp.concatenate(
        [jnp.broadcast_to(jnp.exp(t[chunk - 1:chunk, :]), (1, DN_DV)) for t in gcs], axis=1)

    rows = groups * chunk
    ri, ci, eye, lower, strict_lower = _tri_masks(rows)
    same = (ri >> log2c) == (ci >> log2c)
    causal = same & lower
    strict = same & strict_lower
    decay = jnp.where(causal, jnp.exp(gc - _col_to_row(gc, eye)), 0.0)
    kb = k * beta
    eg = jnp.exp(gc)
    m = jnp.where(strict, _dot_nt(kb, k) * decay, 0.0)
    t_inv = _unit_lower_inverse(m, ri, ci, eye, chunk)
    sol = _dot(t_inv, jnp.concatenate([v * beta, kb * eg], axis=1))
    u, w = sol[:, :DN_DV], sol[:, DN_DV:]
    qk = jnp.where(causal, _dot_nt(q, k) * decay, 0.0)

    def own_block(x):
        return jnp.concatenate(
            [x[g * chunk:(g + 1) * chunk, group_lanes(g)] for g in range(groups)], axis=0)

    s_all = state[...]
    v_new = u - own_block(_dot(w, s_all))
    o = own_block(_dot(q * eg, s_all)) + _dot(qk, v_new)
    wide = (rows, groups * DN_DV)
    row_group = lax.broadcasted_iota(jnp.int32, wide, 0) >> log2c
    lane_group = lax.broadcasted_iota(jnp.int32, wide, 1) // DN_DV
    v_wide = jnp.where(row_group == lane_group, jnp.concatenate([v_new] * groups, axis=1), 0.0)
    state[...] = s_all * last_decay + _dot_tn(k * jnp.exp(gc_last - gc), v_wide)

    o = o * lax.rsqrt(jnp.mean(o * o, axis=-1, keepdims=True) + EPS) * nw_ref[...]
    y = o * _silu(jnp.concatenate(gates, axis=0))
    for s in range(nb):
        for h in range(DN_HEADS):
            g = s * DN_HEADS + h
            o_ref[s, :, h * DN_DV:(h + 1) * DN_DV] = y[g * chunk:(g + 1) * chunk, :]

    @pl.when(c_idx == pl.num_programs(1) - 1)
    def _():
        for s in range(nb):
            for h in range(DN_HEADS):
                s_out_ref[s, h] = state[:, group_lanes(s * DN_HEADS + h)]


def _dn_mixer(proj, conv0, s0, cw, par, nw, *, batch, seq_len, chunk, nb):
    nc = proj.shape[1] // chunk
    return pl.pallas_call(
        functools.partial(_dn_kernel, chunk=chunk, seq_len=seq_len, nb=nb),
        out_shape=(jax.ShapeDtypeStruct((batch, nc * chunk, DN_V), F32),
                   jax.ShapeDtypeStruct((batch, DN_HEADS, DN_DK, DN_DV), F32)),
        grid=(batch // nb, nc),
        in_specs=[pl.BlockSpec((nb, chunk, DN_CONV_DIM), lambda b, c: (b, c, EV_QKV // DN_CONV_DIM)),
                  pl.BlockSpec((nb, chunk, DN_V), lambda b, c: (b, c, EV_GATE // DN_V)),
                  pl.BlockSpec((nb, chunk, SMALL_W), lambda b, c: (b, c, EV_SMALL // SMALL_W)),
                  pl.BlockSpec((nb, CONV_PAD, DN_CONV_DIM), lambda b, c: (b, 0, 0)),
                  pl.BlockSpec((nb, DN_HEADS, DN_DK, DN_DV), lambda b, c: (b, 0, 0, 0)),
                  pl.BlockSpec((CONV_W, DN_CONV_DIM), lambda b, c: (0, 0)),
                  pl.BlockSpec((8, SMALL_W), lambda b, c: (0, 0)),
                  pl.BlockSpec((1, DN_DV), lambda b, c: (0, 0))],
        out_specs=(pl.BlockSpec((nb, chunk, DN_V), lambda b, c: (b, c, 0)),
                   pl.BlockSpec((nb, DN_HEADS, DN_DK, DN_DV), lambda b, c: (b, 0, 0, 0))),
        scratch_shapes=[pltpu.VMEM((nb, chunk + CONV_PAD, DN_CONV_DIM), F32),
                        pltpu.VMEM((DN_DK, nb * DN_HEADS * DN_DV), F32)],
        compiler_params=_params(("parallel", "arbitrary")),
        name="dn_mixer",
    )(proj, proj, proj, conv0, s0, cw, par, nw)


def _ssd_kernel(xbc_ref, z_ref, small_ref, conv0_ref, s0_ref, cw_ref, cb_ref, par_ref, nw_ref,
                y_ref, s_out_ref, xext, state, ybuf, *, chunk, seq_len, nb):
    c_idx = pl.program_id(1)
    groups = nb * SSM_HEADS
    log2c = chunk.bit_length() - 1
    group_rows = lambda g: slice(g * SSM_P, (g + 1) * SSM_P)

    @pl.when(c_idx == 0)
    def _():
        for s in range(nb):
            xext[s, 0:CONV_PAD, :] = conv0_ref[s]
            for h in range(SSM_HEADS):
                state[group_rows(s * SSM_HEADS + h), :] = s0_ref[s, h]

    par = par_ref[...]
    gw = SSM_GROUPS * SSM_N
    heads_per_group = SSM_HEADS // SSM_GROUPS
    xs, bms, cms, dts, acs, ds = [], [], [], [], [], []
    for s in range(nb):
        xbc = _silu(_causal_conv(xbc_ref.at[s], xext.at[s], cw_ref, chunk) + cb_ref[...])
        small = small_ref[s]
        dt_all = _softplus(small + par[1:2, :])
        valid = _valid_rows(chunk, seq_len, small.shape)
        if valid is not None:
            dt_all = jnp.where(valid, dt_all, 0.0)
        ac_all = _cumsum_rows(dt_all * -jnp.exp(par[0:1, :]))
        for h in range(SSM_HEADS):
            g = h // heads_per_group
            lane = 2 * DN_HEADS + h
            xs.append(xbc[:, h * SSM_P:(h + 1) * SSM_P])
            bms.append(xbc[:, SSM_INNER + g * SSM_N:SSM_INNER + (g + 1) * SSM_N])
            cms.append(xbc[:, SSM_INNER + gw + g * SSM_N:SSM_INNER + gw + (g + 1) * SSM_N])
            dts.append(dt_all[:, lane:lane + 1])
            acs.append(ac_all[:, lane:lane + 1])
            ds.append(jnp.broadcast_to(par[2:3, lane:lane + 1], (chunk, 1)))
    per_stack = max(1, min(groups, 256 // chunk))
    rows = per_stack * chunk
    ri, ci, eye, lower, strict_lower = _tri_masks(rows)
    causal = ((ri >> log2c) == (ci >> log2c)) & lower
    wide = (rows, per_stack * SSM_P)
    row_group = lax.broadcasted_iota(jnp.int32, wide, 0) >> log2c
    lane_group = lax.broadcasted_iota(jnp.int32, wide, 1) // SSM_P
    y_groups = []
    for lo in range(0, groups, per_stack):
        part = slice(lo, lo + per_stack)
        x, bm, cm = (jnp.concatenate(t[part], axis=0) for t in (xs, bms, cms))
        dt, ac, d_skip = (jnp.concatenate(t[part], axis=0) for t in (dts, acs, ds))
        ac_last = jnp.concatenate(
            [jnp.broadcast_to(t[chunk - 1:chunk, :], (chunk, 1)) for t in acs[part]], axis=0)
        last_decay = jnp.concatenate(
            [jnp.broadcast_to(jnp.exp(t[chunk - 1:chunk, :]), (SSM_P, 1)) for t in acs[part]], axis=0)
        decay = jnp.where(causal, jnp.exp(ac - _col_to_row(ac, eye)), 0.0)
        xdt = x * dt
        state_rows = slice(lo * SSM_P, (lo + per_stack) * SSM_P)
        s_all = state[state_rows, :]
        off = _dot_nt(cm * jnp.exp(ac), s_all)
        y = _dot(_dot_nt(cm, bm) * decay, xdt) + jnp.concatenate(
            [off[g * chunk:(g + 1) * chunk, group_rows(g)] for g in range(per_stack)], axis=0)
        xdt_pair = jnp.concatenate([xdt, xdt], axis=1)
        xdt_wide = jnp.where(row_group == lane_group,
                             jnp.concatenate([xdt_pair] * (per_stack // 2), axis=1), 0.0)
        state[state_rows, :] = s_all * last_decay + _dot_tn(xdt_wide, bm * jnp.exp(ac_last - ac))
        y = y + d_skip * x
        y_groups += [y[g * chunk:(g + 1) * chunk, :] for g in range(per_stack)]

    nw = nw_ref[...]
    gn = SSM_INNER // SSM_GROUPS
    for s in range(nb):
        for h in range(SSM_HEADS):
            ybuf[:, h * SSM_P:(h + 1) * SSM_P] = y_groups[s * SSM_HEADS + h]
        ys = ybuf[...] * _silu(z_ref[s])
        for g in range(SSM_GROUPS):
            yg = ys[:, g * gn:(g + 1) * gn]
            yg = yg * lax.rsqrt(jnp.mean(yg * yg, axis=-1, keepdims=True) + EPS)
            y_ref[s, :, g * gn:(g + 1) * gn] = yg * nw[:, g * gn:(g + 1) * gn]

    @pl.when(c_idx == pl.num_programs(1) - 1)
    def _():
        for s in range(nb):
            for h in range(SSM_HEADS):
                s_out_ref[s, h] = state[group_rows(s * SSM_HEADS + h), :]


def _ssd_mixer(proj, conv0, s0, cw, cb, par, nw, *, batch, seq_len, chunk, nb):
    nc = proj.shape[1] // chunk
    return pl.pallas_call(
        functools.partial(_ssd_kernel, chunk=chunk, seq_len=seq_len, nb=nb),
        out_shape=(jax.ShapeDtypeStruct((batch, nc * chunk, SSM_INNER), F32),
                   jax.ShapeDtypeStruct((batch, SSM_HEADS, SSM_P, SSM_N), F32)),
        grid=(batch // nb, nc),
        in_specs=[pl.BlockSpec((nb, chunk, SSM_CONV_DIM), lambda b, c: (b, c, EV_XBC // SSM_CONV_DIM)),
                  pl.BlockSpec((nb, chunk, SSM_INNER), lambda b, c: (b, c, EV_Z // SSM_INNER)),
                  pl.BlockSpec((nb, chunk, SMALL_W), lambda b, c: (b, c, EV_SMALL // SMALL_W)),
                  pl.BlockSpec((nb, CONV_PAD, SSM_CONV_DIM), lambda b, c: (b, 0, 0)),
                  pl.BlockSpec((nb, SSM_HEADS, SSM_P, SSM_N), lambda b, c: (b, 0, 0, 0)),
                  pl.BlockSpec((CONV_W, SSM_CONV_DIM), lambda b, c: (0, 0)),
                  pl.BlockSpec((1, SSM_CONV_DIM), lambda b, c: (0, 0)),
                  pl.BlockSpec((8, SMALL_W), lambda b, c: (0, 0)),
                  pl.BlockSpec((1, SSM_INNER), lambda b, c: (0, 0))],
        out_specs=(pl.BlockSpec((nb, chunk, SSM_INNER), lambda b, c: (b, c, 0)),
                   pl.BlockSpec((nb, SSM_HEADS, SSM_P, SSM_N), lambda b, c: (b, 0, 0, 0))),
        scratch_shapes=[pltpu.VMEM((nb, chunk + CONV_PAD, SSM_CONV_DIM), F32),
                        pltpu.VMEM((nb * SSM_HEADS * SSM_P, SSM_N), F32),
                        pltpu.VMEM((chunk, SSM_INNER), F32)],
        compiler_params=_params(("parallel", "arbitrary")),
        name="ssd_mixer",
    )(proj, proj, proj, conv0, s0, cw, cb, par, nw)


def _gla_kernel(qkv_ref, og_ref, glr_ref, s0_ref, wg_ref, bg_ref, nw_ref,
                o_ref, s_out_ref, state, *, chunk, seq_len, nb):
    c_idx = pl.program_id(1)
    groups = nb * GLA_HEADS
    log2c = chunk.bit_length() - 1
    group_lanes = lambda g: slice(g * GLA_DV, (g + 1) * GLA_DV)

    @pl.when(c_idx == 0)
    def _():
        for s in range(nb):
            for h in range(GLA_HEADS):
                state[:, group_lanes(s * GLA_HEADS + h)] = s0_ref[s, h]

    qs, ks, vs, bs, gates = [], [], [], [], []
    for s in range(nb):
        x = bg_ref[...] + jnp.dot(glr_ref[s].astype(BF16), wg_ref[...], preferred_element_type=F32)
        gk = (jnp.minimum(x, 0.0) - jnp.log(1.0 + jnp.exp(-jnp.abs(x)))) / GLA_NORMALIZER
        valid = _valid_rows(chunk, seq_len, gk.shape)
        if valid is not None:
            gk = jnp.where(valid, gk, 0.0)
        b_all = _cumsum_rows(gk)
        for h in range(GLA_HEADS):
            qs.append(qkv_ref[s, :, h * GLA_DK:(h + 1) * GLA_DK] * (GLA_DK ** -0.5))
            ks.append(qkv_ref[s, :, GLA_K + h * GLA_DK:GLA_K + (h + 1) * GLA_DK])
            vs.append(qkv_ref[s, :, 2 * GLA_K + h * GLA_DV:2 * GLA_K + (h + 1) * GLA_DV])
            bs.append(b_all[:, h * GLA_DK:(h + 1) * GLA_DK])
            gates.append(og_ref[s, :, h * GLA_DV:(h + 1) * GLA_DV])
    q, k, b = (jnp.concatenate(t, axis=0) for t in (qs, ks, bs))
    v = jnp.concatenate(vs, axis=0)
    b_last = jnp.concatenate([jnp.broadcast_to(t[chunk - 1:chunk, :], (chunk, GLA_DK)) for t in bs], axis=0)
    eye_k = _tri_masks(GLA_DK)[2]
    last_decay = jnp.concatenate(
        [jnp.broadcast_to(_row_to_col(jnp.exp(t[chunk - 1:chunk, :]), eye_k), (GLA_DK, GLA_DV)) for t in bs],
        axis=1)

    rows = groups * chunk
    ri, ci, eye, lower, strict_lower = _tri_masks(rows)
    causal = ((ri >> log2c) == (ci >> log2c)) & lower
    qe = q * jnp.exp(b)
    att = jnp.where(causal, _dot_nt(qe, k * jnp.exp(-b)), 0.0)

    s_all = state[...]
    qs_all = _dot(qe, s_all)
    o = jnp.concatenate([qs_all[g * chunk:(g + 1) * chunk, group_lanes(g)] for g in range(groups)], axis=0)
    o = o + _dot(att, v)
    wide = (rows, groups * GLA_DV)
    row_group = lax.broadcasted_iota(jnp.int32, wide, 0) >> log2c
    lane_group = lax.broadcasted_iota(jnp.int32, wide, 1) // GLA_DV
    v_wide = jnp.where(row_group == lane_group, jnp.concatenate([v] * groups, axis=1), 0.0)
    state[...] = s_all * last_decay + _dot_tn(k * jnp.exp(b_last - b), v_wide)

    o = o * lax.rsqrt(jnp.mean(o * o, axis=-1, keepdims=True) + EPS) * nw_ref[...]
    y = o * _silu(jnp.concatenate(gates, axis=0))
    for s in range(nb):
        for h in range(GLA_HEADS):
            g = s * GLA_HEADS + h
            o_ref[s, :, h * GLA_DV:(h + 1) * GLA_DV] = y[g * chunk:(g + 1) * chunk, :]

    @pl.when(c_idx == pl.num_programs(1) - 1)
    def _():
        for s in range(nb):
            for h in range(GLA_HEADS):
                s_out_ref[s, h] = state[:, group_lanes(s * GLA_HEADS + h)]


def _gla_mixer(proj, s0, wg, bg, nw, *, batch, seq_len, chunk, nb):
    nc = proj.shape[1] // chunk
    qkv_w = 2 * GLA_K + GLA_V
    return pl.pallas_call(
        functools.partial(_gla_kernel, chunk=chunk, seq_len=seq_len, nb=nb),
        out_shape=(jax.ShapeDtypeStruct((batch, nc * chunk, GLA_V), F32),
                   jax.ShapeDtypeStruct((batch, GLA_HEADS, GLA_DK, GLA_DV), F32)),
        grid=(batch // nb, nc),
        in_specs=[pl.BlockSpec((nb, chunk, qkv_w), lambda b, c: (b, c, OD_QKV // qkv_w)),
                  pl.BlockSpec((nb, chunk, GLA_V), lambda b, c: (b, c, OD_OG // GLA_V)),
                  pl.BlockSpec((nb, chunk, SMALL_W), lambda b, c: (b, c, OD_GLR // SMALL_W)),
                  pl.BlockSpec((nb, GLA_HEADS, GLA_DK, GLA_DV), lambda b, c: (b, 0, 0, 0)),
                  pl.BlockSpec((SMALL_W, GLA_K), lambda b, c: (0, 0)),
                  pl.BlockSpec((1, GLA_K), lambda b, c: (0, 0)),
                  pl.BlockSpec((1, GLA_DV), lambda b, c: (0, 0))],
        out_specs=(pl.BlockSpec((nb, chunk, GLA_V), lambda b, c: (b, c, 0)),
                   pl.BlockSpec((nb, GLA_HEADS, GLA_DK, GLA_DV), lambda b, c: (b, 0, 0, 0))),
        scratch_shapes=[pltpu.VMEM((GLA_DK, nb * GLA_HEADS * GLA_DV), F32)],
        compiler_params=_params(("parallel", "arbitrary")),
        name="gla_mixer",
    )(proj, proj, proj, s0, wg, bg, nw)


def _take_top(x, key, count):
    nr, nt = x.shape
    big = jnp.int32(2 ** 30)
    slot = lax.broadcasted_iota(jnp.int32, (count, nt), 0)

    def body(r, carry):
        x, rank, vals = carry
        m = jnp.max(x, axis=0, keepdims=True)
        first = jnp.min(jnp.where(x == m, key, big), axis=0, keepdims=True)
        hit = key == first
        return (jnp.where(hit, -jnp.inf, x), jnp.where(hit, r, rank), jnp.where(slot == r, m, vals))

    init = (x, jnp.full((nr, nt), count, jnp.int32), jnp.zeros((count, nt), F32))
    _, rank, vals = lax.fori_loop(0, count, body, init)
    return vals, rank


def _pair_candidates(v1, v2):
    k = PEER_TOPK
    nt = v1.shape[1]
    j8 = lax.broadcasted_iota(jnp.int32, (8, nt), 0)
    j16 = lax.broadcasted_iota(jnp.int32, (k, nt), 0)
    sums, pos = [v1[0:1, :] + v2], [j16]
    for i in range(1, 8):
        sums.append(jnp.where(j8 < k // (i + 1), v1[i:i + 1, :] + v2[0:8, :], -jnp.inf))
        pos.append(i * k + j8)
    sums.append(v1[8:k, :] + v2[0:1, :])
    pos.append((8 + j8) * k)
    return jnp.concatenate(sums, axis=0), jnp.concatenate(pos, axis=0)


def _sorting_network(n):
    size = 1 << (n - 1).bit_length()
    pairs = []

    def merge(lo, hi, r):
        step = r * 2
        if step < hi - lo:
            merge(lo, hi, step)
            merge(lo + r, hi, step)
            pairs.extend((i, i + r) for i in range(lo + r, hi - r, step))
        else:
            pairs.append((lo, lo + r))

    def sort(lo, hi):
        if hi - lo >= 1:
            mid = lo + (hi - lo) // 2
            sort(lo, mid)
            sort(mid + 1, hi)
            merge(lo, hi, 1)

    sort(0, size - 1)
    return [(i, j) for i, j in pairs if j < n]


def _take_top_values(x, count):
    sub = 8
    tiles = [x[i:i + sub, :] for i in range(0, x.shape[0], sub)]
    for i, j in _sorting_network(len(tiles)):
        tiles[i], tiles[j] = jnp.maximum(tiles[i], tiles[j]), jnp.minimum(tiles[i], tiles[j])
    slot = lax.broadcasted_iota(jnp.int32, (count, x.shape[1]), 0)
    vals = jnp.zeros((count, x.shape[1]), F32)
    for r in range(count):
        m = jnp.max(tiles[0], axis=0, keepdims=True)
        vals = jnp.where(slot == r, m, vals)
        pop = tiles[0] == m
        depth = min(len(tiles), count - r)
        for k in range(depth - 1):
            tiles[k] = jnp.where(pop, tiles[k + 1], tiles[k])
        tiles[depth - 1] = jnp.where(pop, -jnp.inf, tiles[depth - 1])
    return vals


def _count_rows(mask):
    return jnp.sum(jnp.where(mask, 1.0, 0.0), axis=0, keepdims=True)


def _per_first_key(sel):
    k = PEER_TOPK
    cnt = jnp.where(sel, 1.0, 0.0)
    out = [jnp.sum(cnt[0:k, :], axis=0, keepdims=True)]
    out += [jnp.sum(cnt[k + 8 * (i - 1):k + 8 * i, :], axis=0, keepdims=True) for i in range(1, 8)]
    out += [cnt[k + 56 + i:k + 57 + i, :] for i in range(k - 8)]
    return out


def _route_tables_exact(s1, s2):
    k = PEER_TOPK
    key = lax.broadcasted_iota(jnp.int32, s1.shape, 0)
    v1, rank1 = _take_top(s1, key, k)
    v2, rank2 = _take_top(s2, key, k)
    cand, pos = _pair_candidates(v1, v2)
    _, crank = _take_top(cand, pos, k)
    sel = crank < k
    z = jnp.sum(jnp.where(sel, jnp.exp(cand - (v1[0:1, :] + v2[0:1, :])), 0.0), axis=0, keepdims=True)
    n = jnp.zeros(s1.shape, F32)
    for i, n_i in enumerate(_per_first_key(sel)):
        n = jnp.where(rank1 == i, n_i, n)
    e1 = jnp.where(rank1 < k, jnp.exp(s1 - v1[0:1, :]) / z, 0.0)
    e2 = jnp.where(rank2 < k, jnp.exp(s2 - v2[0:1, :]), 0.0)
    return n, e1, rank2.astype(F32), e2


def _route_tables_distinct(s1, s2):
    k = PEER_TOPK
    v1 = _take_top_values(s1, k)
    v2 = _take_top_values(s2, k)
    in1 = s1 >= v1[k - 1:k, :]
    in2 = s2 >= v2[k - 1:k, :]
    cand, _ = _pair_candidates(v1, v2)
    cv = _take_top_values(cand, k)
    sel = cand >= cv[k - 1:k, :]
    ok = (_count_rows(in1) == k) & (_count_rows(in2) == k) & (_count_rows(sel) == k)
    z = jnp.sum(jnp.where(sel, jnp.exp(cand - (v1[0:1, :] + v2[0:1, :])), 0.0), axis=0, keepdims=True)
    n = jnp.zeros(s1.shape, F32)
    rank2 = jnp.full(s2.shape, float(k), F32)
    for i, n_i in enumerate(_per_first_key(sel)):
        n = jnp.where(s1 == v1[i:i + 1, :], n_i, n)
        rank2 = jnp.where(s2 == v2[i:i + 1, :], float(i), rank2)
    e1 = jnp.where(in1, jnp.exp(s1 - v1[0:1, :]) / z, 0.0)
    e2 = jnp.where(in2, jnp.exp(s2 - v2[0:1, :]), 0.0)
    return (n, e1, rank2, e2), ok


def _peer_route_kernel(q_ref, k1_ref, k2_ref, n_ref, e1_ref, r2_ref, e2_ref, s1_ref, s2_ref):
    q = q_ref[...]
    s1_ref[...] = _dot_nt(k1_ref[0], q[:, :PEER_HALF])
    s2_ref[...] = _dot_nt(k2_ref[0], q[:, PEER_HALF:])
    tt = q.shape[0]
    lw = min(LANES, tt)
    def store(sl, tables):
        n, e1, rank2, e2 = tables
        n_ref[0, :, sl] = n
        e1_ref[0, :, sl] = e1
        r2_ref[:, sl] = rank2.astype(BF16)
        e2_ref[:, sl] = e2.astype(BF16)

    blocks = [slice(lc * lw, (lc + 1) * lw) for lc in range(tt // lw)]
    checks = []
    for sl in blocks:
        tables, ok = _route_tables_distinct(s1_ref[:, sl], s2_ref[:, sl])
        store(sl, tables)
        checks.append(jnp.min(jnp.where(ok, 1, 0)) == 0)

    for sl, redo in zip(blocks, checks):
        @pl.when(redo)
        def _(sl=sl):
            store(sl, _route_tables_exact(s1_ref[:, sl], s2_ref[:, sl]))


def _peer_route(q, keys1, keys2, tt):
    t = q.shape[0]
    out_a = jax.ShapeDtypeStruct((PEER_HEADS, PEER_NKEYS, t), F32)
    out_b = jax.ShapeDtypeStruct((PEER_HEADS * PEER_NKEYS, t), BF16)
    spec_a = pl.BlockSpec((1, PEER_NKEYS, tt), lambda i, h: (h, 0, i))
    spec_b = pl.BlockSpec((PEER_NKEYS, tt), lambda i, h: (h, i))
    key_spec = pl.BlockSpec((1, PEER_NKEYS, PEER_HALF), lambda i, h: (h, 0, 0))
    return pl.pallas_call(
        _peer_route_kernel,
        out_shape=(out_a, out_a, out_b, out_b),
        grid=(t // tt, PEER_HEADS),
        in_specs=[pl.BlockSpec((tt, PEER_DQ), lambda i, h: (i, h)), key_spec, key_spec],
        out_specs=(spec_a, spec_a, spec_b, spec_b),
        scratch_shapes=[pltpu.VMEM((PEER_NKEYS, tt), F32), pltpu.VMEM((PEER_NKEYS, tt), F32)],
        compiler_params=_params(("parallel", "arbitrary")),
        name="peer_route",
    )(q, keys1, keys2)


def _peer_dense_kernel(xa_ref, xc_ref, lnw_ref, fw_ref, n_ref, e1_ref, r2_in_ref, e2_in_ref, u_ref, v_ref,
                       o_ref, h_ref, pre_ref, pre2_ref, act_ref, act2_ref, acc_ref, r2_ref, e2_ref,
                       *, rows_per_step, n_blocks, n_work, final_norm):
    g = pl.program_id(0)
    last = n_work - 1
    ja = jnp.minimum(g, last) % n_blocks
    jb = jnp.clip(g - 1, 0, last) % n_blocks
    jc = jnp.clip(g - 2, 0, last) % n_blocks

    @pl.when(g == 0)
    def _():
        for ref in (pre_ref, pre2_ref, act_ref, act2_ref):
            ref[...] = jnp.zeros_like(ref)

    @pl.when(ja == 0)
    def _():
        x = xa_ref[...]
        y = x * lax.rsqrt(jnp.mean(x * x, axis=-1, keepdims=True) + EPS)
        h_ref[...] = (y * lnw_ref[...]).astype(BF16)

    @pl.when(jb == 0)
    def _():
        r2_ref[...] = r2_in_ref[...]
        e2_ref[...] = e2_in_ref[...]

    @pl.when(jc == 0)
    def _():
        acc_ref[...] = jnp.zeros_like(acc_ref)

    tt = h_ref.shape[0]
    lw = min(LANES, tt)
    reps = PEER_NKEYS // BF16_ROWS

    def stages(pre_w, pre_r, act_w, act_r):
        zero = jnp.zeros((PEER_NKEYS, lw), BF16)

        def stage_b(lc, aa):
            sl = slice(lc * lw, (lc + 1) * lw)
            rows = slice(aa * PEER_NKEYS, (aa + 1) * PEER_NKEYS)
            gate = None
            for h in range(PEER_HEADS):
                n_row = jnp.broadcast_to(n_ref[h, aa:aa + 1, sl], (BF16_ROWS, lw)).astype(BF16)
                e1_row = jnp.broadcast_to(e1_ref[h, aa:aa + 1, sl], (BF16_ROWS, lw)).astype(BF16)
                n_tile = jnp.concatenate([n_row] * reps, axis=0)
                e1_tile = jnp.concatenate([e1_row] * reps, axis=0)
                keys = slice(h * PEER_NKEYS, (h + 1) * PEER_NKEYS)
                part = jnp.where(r2_ref[keys, sl] < n_tile, e2_ref[keys, sl], zero) * e1_tile
                gate = part if gate is None else gate + part
            act = _gelu_tanh(pre_r[rows, sl].astype(BF16)) * gate
            act_w[sl, rows] = act.T

        pre_w[...] = lax.dot_general(u_ref[...], h_ref[...], _NT, preferred_element_type=F32)
        for lc in range(tt // lw):
            for aa in range(rows_per_step):
                stage_b(lc, aa)
        acc_ref[...] += jnp.dot(act_r[...], v_ref[...], preferred_element_type=F32)

    @pl.when(g % 2 == 0)
    def _():
        stages(pre_ref, pre2_ref, act2_ref, act_ref)

    @pl.when(g % 2 == 1)
    def _():
        stages(pre2_ref, pre_ref, act_ref, act2_ref)

    @pl.when((jc == n_blocks - 1) & (g >= 2))
    def _():
        r = xc_ref[...] + acc_ref[...]
        if final_norm:
            r = r * lax.rsqrt(jnp.mean(r * r, axis=-1, keepdims=True) + EPS) * fw_ref[...]
        o_ref[...] = r


def _peer_dense(x, lnw, route, u, v, layer, tt, rows_per_step, final_w):
    t, d = x.shape
    eb = rows_per_step * PEER_NKEYS
    n_blocks = PEER_N // eb
    n_work = (t // tt) * n_blocks

    def item(lag):
        def at(g):
            w = jnp.clip(g - lag, 0, n_work - 1)
            return w // n_blocks, w % n_blocks
        return at

    a, b, c = item(0), item(1), item(2)
    spec_rows = pl.BlockSpec((PEER_HEADS, rows_per_step, tt), lambda g: (0, b(g)[1], b(g)[0]))
    spec_keys = pl.BlockSpec((PEER_HEADS * PEER_NKEYS, tt), lambda g: (0, b(g)[0]))
    return pl.pallas_call(
        functools.partial(_peer_dense_kernel, rows_per_step=rows_per_step, n_blocks=n_blocks, n_work=n_work,
                          final_norm=final_w is not None),
        out_shape=jax.ShapeDtypeStruct((t, d), F32),
        grid=(n_work + 2,),
        in_specs=[pl.BlockSpec((tt, d), lambda g: (a(g)[0], 0)),
                  pl.BlockSpec((tt, d), lambda g: (c(g)[0], 0)),
                  pl.BlockSpec((1, d), lambda g: (0, 0)),
                  pl.BlockSpec((1, d), lambda g: (0, 0)),
                  spec_rows, spec_rows, spec_keys, spec_keys,
                  pl.BlockSpec((None, eb, d), lambda g: (layer, a(g)[1], 0)),
                  pl.BlockSpec((None, eb, d), lambda g: (layer, c(g)[1], 0))],
        out_specs=pl.BlockSpec((tt, d), lambda g: (c(g)[0], 0)),
        scratch_shapes=[pltpu.VMEM((tt, d), BF16), pltpu.VMEM((eb, tt), F32), pltpu.VMEM((eb, tt), F32),
                        pltpu.VMEM((tt, eb), BF16), pltpu.VMEM((tt, eb), BF16), pltpu.VMEM((tt, d), F32),
                        pltpu.VMEM((PEER_HEADS * PEER_NKEYS, tt), BF16),
                        pltpu.VMEM((PEER_HEADS * PEER_NKEYS, tt), BF16)],
        compiler_params=_params(("arbitrary",)),
        name="peer_dense",
    )(x, x, lnw.reshape(1, d), (lnw if final_w is None else final_w).reshape(1, d), *route, u, v)


def _peer_ffn(x, lnw, wq, keys1, keys2, u, v, layer, final_w=None):
    tt = min(512, x.shape[0])
    q = _norm_matmul(x, lnw, wq)
    route = _peer_route(q, keys1, keys2, tt)
    return _peer_dense(x, lnw, route, u, v, layer, tt, rows_per_step=16, final_w=final_w)


def _pad_lanes(vec, offset):
    return jnp.zeros((SMALL_W,), F32).at[offset:offset + vec.shape[0]].set(vec.astype(F32))


def _conv_history(buf):
    return jnp.pad(buf, ((0, 0), (CONV_PAD - (CONV_W - 1), 0), (0, 0)))


def kernel(x_prompt, x_sample, state_dn_conv, state_dn, state_ssm_conv, state_ssm, state_gla, ln_mix, ln_ffn, ln_final, w_in_even, w_out_even, dn_conv_w, dn_a_log, dn_dt_bias, dn_norm_w, ssm_conv_w, ssm_conv_b, ssm_a_log, ssm_dt_bias, ssm_d, ssm_norm_w, w_in_odd, gla_w_gate2, gla_b_gate, gla_norm_w, w_out_odd, peer_w_q, peer_keys1, peer_keys2, peer_u, peer_v):
    we = w_in_even[0]
    o = 0
    cuts = {}
    for name, size in (("qkv", DN_CONV_DIM), ("a", DN_HEADS), ("b", DN_HEADS), ("gate", DN_V),
                       ("z", SSM_INNER), ("xbc", SSM_CONV_DIM), ("dt", SSM_HEADS)):
        cuts[name] = we[:, o:o + size]
        o += size
    small_w = jnp.concatenate([cuts["a"], cuts["b"], cuts["dt"]], axis=1)
    small_w = jnp.pad(small_w, ((0, 0), (0, SMALL_W - small_w.shape[1])))
    w_even = jnp.concatenate([cuts["qkv"], cuts["gate"], cuts["xbc"], cuts["z"], small_w], axis=1).astype(BF16)
    wo = w_in_odd[0]
    qkv_w = 2 * GLA_K + GLA_V
    glr_w = jnp.pad(wo[:, qkv_w:qkv_w + GLA_LR], ((0, 0), (0, SMALL_W - GLA_LR)))
    w_odd = jnp.concatenate([wo[:, :qkv_w], wo[:, qkv_w + GLA_LR:], glr_w], axis=1).astype(BF16)
    w_out_dn = w_out_even[0, :DN_V].astype(BF16)
    w_out_ssm = w_out_even[0, DN_V:].astype(BF16)
    w_out_gla = w_out_odd[0].astype(BF16)
    wg = jnp.pad(gla_w_gate2[0], ((0, SMALL_W - GLA_LR), (0, 0))).astype(BF16)
    wq = peer_w_q.astype(BF16)
    k1 = peer_keys1.astype(BF16)
    k2 = peer_keys2.astype(BF16)
    pu = peer_u.astype(BF16)
    pv = peer_v.astype(BF16)

    zero8 = jnp.zeros((8 - 2, SMALL_W), F32)
    dn_par = jnp.concatenate([_pad_lanes(dn_a_log[0], 0)[None], _pad_lanes(dn_dt_bias[0], 0)[None], zero8])
    lane0 = 2 * DN_HEADS
    ssm_par = jnp.concatenate([_pad_lanes(ssm_a_log[0], lane0)[None], _pad_lanes(ssm_dt_bias[0], lane0)[None],
                               _pad_lanes(ssm_d[0], lane0)[None], zero8[1:]])

    def trunk(x, batch, seq_len, conv_dn0, s_dn0, conv_ssm0, s_ssm0, s_gla0):
        chunk = PROMPT_CHUNK if seq_len >= PROMPT_CHUNK else STEP_CHUNK
        nc = -(-seq_len // chunk)
        padded = nc * chunk != seq_len

        def stacked(heads):
            nb = max(1, min(batch, 256 // (heads * chunk)))
            while batch % nb:
                nb -= 1
            return nb

        dn_nb, ssd_nb = stacked(DN_HEADS), stacked(SSM_HEADS)

        def to_seq(a):
            if not padded:
                return a
            a = a.reshape(batch, seq_len, a.shape[-1])
            return jnp.pad(a, ((0, 0), (0, nc * chunk - seq_len), (0, 0))).reshape(batch * nc * chunk, -1)

        def from_seq(a):
            if not padded:
                return a
            return a.reshape(batch, nc * chunk, a.shape[-1])[:, :seq_len].reshape(batch * seq_len, -1)

        def last_rows(buf, proj, col, width):
            tail = proj.reshape(batch, seq_len, proj.shape[-1])[:, -min(seq_len, CONV_W - 1):, col:col + width]
            return jnp.concatenate([buf, tail], axis=1)[:, -(CONV_W - 1):]

        seq = dict(batch=batch, seq_len=seq_len, chunk=chunk)

        proj = _norm_matmul(x, ln_mix[0], w_even)
        proj_seq = to_seq(proj)
        o_dn, s_dn = _dn_mixer(proj_seq.reshape(batch, nc * chunk, EV_WIDTH), _conv_history(conv_dn0[0]),
                               s_dn0[0], dn_conv_w[0], dn_par, dn_norm_w[0].reshape(1, DN_DV),
                               nb=dn_nb, **seq)
        o_dn = o_dn.reshape(batch * nc * chunk, DN_V)
        y_ssm, s_ssm = _ssd_mixer(proj_seq.reshape(batch, nc * chunk, EV_WIDTH), _conv_history(conv_ssm0[0]),
                                  s_ssm0[0], ssm_conv_w[0], ssm_conv_b[0].reshape(1, SSM_CONV_DIM), ssm_par,
                                  ssm_norm_w[0].reshape(1, SSM_INNER), nb=ssd_nb, **seq)
        y_ssm = y_ssm.reshape(batch * nc * chunk, SSM_INNER)
        conv_dn = last_rows(conv_dn0[0], proj, EV_QKV, DN_CONV_DIM)
        conv_ssm = last_rows(conv_ssm0[0], proj, EV_XBC, SSM_CONV_DIM)
        x = _proj_residual(x, [from_seq(o_dn), from_seq(y_ssm)], [w_out_dn, w_out_ssm])
        x = _peer_ffn(x, ln_ffn[0], wq[0], k1[0], k2[0], pu, pv, 0)

        proj = _norm_matmul(x, ln_mix[1], w_odd)
        o_gla, s_gla = _gla_mixer(to_seq(proj).reshape(batch, nc * chunk, OD_WIDTH), s_gla0[0], wg,
                                  gla_b_gate[0].reshape(1, GLA_K), gla_norm_w[0].reshape(1, GLA_DV),
                                  nb=dn_nb, **seq)
        x = _proj_residual(x, [from_seq(o_gla.reshape(batch * nc * chunk, GLA_V))], [w_out_gla])
        x = _peer_ffn(x, ln_ffn[1], wq[1], k1[1], k2[1], pu, pv, 1, final_w=ln_final)

        y = x.reshape(batch, seq_len, D_MODEL)
        return y, conv_dn[None], s_dn[None], conv_ssm[None], s_ssm[None], s_gla[None]

    bp, lp = x_prompt.shape[:2]
    bs, ls = x_sample.shape[:2]

    def zeros_like_state(s):
        return jnp.zeros((s.shape[0], bp) + s.shape[2:], s.dtype)

    prompt = trunk(x_prompt.reshape(bp * lp, D_MODEL), bp, lp, zeros_like_state(state_dn_conv),
                   zeros_like_state(state_dn), zeros_like_state(state_ssm_conv),
                   zeros_like_state(state_ssm), zeros_like_state(state_gla))
    sample = trunk(x_sample.reshape(bs * ls, D_MODEL), bs, ls, state_dn_conv, state_dn,
                   state_ssm_conv, state_ssm, state_gla)
    return (prompt[0], sample[0]) + prompt[1:] + sample[1:]
```
